```python
import jax, jax.numpy as jnp
from jax import lax
import numpy as np

D_MODEL = 2048
BATCH = 1
SEQ = 8192
DEPTH = 1

CHUNK = 64
M_HEADS = 4
M_QK_DIM = 128
M_V_DIM = 256
M_CONV = 4
A_HEADS = 8
A_HEAD_DIM = 128
A_PAST_CHUNKS = 8
A_MAX_REL = 128
D_FF = 5632
FFN_CONV = 3

EPS = 1e-6
NEG = -1e30

M_QK = M_HEADS * M_QK_DIM
M_V = M_HEADS * M_V_DIM
A_W = A_HEADS * A_HEAD_DIM
SPLITS = (M_QK, M_QK, M_V, M_V, M_HEADS, M_HEADS, A_W, A_W, A_W, D_MODEL, D_MODEL)
PROJ_WIDTH = sum(SPLITS)

kernel_name = "hybrid_mlstm_chunkattn_convffn"


def rmsnorm(x, g):
    xf = x.astype(jnp.float32)
    y = xf * lax.rsqrt(jnp.mean(xf * xf, axis=-1, keepdims=True) + EPS)
    return (y * g.astype(jnp.float32)).astype(x.dtype)


def causal_dwconv(x, w, b):
    K = w.shape[0]
    S = x.shape[1]
    xp = jnp.pad(x, ((0, 0), (K - 1, 0), (0, 0)))
    y = b
    for tap in range(K):
        y = y + xp[:, tap:tap + S] * w[tap]
    return y


def to_heads(t, n_heads):
    B, S, _ = t.shape
    return t.reshape(B, S, n_heads, -1).transpose(0, 2, 1, 3)


def from_heads(t):
    B, H, S, d = t.shape
    return t.transpose(0, 2, 1, 3).reshape(B, S, H * d)


def mlstm_chunkwise(q, k, v, li, lf):
    B, H, S, dk = q.shape
    dv = v.shape[-1]
    L = CHUNK
    NC = S // L
    q = q.reshape(B, H, NC, L, dk)
    k = k.reshape(B, H, NC, L, dk)
    v = v.reshape(B, H, NC, L, dv)
    li = li.reshape(B, H, NC, L)
    lf = lf.reshape(B, H, NC, L)
    b = jnp.cumsum(lf, axis=-1)
    b_tot = b[..., -1]
    causal = jnp.tril(jnp.ones((L, L), dtype=bool))
    d_log = jnp.where(causal, b[..., :, None] - b[..., None, :] + li[..., None, :], NEG)
    w_log = b_tot[..., None] - b + li
    m_loc = jnp.max(w_log, axis=-1)
    wk = jnp.exp(w_log - m_loc[..., None])[..., None] * k
    c_loc = jnp.einsum('bhcsk,bhcsv->bhckv', wk, v)
    n_loc = jnp.sum(wk, axis=3)

    def step(carry, inp):
        c_st, n_st, m_st = carry
        c_l, n_l, m_l, bt = inp
        m_new = jnp.maximum(bt + m_st, m_l)
        a = jnp.exp(bt + m_st - m_new)
        cl = jnp.exp(m_l - m_new)
        c_new = a[..., None, None] * c_st + cl[..., None, None] * c_l
        n_new = a[..., None] * n_st + cl[..., None] * n_l
        return (c_new, n_new, m_new), (c_st, n_st, m_st)

    init = (jnp.zeros((B, H, dk, dv), jnp.float32),
            jnp.zeros((B, H, dk), jnp.float32),
            jnp.full((B, H), NEG, jnp.float32))
    xs = (jnp.moveaxis(c_loc, 2, 0), jnp.moveaxis(n_loc, 2, 0),
          jnp.moveaxis(m_loc, 2, 0), jnp.moveaxis(b_tot, 2, 0))
    _, (c_prev, n_prev, m_prev) = lax.scan(step, init, xs)
    c_prev = jnp.moveaxis(c_prev, 0, 2)
    n_prev = jnp.moveaxis(n_prev, 0, 2)
    m_prev = jnp.moveaxis(m_prev, 0, 2)

    inter_log = b + m_prev[..., None]
    m_t = jnp.maximum(jnp.max(d_log, axis=-1), inter_log)
    p = jnp.exp(d_log - m_t[..., None]) * jnp.einsum('bhctk,bhcsk->bhcts', q, k)
    a_inter = jnp.exp(inter_log - m_t)
    num = jnp.einsum('bhcts,bhcsv->bhctv', p, v) + a_inter[..., None] * jnp.einsum('bhctk,bhckv->bhctv', q, c_prev)
    den = jnp.sum(p, axis=-1) + a_inter * jnp.einsum('bhctk,bhck->bhct', q, n_prev)
    h = num / jnp.maximum(jnp.abs(den), jnp.exp(-m_t))[..., None]
    return h.reshape(B, H, S, dv)


def chunk_band_attention(q, k, v, rel_table):
    B, H, S, dh = q.shape
    L = CHUNK
    NC = S // L
    W = A_PAST_CHUNKS + 1
    qc = q.reshape(B, H, NC, L, dh)
    pad = ((0, 0), (0, 0), (A_PAST_CHUNKS, 0), (0, 0), (0, 0))
    kp = jnp.pad(k.reshape(B, H, NC, L, dh), pad)
    vp = jnp.pad(v.reshape(B, H, NC, L, dh), pad)
    idx = jnp.arange(NC)[:, None] + jnp.arange(W)[None, :]
    kb = kp[:, :, idx].reshape(B, H, NC, W * L, dh)
    vb = vp[:, :, idx].reshape(B, H, NC, W * L, dh)
    s = jnp.einsum('bhcqd,bhckd->bhcqk', qc, kb).astype(jnp.float32) * (dh ** -0.5)
    qpos = A_PAST_CHUNKS * L + jnp.arange(L)
    kpos = jnp.arange(W * L)
    rel = jnp.clip(qpos[:, None] - kpos[None, :], -A_MAX_REL, A_MAX_REL) + A_MAX_REL
    bias = rel_table.astype(jnp.float32)[:, rel]
    valid = (jnp.arange(NC)[:, None] - A_PAST_CHUNKS + jnp.arange(W)[None, :]) >= 0
    valid = jnp.repeat(valid, L, axis=1)
    s = jnp.where(valid[None, None, :, None, :], s + bias[None, :, None], NEG)
    p = jax.nn.softmax(s, axis=-1).astype(v.dtype)
    o = jnp.einsum('bhcqk,bhckd->bhcqd', p, vb)
    return o.reshape(B, H, S, dh)


def setup_inputs(seed: int = 0) -> dict:
    key = jax.random.key(seed)
    ks = jax.random.split(key, 24)
    f32 = jnp.float32

    def nrm(k, shape, scale):
        return jax.random.normal(k, shape, f32) * scale

    return {
        "x": nrm(ks[0], (BATCH, SEQ, D_MODEL), 1.0),
        "norm_mix": 1.0 + nrm(ks[1], (DEPTH, D_MODEL), 0.02),
        "w_in": nrm(ks[2], (DEPTH, D_MODEL, PROJ_WIDTH), D_MODEL ** -0.5),
        "conv_qk_w": nrm(ks[3], (DEPTH, M_CONV, 2 * M_QK), M_CONV ** -0.5),
        "conv_qk_b": nrm(ks[4], (DEPTH, 2 * M_QK), 0.02),
        "b_igate": nrm(ks[5], (DEPTH, M_HEADS), 0.1),
        "b_fgate": jnp.linspace(3.0, 6.0, M_HEADS, dtype=f32)[None, :] + nrm(ks[6], (DEPTH, M_HEADS), 0.1),
        "m_norm": 1.0 + nrm(ks[7], (DEPTH, M_V), 0.02),
        "rel_bias": nrm(ks[8], (DEPTH, A_HEADS, 2 * A_MAX_REL + 1), 0.5),
        "gate_bias": nrm(ks[9], (DEPTH, 2, D_MODEL), 0.02),
        "w_branch_m": nrm(ks[10], (DEPTH, M_V, D_MODEL), M_V ** -0.5),
        "w_branch_a": nrm(ks[11], (DEPTH, A_W, D_MODEL), A_W ** -0.5),
        "w_out": nrm(ks[12], (DEPTH, D_MODEL, D_MODEL), D_MODEL ** -0.5),
        "norm_ffn": 1.0 + nrm(ks[13], (DEPTH, D_MODEL), 0.02),
        "w_up": nrm(ks[14], (DEPTH, D_MODEL, 2 * D_FF), D_MODEL ** -0.5),
        "conv_ffn_w": nrm(ks[15], (DEPTH, FFN_CONV, D_FF), FFN_CONV ** -0.5),
        "conv_ffn_b": nrm(ks[16], (DEPTH, D_FF), 0.02),
        "w_down": nrm(ks[17], (DEPTH, D_FF, D_MODEL), D_FF ** -0.5),
        "norm_final": 1.0 + nrm(ks[18], (D_MODEL,), 0.02),
    }


def reference(x, norm_mix, w_in, conv_qk_w, conv_qk_b, b_igate, b_fgate, m_norm, rel_bias,
              gate_bias, w_branch_m, w_branch_a, w_out, norm_ffn, w_up, conv_ffn_w, conv_ffn_b,
              w_down, norm_final):
    f32 = jnp.float32
    split_idx = [int(i) for i in np.cumsum(SPLITS)[:-1]]
    for l in range(DEPTH):
        h = rmsnorm(x, norm_mix[l])
        proj = h @ w_in[l]
        mq, mk, mv, mo, mi, mf, aq, ak, av, g_m, g_a = jnp.split(proj, split_idx, axis=-1)

        qk = jax.nn.silu(causal_dwconv(jnp.concatenate([mq, mk], axis=-1), conv_qk_w[l], conv_qk_b[l]))
        mq, mk = jnp.split(qk, 2, axis=-1)
        q_h = to_heads(mq, M_HEADS).astype(f32)
        k_h = to_heads(mk, M_HEADS).astype(f32) * (M_QK_DIM ** -0.5)
        v_h = to_heads(mv, M_HEADS).astype(f32)
        li = (mi.astype(f32) + b_igate[l].astype(f32)).transpose(0, 2, 1)
        lf = jax.nn.log_sigmoid(mf.astype(f32) + b_fgate[l].astype(f32)).transpose(0, 2, 1)
        hm = mlstm_chunkwise(q_h, k_h, v_h, li, lf)
        hm = hm * lax.rsqrt(jnp.mean(hm * hm, axis=-1, keepdims=True) + EPS)
        hm = from_heads(hm) * m_norm[l].astype(f32)
        hm = (hm * jax.nn.sigmoid(mo.astype(f32))).astype(x.dtype)
        y_m = hm @ w_branch_m[l]

        ha = chunk_band_attention(to_heads(aq, A_HEADS), to_heads(ak, A_HEADS), to_heads(av, A_HEADS), rel_bias[l])
        y_a = from_heads(ha) @ w_branch_a[l]

        merged = jax.nn.sigmoid(g_m + gate_bias[l, 0]) * y_m + jax.nn.sigmoid(g_a + gate_bias[l, 1]) * y_a
        x = x + merged @ w_out[l]

        h = rmsnorm(x, norm_ffn[l])
        u_gate, u_val = jnp.split(h @ w_up[l], 2, axis=-1)
        u_gate = causal_dwconv(u_gate, conv_ffn_w[l], conv_ffn_b[l])
        x = x + (jax.nn.silu(u_gate) * u_val) @ w_down[l]
    return rmsnorm(x, norm_final)
```

```python
import functools

import jax
import jax.numpy as jnp
from jax import lax
from jax.experimental import pallas as pl
from jax.experimental.pallas import tpu as pltpu

D_MODEL = 2048
SEQ = 8192
CHUNK = 64
M_HEADS = 4
M_QK_DIM = 128
M_V_DIM = 256
M_CONV = 4
A_HEADS = 8
A_HEAD_DIM = 128
A_PAST_CHUNKS = 8
A_MAX_REL = 128
D_FF = 5632
FFN_CONV = 3
EPS = 1e-6
NEG = -1e30

M_QK = M_HEADS * M_QK_DIM
M_V = M_HEADS * M_V_DIM
A_W = A_HEADS * A_HEAD_DIM
GATE_COL0 = 2 * M_QK + 2 * M_V
N_GATES = 2 * M_HEADS

LANES = 128
SUBLANES = 8
VMEM_LIMIT = 56 * 1024 * 1024

BF16 = jnp.bfloat16
F32 = jnp.float32


def _params(sem, vmem=VMEM_LIMIT):
    return pltpu.CompilerParams(dimension_semantics=sem, vmem_limit_bytes=vmem)


def _dot(a, b):
    return jnp.dot(a, b, preferred_element_type=F32)


def _dot_nt(a, b):
    return lax.dot_general(a, b, (((1,), (1,)), ((), ())), preferred_element_type=F32)


def _sigmoid(z):
    return 1.0 / (1.0 + jnp.exp(-z))


def _log_sigmoid(z):
    return jnp.minimum(z, 0.0) - jnp.log(1.0 + jnp.exp(-jnp.abs(z)))


def _prenorm_kernel(x_ref, g_ref, wg_ref, wgt_ref, h_ref, gcol_ref, gt_ref):
    xf = x_ref[...]
    y = xf * lax.rsqrt(jnp.mean(xf * xf, axis=-1, keepdims=True) + EPS) * g_ref[...]
    h_ref[...] = y.astype(BF16)
    gcol_ref[...] = jnp.dot(y, wg_ref[...], preferred_element_type=F32,
                            precision=lax.Precision.HIGHEST)
    gt_ref[...] = lax.dot_general(wgt_ref[...], y, (((1,), (1,)), ((), ())),
                                  preferred_element_type=F32,
                                  precision=lax.Precision.HIGHEST)


def _prenorm(x, g, wg, wgt, tm=512):
    s, d = x.shape
    return pl.pallas_call(
        _prenorm_kernel,
        grid=(s // tm,),
        in_specs=[
            pl.BlockSpec((tm, d), lambda i: (i, 0)),
            pl.BlockSpec((1, d), lambda i: (0, 0)),
            pl.BlockSpec((d, LANES), lambda i: (0, 0)),
            pl.BlockSpec((SUBLANES, d), lambda i: (0, 0)),
        ],
        out_specs=[
            pl.BlockSpec((tm, d), lambda i: (i, 0)),
            pl.BlockSpec((tm, LANES), lambda i: (i, 0)),
            pl.BlockSpec((SUBLANES, tm), lambda i: (0, i)),
        ],
        out_shape=[
            jax.ShapeDtypeStruct((s, d), BF16),
            jax.ShapeDtypeStruct((s, LANES), F32),
            jax.ShapeDtypeStruct((SUBLANES, s), F32),
        ],
        compiler_params=_params(("arbitrary",)),
        name="prenorm",
    )(x, g, wg, wgt)


def _proj_kernel(h_ref, w_ref, o_ref):
    o_ref[...] = _dot(h_ref[...], w_ref[...]).astype(o_ref.dtype)


def _proj(h, w, col0, ncols, out_dtype, name, tm=1024, tn=512):
    s, d = h.shape
    off = col0 // tn
    return pl.pallas_call(
        _proj_kernel,
        grid=(s // tm, ncols // tn),
        in_specs=[
            pl.BlockSpec((tm, d), lambda i, j: (i, 0)),
            pl.BlockSpec((d, tn), lambda i, j: (0, off + j)),
        ],
        out_specs=pl.BlockSpec((tm, tn), lambda i, j: (i, j)),
        out_shape=jax.ShapeDtypeStruct((s, ncols), out_dtype),
        compiler_params=_params(("arbitrary", "arbitrary")),
        name=name,
    )(h, w)


def _seg_cumsum(v, axis):
    pos = lax.broadcasted_iota(jnp.int32, v.shape, axis) & (CHUNK - 1)
    shift = 1
    while shift < CHUNK:
        rolled = pltpu.roll(v, shift, axis)
        v = v + jnp.where(pos >= shift, rolled, 0.0)
        shift *= 2
    return v


def _mlstm_kernel(q_ref, k_ref, v_ref, o_ref, gcol_ref, gt_ref, cw_ref, cb_ref, gbr_ref, gbc_ref,
                  mn_ref, out_ref, qk_buf, c_s, m_s, *, tc):
    i = pl.program_id(0)
    halo = SUBLANES

    @pl.when(i == 0)
    def _():
        qk_buf[0:halo, :] = jnp.zeros((halo, 2 * M_QK), F32)
        c_s[...] = jnp.zeros(c_s.shape, F32)
        m_s[...] = jnp.full(m_s.shape, NEG, F32)

    qk_buf[halo:halo + tc, 0:M_QK] = q_ref[...]
    qk_buf[halo:halo + tc, M_QK:2 * M_QK] = k_ref[...]
    acc = jnp.broadcast_to(cb_ref[...], (tc, 2 * M_QK))
    for tap in range(M_CONV):
        lo = halo - (M_CONV - 1) + tap
        acc = acc + qk_buf[lo:lo + tc, :] * cw_ref[tap:tap + 1, :]
    qk_buf[0:halo, :] = qk_buf[tc:tc + halo, :]
    qk = acc * _sigmoid(acc)
    q_all = qk[:, 0:M_QK].astype(BF16)
    k_all = qk[:, M_QK:2 * M_QK] * (M_QK_DIM ** -0.5)

    zt = gt_ref[...] + gbc_ref[:, 0:1]
    bt_all = _seg_cumsum(_log_sigmoid(zt), 1)
    zc = gcol_ref[...] + gbr_ref[0:1, :]
    bc_all = _seg_cumsum(_log_sigmoid(zc), 0)

    row = lax.broadcasted_iota(jnp.int32, (CHUNK, CHUNK), 0)
    col = lax.broadcasted_iota(jnp.int32, (CHUNK, CHUNK), 1)
    causal = col <= row
    ones_aug = jnp.ones((CHUNK, LANES), BF16)

    for h in range(M_HEADS):
        kt_h = k_all[:, h * M_QK_DIM:(h + 1) * M_QK_DIM].T
        m_prev = m_s[h][0:1, 0:1]
        for c in range(tc // CHUNK):
            r0 = c * CHUNK
            rows = slice(r0, r0 + CHUNK)
            vcols = slice(h * M_V_DIM, (h + 1) * M_V_DIM)
            li_t = zt[h:h + 1, rows]
            b_t = bt_all[M_HEADS + h:M_HEADS + h + 1, rows]
            u_t = li_t - b_t
            b_tot = b_t[:, CHUNK - 1:CHUNK]
            b_col = bc_all[rows, M_HEADS + h:M_HEADS + h + 1]

            d_log = jnp.where(causal, b_col + u_t, NEG)
            inter = b_col + m_prev
            m_t = jnp.maximum(jnp.max(d_log, axis=1, keepdims=True), inter)
            q_c = q_all[rows, h * M_QK_DIM:(h + 1) * M_QK_DIM]
            kt_c = kt_h[:, rows]
            p = jnp.exp(d_log - m_t) * _dot(q_c, kt_c.astype(BF16))
            a_inter = jnp.exp(inter - m_t)
            c_prev = c_s[h]
            v_aug = jnp.concatenate([v_ref[rows, vcols].astype(BF16), ones_aug], axis=1)
            na = _dot(p.astype(BF16), v_aug) + a_inter * _dot(q_c, c_prev.astype(BF16))
            num = na[:, 0:M_V_DIM]
            den = na[:, M_V_DIM:M_V_DIM + 1]
            hh = num / jnp.maximum(jnp.abs(den), jnp.exp(-m_t))
            hh = hh * lax.rsqrt(jnp.mean(hh * hh, axis=-1, keepdims=True) + EPS)
            hh = hh * mn_ref[0:1, vcols] * _sigmoid(o_ref[rows, vcols])
            out_ref[rows, vcols] = hh.astype(BF16)

            w_log = b_tot + u_t
            m_loc = jnp.max(w_log, axis=1, keepdims=True)
            wkt = kt_c * jnp.exp(w_log - m_loc)
            c_loc = _dot(wkt.astype(BF16), v_aug)
            m_new = jnp.maximum(b_tot + m_prev, m_loc)
            a = jnp.exp(b_tot + m_prev - m_new)
            cl = jnp.exp(m_loc - m_new)
            c_s[h] = a * c_prev + cl * c_loc
            m_prev = m_new
        m_s[h] = jnp.broadcast_to(m_prev, (SUBLANES, LANES))


def _mlstm(pa, gcol, gt, cw, cb, gbr, gbc, mn, tc=256):
    s = pa.shape[0]
    return pl.pallas_call(
        functools.partial(_mlstm_kernel, tc=tc),
        grid=(s // tc,),
        in_specs=[
            pl.BlockSpec((tc, M_QK), lambda i: (i, 0)),
            pl.BlockSpec((tc, M_QK), lambda i: (i, 1)),
            pl.BlockSpec((tc, M_V), lambda i: (i, (2 * M_QK) // M_V)),
            pl.BlockSpec((tc, M_V), lambda i: (i, (2 * M_QK) // M_V + 1)),
            pl.BlockSpec((tc, LANES), lambda i: (i, 0)),
            pl.BlockSpec((SUBLANES, tc), lambda i: (0, i)),
            pl.BlockSpec((M_CONV, 2 * M_QK), lambda i: (0, 0)),
            pl.BlockSpec((1, 2 * M_QK), lambda i: (0, 0)),
            pl.BlockSpec((SUBLANES, LANES), lambda i: (0, 0)),
            pl.BlockSpec((SUBLANES, LANES), lambda i: (0, 0)),
            pl.BlockSpec((1, M_V), lambda i: (0, 0)),
        ],
        out_specs=pl.BlockSpec((tc, M_V), lambda i: (i, 0)),
        out_shape=jax.ShapeDtypeStruct((s, M_V), BF16),
        scratch_shapes=[
            pltpu.VMEM((SUBLANES + tc, 2 * M_QK), F32),
            pltpu.VMEM((M_HEADS, M_QK_DIM, M_V_DIM + LANES), F32),
            pltpu.VMEM((M_HEADS, SUBLANES, LANES), F32),
        ],
        compiler_params=_params(("arbitrary",)),
        name="mlstm",
    )(pa, pa, pa, pa, gcol, gt, cw, cb, gbr, gbc, mn)


ATT_G = 4
ATT_ROWS = ATT_G * CHUNK
ATT_KBLOCKS = (A_PAST_CHUNKS + ATT_G) // ATT_G


def _attn_kernel(q_ref, k0_ref, k1_ref, k2_ref, v0_ref, v1_ref, v2_ref, bias_ref, out_ref):
    g = pl.program_id(0)
    kcol = lax.broadcasted_iota(jnp.int32, (ATT_ROWS, ATT_KBLOCKS * ATT_ROWS), 1)
    in_seq = kcol >= (ATT_KBLOCKS - 1 - g) * ATT_ROWS
    scale = A_HEAD_DIM ** -0.5
    for h in range(A_HEADS):
        cols = slice(h * A_HEAD_DIM, (h + 1) * A_HEAD_DIM)
        q = q_ref[:, cols]
        s = jnp.concatenate([_dot_nt(q, k0_ref[:, cols]), _dot_nt(q, k1_ref[:, cols]),
                             _dot_nt(q, k2_ref[:, cols])], axis=1)
        s = jnp.where(in_seq, s * scale + bias_ref[h], NEG)
        e = jnp.exp(s - jnp.max(s, axis=1, keepdims=True))
        l = jnp.sum(e, axis=1, keepdims=True)
        p = e.astype(BF16)
        o = (_dot(p[:, 0:ATT_ROWS], v0_ref[:, cols])
             + _dot(p[:, ATT_ROWS:2 * ATT_ROWS], v1_ref[:, cols])
             + _dot(p[:, 2 * ATT_ROWS:3 * ATT_ROWS], v2_ref[:, cols]))
        out_ref[:, cols] = (o / l).astype(BF16)


def _attention(pb, bias4):
    s = pb.shape[0]

    def kv_spec(group, back):
        return pl.BlockSpec((ATT_ROWS, A_W), lambda g: (jnp.maximum(g - back, 0), group))

    return pl.pallas_call(
        _attn_kernel,
        grid=(s // ATT_ROWS,),
        in_specs=[
            pl.BlockSpec((ATT_ROWS, A_W), lambda g: (g, 0)),
            kv_spec(1, 2), kv_spec(1, 1), kv_spec(1, 0),
            kv_spec(2, 2), kv_spec(2, 1), kv_spec(2, 0),
            pl.BlockSpec((A_HEADS, ATT_ROWS, ATT_KBLOCKS * ATT_ROWS), lambda g: (0, 0, 0)),
        ],
        out_specs=pl.BlockSpec((ATT_ROWS, A_W), lambda g: (g, 0)),
        out_shape=jax.ShapeDtypeStruct((s, A_W), BF16),
        compiler_params=_params(("arbitrary",)),
        name="attention",
    )(pb, pb, pb, pb, pb, pb, pb, bias4)


def _band_bias(rel_table):
    w = A_PAST_CHUNKS + 1
    qpos = A_PAST_CHUNKS * CHUNK + jnp.arange(CHUNK)
    kpos = jnp.arange(w * CHUNK)
    rel = jnp.clip(qpos[:, None] - kpos[None, :], -A_MAX_REL, A_MAX_REL) + A_MAX_REL
    bias = rel_table.astype(F32)[:, rel]
    out = jnp.full((A_HEADS, ATT_ROWS, ATT_KBLOCKS * ATT_ROWS), NEG, F32)
    for ci in range(ATT_G):
        out = out.at[:, ci * CHUNK:(ci + 1) * CHUNK, ci * CHUNK:ci * CHUNK + w * CHUNK].set(bias)
    return out


def _merge_kernel(hm_ref, ha_ref, gm_ref, ga_ref, x_ref, wm_ref, wa_ref, wo_ref, gb_ref, nf_ref,
                  x1_ref, h2_ref):
    ym = _dot(hm_ref[...], wm_ref[...])
    ya = _dot(ha_ref[...], wa_ref[...])
    merged = (_sigmoid(gm_ref[...] + gb_ref[0:1, :]) * ym
              + _sigmoid(ga_ref[...] + gb_ref[1:2, :]) * ya)
    x1 = x_ref[...] + _dot(merged.astype(BF16), wo_ref[...])
    x1_ref[...] = x1
    y = x1 * lax.rsqrt(jnp.mean(x1 * x1, axis=-1, keepdims=True) + EPS) * nf_ref[...]
    h2_ref[...] = y.astype(BF16)


def _merge(hm, ha, pc, x, wm, wa, wo, gb, nf, tm=256):
    s, d = x.shape
    const = lambda i: (0, 0)
    resident = pl.Buffered(1)
    return pl.pallas_call(
        _merge_kernel,
        grid=(s // tm,),
        in_specs=[
            pl.BlockSpec((tm, M_V), lambda i: (i, 0)),
            pl.BlockSpec((tm, A_W), lambda i: (i, 0)),
            pl.BlockSpec((tm, d), lambda i: (i, 0)),
            pl.BlockSpec((tm, d), lambda i: (i, 1)),
            pl.BlockSpec((tm, d), lambda i: (i, 0)),
            pl.BlockSpec((M_V, d), const, pipeline_mode=resident),
            pl.BlockSpec((A_W, d), const, pipeline_mode=resident),
            pl.BlockSpec((d, d), const, pipeline_mode=resident),
            pl.BlockSpec((2, d), const),
            pl.BlockSpec((1, d), const),
        ],
        out_specs=[
            pl.BlockSpec((tm, d), lambda i: (i, 0)),
            pl.BlockSpec((tm, d), lambda i: (i, 0)),
        ],
        out_shape=[
            jax.ShapeDtypeStruct((s, d), F32),
            jax.ShapeDtypeStruct((s, d), BF16),
        ],
        compiler_params=_params(("arbitrary",)),
        name="merge",
    )(hm, ha, pc, pc, x, wm, wa, wo, gb, nf)


def _ffn_up_kernel(h_ref, wg_ref, wv_ref, cw_ref, cb_ref, a_ref, w_bf, ug_buf, *, tm, tf):
    m = pl.program_id(1)
    halo = SUBLANES

    @pl.when(m == 0)
    def _():
        w_bf[:, 0:tf] = wg_ref[...].astype(BF16)
        w_bf[:, tf:2 * tf] = wv_ref[...].astype(BF16)
        ug_buf[0:halo, :] = jnp.zeros((halo, tf), F32)

    h = h_ref[...]
    ug = _dot(h, w_bf[:, 0:tf])
    uv = _dot(h, w_bf[:, tf:2 * tf])
    ug_buf[halo:halo + tm, :] = ug
    conv = jnp.broadcast_to(cb_ref[...], (tm, tf))
    for tap in range(FFN_CONV):
        lo = halo - (FFN_CONV - 1) + tap
        conv = conv + ug_buf[lo:lo + tm, :] * cw_ref[tap:tap + 1, :]
    ug_buf[0:halo, :] = ug_buf[tm:tm + halo, :]
    a_ref[...] = (conv * _sigmoid(conv) * uv).astype(BF16)


def _ffn_up(h2, w_up, cw, cb, tm=512, tf=512):
    s, d = h2.shape
    nf = D_FF // tf
    return pl.pallas_call(
        functools.partial(_ffn_up_kernel, tm=tm, tf=tf),
        grid=(nf, s // tm),
        in_specs=[
            pl.BlockSpec((tm, d), lambda j, m: (m, 0)),
            pl.BlockSpec((d, tf), lambda j, m: (0, j)),
            pl.BlockSpec((d, tf), lambda j, m: (0, nf + j)),
            pl.BlockSpec((FFN_CONV, tf), lambda j, m: (0, j)),
            pl.BlockSpec((1, tf), lambda j, m: (0, j)),
        ],
        out_specs=pl.BlockSpec((tm, tf), lambda j, m: (m, j)),
        out_shape=jax.ShapeDtypeStruct((s, D_FF), BF16),
        scratch_shapes=[
            pltpu.VMEM((d, 2 * tf), BF16),
            pltpu.VMEM((SUBLANES + tm, tf), F32),
        ],
        compiler_params=_params(("arbitrary", "arbitrary")),
        name="ffn_up",
    )(h2, w_up, w_up, cw, cb)


def _ffn_down_kernel(a_ref, w_ref, x1_ref, nf_ref, o_ref, acc):
    k = pl.program_id(1)

    @pl.when(k == 0)
    def _():
        acc[...] = x1_ref[...]

    acc[...] += _dot(a_ref[...], w_ref[...])

    @pl.when(k == pl.num_programs(1) - 1)
    def _():
        y = acc[...]
        o_ref[...] = y * lax.rsqrt(jnp.mean(y * y, axis=-1, keepdims=True) + EPS) * nf_ref[...]


def _ffn_down(a, w_down, x1, nf, tm=512, tk=512):
    s, d = x1.shape
    return pl.pallas_call(
        _ffn_down_kernel,
        grid=(s // tm, D_FF // tk),
        in_specs=[
            pl.BlockSpec((tm, tk), lambda i, k: (i, k)),
            pl.BlockSpec((tk, d), lambda i, k: (k, 0)),
            pl.BlockSpec((tm, d), lambda i, k: (i, 0)),
            pl.BlockSpec((1, d), lambda i, k: (0, 0)),
        ],
        out_specs=pl.BlockSpec((tm, d), lambda i, k: (i, 0)),
        out_shape=jax.ShapeDtypeStruct((s, d), F32),
        scratch_shapes=[pltpu.VMEM((tm, d), F32)],
        compiler_params=_params(("arbitrary", "arbitrary")),
        name="ffn_down",
    )(a, w_down, x1, nf)


def kernel(x, norm_mix, w_in, conv_qk_w, conv_qk_b, b_igate, b_fgate, m_norm, rel_bias, gate_bias,
           w_branch_m, w_branch_a, w_out, norm_ffn, w_up, conv_ffn_w, conv_ffn_b, w_down, norm_final):
    batch, seq, d = x.shape
    depth = w_in.shape[0]
    assert (batch, seq, d, depth) == (1, SEQ, D_MODEL, 1)
    xs = x[0]
    l = 0

    w_main = jnp.concatenate([w_in[l][:, :GATE_COL0], w_in[l][:, GATE_COL0 + N_GATES:]],
                             axis=1).astype(BF16)
    w_g = w_in[l][:, GATE_COL0:GATE_COL0 + N_GATES]
    wg_pad = jnp.pad(w_g, ((0, 0), (0, LANES - N_GATES)))
    wg_t = w_g.T
    gate_b = jnp.concatenate([b_igate[l], b_fgate[l]])
    gb_row = jnp.broadcast_to(jnp.pad(gate_b, (0, LANES - N_GATES))[None, :], (SUBLANES, LANES))
    gb_col = jnp.broadcast_to(gate_b[:, None], (SUBLANES, LANES))
    bias4 = _band_bias(rel_bias[l])

    h, gcol, gt = _prenorm(xs, norm_mix[l][None, :], wg_pad, wg_t)
    na = 2 * M_QK + 2 * M_V
    pa = _proj(h, w_main, 0, na, F32, "proj_mlstm")
    pb = _proj(h, w_main, na, 3 * A_W, BF16, "proj_attn")
    pc = _proj(h, w_main, na + 3 * A_W, 2 * D_MODEL, F32, "proj_gates")

    hm = _mlstm(pa, gcol, gt, conv_qk_w[l], conv_qk_b[l][None, :], gb_row, gb_col,
                m_norm[l][None, :])
    ha = _attention(pb, bias4)
    x1, h2 = _merge(hm, ha, pc, xs, w_branch_m[l].astype(BF16), w_branch_a[l].astype(BF16),
                    w_out[l].astype(BF16), gate_bias[l], norm_ffn[l][None, :])
    a = _ffn_up(h2, w_up[l], conv_ffn_w[l], conv_ffn_b[l][None, :])
    out = _ffn_down(a, w_down[l].astype(BF16), x1, norm_final[None, :])
    return out[None]
```

```python
import functools
import math

import jax
import jax.numpy as jnp
from jax import lax
from jax.experimental import pallas as pl
from jax.experimental.pallas import tpu as pltpu

D_MODEL = 2048
SEQ = 8192
CHUNK = 64
M_HEADS = 4
M_QK_DIM = 128
M_V_DIM = 256
M_CONV = 4
A_HEADS = 8
A_HEAD_DIM = 128
A_PAST_CHUNKS = 8
A_MAX_REL = 128
D_FF = 5632
FFN_CONV = 3
EPS = 1e-6
NEG = -1e30
LOG2E = math.log2(math.e)

M_QK = M_HEADS * M_QK_DIM
M_V = M_HEADS * M_V_DIM
A_W = A_HEADS * A_HEAD_DIM
GATE_COL0 = 2 * M_QK + 2 * M_V
N_GATES = 2 * M_HEADS

LANES = 128
SUBLANES = 8
VMEM_LIMIT = 56 * 1024 * 1024

BF16 = jnp.bfloat16
F32 = jnp.float32


def _params(sem, vmem=VMEM_LIMIT):
    return pltpu.CompilerParams(dimension_semantics=sem, vmem_limit_bytes=vmem)


def _dot(a, b):
    return jnp.dot(a, b, preferred_element_type=F32)


def _dot_nt(a, b):
    return lax.dot_general(a, b, (((1,), (1,)), ((), ())), preferred_element_type=F32)


def _sigmoid(z):
    return 0.5 * jnp.tanh(0.5 * z) + 0.5


def _log_sigmoid(z):
    return jnp.minimum(z, 0.0) - jnp.log(1.0 + jnp.exp(-jnp.abs(z)))


def _rms_scale(y):
    return lax.rsqrt(jnp.mean(y * y, axis=-1, keepdims=True) + EPS)


def _prenorm_kernel(x_ref, g_ref, wg_ref, h_ref, gt_ref):
    xf = x_ref[...]
    hb = (xf * _rms_scale(xf) * g_ref[...]).astype(BF16)
    h_ref[...] = hb
    gcol = _dot(hb, wg_ref[...])
    gt_ref[...] = gcol.T[0:SUBLANES, :]


def _prenorm(x, g, wg, tm=512):
    s, d = x.shape
    return pl.pallas_call(
        _prenorm_kernel,
        grid=(s // tm,),
        in_specs=[
            pl.BlockSpec((tm, d), lambda i: (i, 0)),
            pl.BlockSpec((1, d), lambda i: (0, 0)),
            pl.BlockSpec((d, LANES), lambda i: (0, 0)),
        ],
        out_specs=[
            pl.BlockSpec((tm, d), lambda i: (i, 0)),
            pl.BlockSpec((SUBLANES, tm), lambda i: (0, i)),
        ],
        out_shape=[
            jax.ShapeDtypeStruct((s, d), BF16),
            jax.ShapeDtypeStruct((SUBLANES, s), F32),
        ],
        compiler_params=_params(("arbitrary",)),
        name="prenorm",
    )(x, g, wg)


def _proj_kernel(*refs, shift, tn):
    if shift:
        h_ref, wm_ref, wx_ref, o_ref, w_bf = refs
    else:
        h_ref, wm_ref, o_ref, w_bf = refs

    @pl.when(pl.program_id(1) == 0)
    def _():
        if shift:
            w_bf[0:tn - shift, :] = wm_ref[shift:tn, :].astype(BF16)
            w_bf[tn - shift:tn, :] = wx_ref[...].astype(BF16)
        else:
            w_bf[...] = wm_ref[...].astype(BF16)

    o_ref[...] = _dot_nt(h_ref[...], w_bf[...]).astype(o_ref.dtype)


def _proj(h, wt, col0, ncols, out_dtype, name, tm=1024, tn=1024):
    s, d = h.shape
    shift = col0 % tn
    base = col0 - shift
    assert shift % SUBLANES == 0 and ncols % tn == 0
    in_specs = [
        pl.BlockSpec((tm, d), lambda j, m: (m, 0)),
        pl.BlockSpec((tn, d), lambda j, m: (base // tn + j, 0)),
    ]
    args = [h, wt]
    if shift:
        in_specs.append(pl.BlockSpec((shift, d), lambda j, m: ((base + (j + 1) * tn) // shift, 0)))
        args.append(wt)
    return pl.pallas_call(
        functools.partial(_proj_kernel, shift=shift, tn=tn),
        grid=(ncols // tn, s // tm),
        in_specs=in_specs,
        out_specs=pl.BlockSpec((tm, tn), lambda j, m: (m, j)),
        out_shape=jax.ShapeDtypeStruct((s, ncols), out_dtype),
        scratch_shapes=[pltpu.VMEM((tn, d), BF16)],
        compiler_params=_params(("arbitrary", "arbitrary")),
        name=name,
    )(*args)


MLSTM_BLOCK = 256


def _cumsum_lanes(v):
    n = v.shape[-1]
    pos = lax.broadcasted_iota(jnp.int32, v.shape, v.ndim - 1)
    shift = 1
    while shift < n:
        v = v + jnp.where(pos >= shift, pltpu.roll(v, shift, v.ndim - 1), 0.0)
        shift *= 2
    return v


def _mlstm_kernel(q_ref, k_ref, v_ref, o_ref, gt_ref, cw_ref, cb_ref, gbc_ref, mn_ref, out_ref,
                  qk_buf, c_s, m_s):
    i = pl.program_id(0)
    halo = SUBLANES
    tc = MLSTM_BLOCK

    @pl.when(i == 0)
    def _():
        qk_buf[0:halo, :] = jnp.zeros((halo, 2 * M_QK), F32)
        c_s[...] = jnp.zeros(c_s.shape, F32)
        m_s[...] = jnp.full(m_s.shape, NEG, F32)

    qk_buf[halo:halo + tc, 0:M_QK] = q_ref[...]
    qk_buf[halo:halo + tc, M_QK:2 * M_QK] = k_ref[...]
    acc = jnp.broadcast_to(cb_ref[...], (tc, 2 * M_QK))
    for tap in range(M_CONV):
        lo = halo - (M_CONV - 1) + tap
        acc = acc + qk_buf[lo:lo + tc, :] * cw_ref[tap:tap + 1, :]
    qk_buf[0:halo, :] = qk_buf[tc:tc + halo, :]
    qk = acc * _sigmoid(acc)
    q_all = qk[:, 0:M_QK].astype(BF16)
    k_all = qk[:, M_QK:2 * M_QK] * (M_QK_DIM ** -0.5)

    zt = gt_ref[...] + gbc_ref[:, 0:1]
    lf_all = _log_sigmoid(zt)
    b_all = _cumsum_lanes(lf_all)

    row = lax.broadcasted_iota(jnp.int32, (tc, tc), 0)
    col = lax.broadcasted_iota(jnp.int32, (tc, tc), 1)
    causal = col <= row
    ones_aug = jnp.ones((tc, LANES), BF16)

    for h in range(M_HEADS):
        qcols = slice(h * M_QK_DIM, (h + 1) * M_QK_DIM)
        vcols = slice(h * M_V_DIM, (h + 1) * M_V_DIM)
        f = M_HEADS + h
        q_h = q_all[:, qcols]
        kt = k_all[:, qcols].T
        m_prev = m_s[h][0:1, 0:1]
        c_prev = c_s[h]

        u = zt[h:h + 1, :] - b_all[f:f + 1, :]
        b_tot = b_all[f:f + 1, tc - 1:tc]
        um = jnp.where(causal, u, NEG)
        g = jnp.maximum(jnp.max(um, axis=1, keepdims=True), m_prev)
        b_col = jnp.sum(jnp.where(causal, lf_all[f:f + 1, :], 0.0), axis=1, keepdims=True)
        p = jnp.exp(um - g) * _dot(q_h, kt.astype(BF16))
        a_inter = jnp.exp(m_prev - g)
        v_aug = jnp.concatenate([v_ref[:, vcols].astype(BF16), ones_aug], axis=1)
        na = _dot(p.astype(BF16), v_aug) + a_inter * _dot(q_h, c_prev.astype(BF16))
        num = na[:, 0:M_V_DIM]
        den = na[:, M_V_DIM:M_V_DIM + 1]
        hh = num / jnp.maximum(jnp.abs(den), jnp.exp(-(b_col + g)))
        hh = hh * _rms_scale(hh) * mn_ref[0:1, vcols] * _sigmoid(o_ref[:, vcols])
        out_ref[:, vcols] = hh.astype(BF16)

        w_log = b_tot + u
        m_loc = jnp.max(w_log, axis=1, keepdims=True)
        c_loc = _dot((kt * jnp.exp(w_log - m_loc)).astype(BF16), v_aug)
        m_new = jnp.maximum(b_tot + m_prev, m_loc)
        c_s[h] = jnp.exp(b_tot + m_prev - m_new) * c_prev + jnp.exp(m_loc - m_new) * c_loc
        m_s[h] = jnp.broadcast_to(m_new, (SUBLANES, LANES))


def _mlstm(pa, gt, cw, cb, gbc, mn):
    s = pa.shape[0]
    tc = MLSTM_BLOCK
    return pl.pallas_call(
        _mlstm_kernel,
        grid=(s // tc,),
        in_specs=[
            pl.BlockSpec((tc, M_QK), lambda i: (i, 0)),
            pl.BlockSpec((tc, M_QK), lambda i: (i, 1)),
            pl.BlockSpec((tc, M_V), lambda i: (i, (2 * M_QK) // M_V)),
            pl.BlockSpec((tc, M_V), lambda i: (i, (2 * M_QK) // M_V + 1)),
            pl.BlockSpec((SUBLANES, tc), lambda i: (0, i)),
            pl.BlockSpec((M_CONV, 2 * M_QK), lambda i: (0, 0)),
            pl.BlockSpec((1, 2 * M_QK), lambda i: (0, 0)),
            pl.BlockSpec((SUBLANES, LANES), lambda i: (0, 0)),
            pl.BlockSpec((1, M_V), lambda i: (0, 0)),
        ],
        out_specs=pl.BlockSpec((tc, M_V), lambda i: (i, 0)),
        out_shape=jax.ShapeDtypeStruct((s, M_V), BF16),
        scratch_shapes=[
            pltpu.VMEM((SUBLANES + tc, 2 * M_QK), F32),
            pltpu.VMEM((M_HEADS, M_QK_DIM, M_V_DIM + LANES), F32),
            pltpu.VMEM((M_HEADS, SUBLANES, LANES), F32),
        ],
        compiler_params=_params(("arbitrary",)),
        name="mlstm",
    )(pa, pa, pa, pa, gt, cw, cb, gbc, mn)


ATT_G = 4
ATT_ROWS = ATT_G * CHUNK
ATT_KBLOCKS = (A_PAST_CHUNKS + ATT_G) // ATT_G


def _attn_kernel(q_ref, k0_ref, k1_ref, k2_ref, v0_ref, v1_ref, v2_ref, bias_ref, out_ref):
    g = pl.program_id(0)
    scale2 = (A_HEAD_DIM ** -0.5) * LOG2E

    def heads(mask_start):
        if mask_start:
            kcol = lax.broadcasted_iota(jnp.int32, (ATT_ROWS, ATT_KBLOCKS * ATT_ROWS), 1)
            in_seq = kcol >= (ATT_KBLOCKS - 1 - g) * ATT_ROWS
        for h in range(A_HEADS):
            cols = slice(h * A_HEAD_DIM, (h + 1) * A_HEAD_DIM)
            q = q_ref[:, cols]
            s = jnp.concatenate([_dot_nt(q, k0_ref[:, cols]), _dot_nt(q, k1_ref[:, cols]),
                                 _dot_nt(q, k2_ref[:, cols])], axis=1)
            s = s * scale2 + bias_ref[h]
            if mask_start:
                s = jnp.where(in_seq, s, NEG)
            e = jnp.exp2(s - jnp.max(s, axis=1, keepdims=True))
            l = jnp.sum(e, axis=1, keepdims=True)
            p = e.astype(BF16)
            o = (_dot(p[:, 0:ATT_ROWS], v0_ref[:, cols])
                 + _dot(p[:, ATT_ROWS:2 * ATT_ROWS], v1_ref[:, cols])
                 + _dot(p[:, 2 * ATT_ROWS:3 * ATT_ROWS], v2_ref[:, cols]))
            out_ref[:, cols] = (o / l).astype(BF16)

    @pl.when(g < ATT_KBLOCKS - 1)
    def _():
        heads(True)

    @pl.when(g >= ATT_KBLOCKS - 1)
    def _():
        heads(False)


def _attention(pb, bias4):
    s = pb.shape[0]

    def kv_spec(group, back):
        return pl.BlockSpec((ATT_ROWS, A_W), lambda g: (jnp.maximum(g - back, 0), group))

    return pl.pallas_call(
        _attn_kernel,
        grid=(s // ATT_ROWS,),
        in_specs=[
            pl.BlockSpec((ATT_ROWS, A_W), lambda g: (g, 0)),
            kv_spec(1, 2), kv_spec(1, 1), kv_spec(1, 0),
            kv_spec(2, 2), kv_spec(2, 1), kv_spec(2, 0),
            pl.BlockSpec((A_HEADS, ATT_ROWS, ATT_KBLOCKS * ATT_ROWS), lambda g: (0, 0, 0)),
        ],
        out_specs=pl.BlockSpec((ATT_ROWS, A_W), lambda g: (g, 0)),
        out_shape=jax.ShapeDtypeStruct((s, A_W), BF16),
        compiler_params=_params(("arbitrary",)),
        name="attention",
    )(pb, pb, pb, pb, pb, pb, pb, bias4)


def _band_bias(rel_table):
    hds = rel_table.shape[0]
    wcols = ATT_KBLOCKS * ATT_ROWS
    far = A_PAST_CHUNKS * CHUNK - A_MAX_REL + CHUNK
    tab = rel_table.astype(F32)
    e = jnp.concatenate([jnp.broadcast_to(tab[:, 2 * A_MAX_REL:], (hds, far)),
                         tab[:, A_MAX_REL - CHUNK + 1:2 * A_MAX_REL][:, ::-1]], axis=1)
    period = ATT_ROWS + wcols
    lead = ATT_ROWS - CHUNK
    e_pad = jnp.pad(e, ((0, 0), (lead, period - lead - e.shape[1])))
    tiled = jnp.broadcast_to(e_pad[:, None, :], (hds, ATT_ROWS, period)).reshape(hds, -1)
    skew = tiled[:, :ATT_ROWS * (period - 1)].reshape(hds, ATT_ROWS, period - 1)
    toep = skew[:, :, ATT_ROWS - 1:ATT_ROWS - 1 + wcols]
    qc = jnp.arange(ATT_ROWS)[:, None] // CHUNK
    kc = jnp.arange(wcols)[None, :] // CHUNK
    band = (kc >= qc) & (kc <= qc + A_PAST_CHUNKS)
    return jnp.where(band[None], toep * LOG2E, NEG)


def _merge_kernel(hm_ref, ha_ref, gm_ref, ga_ref, x_ref, wm_ref, wa_ref, wo_ref, gb_ref, nf_ref,
                  x1_ref, h2_ref):
    ym = _dot(hm_ref[...], wm_ref[...])
    ya = _dot(ha_ref[...], wa_ref[...])
    merged = (_sigmoid(gm_ref[...].astype(F32) + gb_ref[0:1, :]) * ym
              + _sigmoid(ga_ref[...].astype(F32) + gb_ref[1:2, :]) * ya)
    x1 = x_ref[...] + _dot(merged.astype(BF16), wo_ref[...])
    x1_ref[...] = x1
    h2_ref[...] = (x1 * _rms_scale(x1) * nf_ref[...]).astype(BF16)


def _merge(hm, ha, pc, x, wm, wa, wo, gb, nf, tm=256):
    s, d = x.shape
    const = lambda i: (0, 0)
    resident = pl.Buffered(1)
    return pl.pallas_call(
        _merge_kernel,
        grid=(s // tm,),
        in_specs=[
            pl.BlockSpec((tm, M_V), lambda i: (i, 0)),
            pl.BlockSpec((tm, A_W), lambda i: (i, 0)),
            pl.BlockSpec((tm, d), lambda i: (i, 0)),
            pl.BlockSpec((tm, d), lambda i: (i, 1)),
            pl.BlockSpec((tm, d), lambda i: (i, 0)),
            pl.BlockSpec((M_V, d), const, pipeline_mode=resident),
            pl.BlockSpec((A_W, d), const, pipeline_mode=resident),
            pl.BlockSpec((d, d), const, pipeline_mode=resident),
            pl.BlockSpec((2, d), const),
            pl.BlockSpec((1, d), const),
        ],
        out_specs=[
            pl.BlockSpec((tm, d), lambda i: (i, 0)),
            pl.BlockSpec((tm, d), lambda i: (i, 0)),
        ],
        out_shape=[
            jax.ShapeDtypeStruct((s, d), F32),
            jax.ShapeDtypeStruct((s, d), BF16),
        ],
        compiler_params=_params(("arbitrary",)),
        name="merge",
    )(hm, ha, pc, pc, x, wm, wa, wo, gb, nf)


FFN_SUB_ROWS = 256


def _ffn_up_kernel(h_ref, wg_ref, wv_ref, cw_ref, cb_ref, a_ref, w_bf, ug_buf, *, tm, tf):
    halo = SUBLANES
    rs = FFN_SUB_ROWS

    @pl.when(pl.program_id(1) == 0)
    def _():
        w_bf[:, 0:tf] = wg_ref[...].astype(BF16)
        w_bf[:, tf:2 * tf] = wv_ref[...].astype(BF16)
        ug_buf[0:halo, :] = jnp.zeros((halo, tf), F32)

    for r in range(tm // rs):
        h = h_ref[r * rs:(r + 1) * rs, :]
        ug = _dot(h, w_bf[:, 0:tf])
        uv = _dot(h, w_bf[:, tf:2 * tf])
        base = halo + r * rs
        ug_buf[base:base + rs, :] = ug
        conv = cb_ref[...] + ug * cw_ref[FFN_CONV - 1:FFN_CONV, :]
        for tap in range(FFN_CONV - 1):
            lo = base - (FFN_CONV - 1) + tap
            conv = conv + ug_buf[lo:lo + rs, :] * cw_ref[tap:tap + 1, :]
        a_ref[r * rs:(r + 1) * rs, :] = (conv * _sigmoid(conv) * uv).astype(BF16)
    ug_buf[0:halo, :] = ug_buf[tm:tm + halo, :]


def _ffn_up(h2, w_up, cw, cb, tm=1024, tf=512):
    s, d = h2.shape
    nf = D_FF // tf
    return pl.pallas_call(
        functools.partial(_ffn_up_kernel, tm=tm, tf=tf),
        grid=(nf, s // tm),
        in_specs=[
            pl.BlockSpec((tm, d), lambda j, m: (m, 0)),
            pl.BlockSpec((d, tf), lambda j, m: (0, j)),
            pl.BlockSpec((d, tf), lambda j, m: (0, nf + j)),
            pl.BlockSpec((FFN_CONV, tf), lambda j, m: (0, j)),
            pl.BlockSpec((1, tf), lambda j, m: (0, j)),
        ],
        out_specs=pl.BlockSpec((tm, tf), lambda j, m: (m, j)),
        out_shape=jax.ShapeDtypeStruct((s, D_FF), BF16),
        scratch_shapes=[
            pltpu.VMEM((d, 2 * tf), BF16),
            pltpu.VMEM((SUBLANES + tm, tf), F32),
        ],
        compiler_params=_params(("arbitrary", "arbitrary")),
        name="ffn_up",
    )(h2, w_up, w_up, cw, cb)


def _ffn_down_kernel(a_ref, w_ref, x1_ref, nf_ref, o_ref):
    y = x1_ref[...] + _dot(a_ref[...], w_ref[...])
    o_ref[...] = y * _rms_scale(y) * nf_ref[...]


def _ffn_down(a, w_down, x1, nf, tm=256):
    s, d = x1.shape
    dff = a.shape[1]
    return pl.pallas_call(
        _ffn_down_kernel,
        grid=(s // tm,),
        in_specs=[
            pl.BlockSpec((tm, dff), lambda i: (i, 0)),
            pl.BlockSpec((dff, d), lambda i: (0, 0), pipeline_mode=pl.Buffered(1)),
            pl.BlockSpec((tm, d), lambda i: (i, 0)),
            pl.BlockSpec((1, d), lambda i: (0, 0)),
        ],
        out_specs=pl.BlockSpec((tm, d), lambda i: (i, 0)),
        out_shape=jax.ShapeDtypeStruct((s, d), F32),
        compiler_params=_params(("arbitrary",)),
        name="ffn_down",
    )(a, w_down, x1, nf)


def kernel(x, norm_mix, w_in, conv_qk_w, conv_qk_b, b_igate, b_fgate, m_norm, rel_bias, gate_bias,
           w_branch_m, w_branch_a, w_out, norm_ffn, w_up, conv_ffn_w, conv_ffn_b, w_down, norm_final):
    batch, seq, d = x.shape
    depth = w_in.shape[0]
    assert (batch, seq, d, depth) == (1, SEQ, D_MODEL, 1)
    xs = x[0]
    l = 0

    w_in_t = jnp.swapaxes(w_in[l], 0, 1)
    w_g = jnp.pad(w_in_t[GATE_COL0:GATE_COL0 + N_GATES, :].T, ((0, 0), (0, LANES - N_GATES)))
    gate_b = jnp.concatenate([b_igate[l], b_fgate[l]])
    gb_col = jnp.broadcast_to(gate_b[:, None], (SUBLANES, LANES))
    bias4 = _band_bias(rel_bias[l])

    h, gt = _prenorm(xs, norm_mix[l][None, :], w_g.astype(BF16))
    na = 2 * M_QK + 2 * M_V
    attn0 = GATE_COL0 + N_GATES
    pa = _proj(h, w_in_t, 0, na, F32, "proj_mlstm")
    pb = _proj(h, w_in_t, attn0, 3 * A_W, BF16, "proj_attn")
    pc = _proj(h, w_in_t, attn0 + 3 * A_W, 2 * D_MODEL, BF16, "proj_gates")

    hm = _mlstm(pa, gt, conv_qk_w[l], conv_qk_b[l][None, :], gb_col, m_norm[l][None, :])
    ha = _attention(pb, bias4)
    x1, h2 = _merge(hm, ha, pc, xs, w_branch_m[l].astype(BF16), w_branch_a[l].astype(BF16),
                    w_out[l].astype(BF16), gate_bias[l], norm_ffn[l][None, :])
    a = _ffn_up(h2, w_up[l], conv_ffn_w[l], conv_ffn_b[l][None, :])
    out = _ffn_down(a, w_down[l].astype(BF16), x1, norm_final[None, :])
    return out[None]
```

```python
import functools
import math
from typing import Callable, NamedTuple

import jax
import jax.numpy as jnp
from jax import lax
from jax.experimental import pallas as pl
from jax.experimental.pallas import tpu as pltpu

D_MODEL = 2048
SEQ = 8192
CHUNK = 64
M_HEADS = 4
M_QK_DIM = 128
M_V_DIM = 256
M_CONV = 4
A_HEADS = 8
A_HEAD_DIM = 128
A_PAST_CHUNKS = 8
A_MAX_REL = 128
D_FF = 5632
FFN_CONV = 3
EPS = 1e-6
NEG = -1e30
LOG2E = math.log2(math.e)

M_QK = M_HEADS * M_QK_DIM
M_V = M_HEADS * M_V_DIM
A_W = A_HEADS * A_HEAD_DIM
GATE_COL0 = 2 * M_QK + 2 * M_V
N_GATES = 2 * M_HEADS

LANES = 128
SUBLANES = 8
VMEM_LIMIT = 56 * 1024 * 1024

BF16 = jnp.bfloat16
F32 = jnp.float32


def _params(sem, vmem=VMEM_LIMIT):
    return pltpu.CompilerParams(dimension_semantics=sem, vmem_limit_bytes=vmem)


def _dot(a, b):
    return jnp.dot(a, b, preferred_element_type=F32)


def _dot_nt(a, b):
    return lax.dot_general(a, b, (((1,), (1,)), ((), ())), preferred_element_type=F32)


def _sigmoid(z):
    return 0.5 * jnp.tanh(0.5 * z) + 0.5


def _log_sigmoid(z):
    return jnp.minimum(z, 0.0) - jnp.log(1.0 + jnp.exp(-jnp.abs(z)))


def _rms_scale(y):
    return lax.rsqrt(jnp.mean(y * y, axis=-1, keepdims=True) + EPS)


def _prenorm_kernel(x_ref, g_ref, wg_ref, h_ref, gt_ref):
    xf = x_ref[...]
    hb = (xf * _rms_scale(xf) * g_ref[...]).astype(BF16)
    h_ref[...] = hb
    gcol = _dot(hb, wg_ref[...])
    gt_ref[...] = gcol.T[0:SUBLANES, :]


def _prenorm(x, g, wg, tm=512):
    s, d = x.shape
    return pl.pallas_call(
        _prenorm_kernel,
        grid=(s // tm,),
        in_specs=[
            pl.BlockSpec((tm, d), lambda i: (i, 0)),
            pl.BlockSpec((1, d), lambda i: (0, 0)),
            pl.BlockSpec((d, LANES), lambda i: (0, 0)),
        ],
        out_specs=[
            pl.BlockSpec((tm, d), lambda i: (i, 0)),
            pl.BlockSpec((SUBLANES, tm), lambda i: (0, i)),
        ],
        out_shape=[
            jax.ShapeDtypeStruct((s, d), BF16),
            jax.ShapeDtypeStruct((SUBLANES, s), F32),
        ],
        compiler_params=_params(("arbitrary",)),
        name="prenorm",
    )(x, g, wg)


class _Spec(NamedTuple):
    block: tuple
    index: Callable
    mode: object = None


class _Side(NamedTuple):
    body: Callable
    init: Callable
    args: tuple
    in_specs: tuple
    out_specs: tuple
    out_shapes: tuple
    scratch: tuple
    steps: int


PROJ_SLICE = 256


def _interleave(i, n_units, emit, n_parts):
    for k in range(i * n_parts // n_units, (i + 1) * n_parts // n_units):
        emit(k)


def _proj_kernel(*refs, shift, tn, side):
    n_w = 3 if shift else 2
    n_si = len(side.args) if side else 0
    n_so = len(side.out_specs) if side else 0
    h_ref, wm_ref = refs[0], refs[1]
    side_in = refs[n_w:n_w + n_si]
    o_ref = refs[n_w + n_si]
    side_out = refs[n_w + n_si + 1:n_w + n_si + 1 + n_so]
    w_bf = refs[n_w + n_si + 1 + n_so]
    side_scratch = refs[n_w + n_si + 2 + n_so:]

    @pl.when(pl.program_id(1) == 0)
    def _():
        if shift:
            w_bf[0:tn - shift, :] = wm_ref[shift:tn, :].astype(BF16)
            w_bf[tn - shift:tn, :] = refs[2][...].astype(BF16)
        else:
            w_bf[...] = wm_ref[...].astype(BF16)

    step = pl.program_id(0) * pl.num_programs(1) + pl.program_id(1)
    if side and side.init:
        side.init(step, side_scratch)

    def emit(k):
        cols = slice(k * PROJ_SLICE, (k + 1) * PROJ_SLICE)
        o_ref[:, cols] = _dot_nt(h_ref[...], w_bf[cols, :]).astype(o_ref.dtype)

    n_parts = tn // PROJ_SLICE
    if side:
        side.body(step, side_in, side_out, side_scratch, emit, n_parts)
    else:
        for k in range(n_parts):
            emit(k)


def _proj(h, wt, col0, ncols, out_dtype, name, tm=1024, tn=1024, side=None):
    s, d = h.shape
    shift = col0 % tn
    base = col0 - shift
    n_m = s // tm
    assert shift % SUBLANES == 0 and ncols % tn == 0
    assert side is None or side.steps == (ncols // tn) * n_m

    def flat(spec):
        return pl.BlockSpec(spec.block, lambda j, m: spec.index(j * n_m + m),
                            pipeline_mode=spec.mode)

    in_specs = [
        pl.BlockSpec((tm, d), lambda j, m: (m, 0)),
        pl.BlockSpec((tn, d), lambda j, m: (base // tn + j, 0)),
    ]
    args = [h, wt]
    if shift:
        in_specs.append(pl.BlockSpec((shift, d), lambda j, m: ((base + (j + 1) * tn) // shift, 0)))
        args.append(wt)
    out_specs = [pl.BlockSpec((tm, tn), lambda j, m: (m, j))]
    out_shapes = [jax.ShapeDtypeStruct((s, ncols), out_dtype)]
    scratch = [pltpu.VMEM((tn, d), BF16)]
    if side:
        in_specs += [flat(sp) for sp in side.in_specs]
        args += list(side.args)
        out_specs += [flat(sp) for sp in side.out_specs]
        out_shapes += list(side.out_shapes)
        scratch += list(side.scratch)
    outs = pl.pallas_call(
        functools.partial(_proj_kernel, shift=shift, tn=tn, side=side),
        grid=(ncols // tn, n_m),
        in_specs=in_specs,
        out_specs=out_specs,
        out_shape=out_shapes,
        scratch_shapes=scratch,
        compiler_params=_params(("arbitrary", "arbitrary")),
        name=name,
    )(*args)
    return outs if side else outs[0]


MLSTM_BLOCK = 256


def _cumsum_lanes(v):
    n = v.shape[-1]
    pos = lax.broadcasted_iota(jnp.int32, v.shape, v.ndim - 1)
    shift = 1
    while shift < n:
        v = v + jnp.where(pos >= shift, pltpu.roll(v, shift, v.ndim - 1), 0.0)
        shift *= 2
    return v


def _mlstm_body(step, in_refs, out_refs, scratch_refs, emit, n_parts):
    q_ref, k_ref, v_ref, o_ref, gt_ref, cw_ref, cb_ref, gbc_ref, mn_ref = in_refs
    (out_ref,) = out_refs
    qk_buf, c_s, m_s = scratch_refs
    del step
    halo = SUBLANES
    tc = MLSTM_BLOCK

    qk_buf[halo:halo + tc, 0:M_QK] = q_ref[...]
    qk_buf[halo:halo + tc, M_QK:2 * M_QK] = k_ref[...]
    acc = jnp.broadcast_to(cb_ref[...], (tc, 2 * M_QK))
    for tap in range(M_CONV):
        lo = halo - (M_CONV - 1) + tap
        acc = acc + qk_buf[lo:lo + tc, :] * cw_ref[tap:tap + 1, :]
    qk_buf[0:halo, :] = qk_buf[tc:tc + halo, :]
    qk = acc * _sigmoid(acc)
    q_all = qk[:, 0:M_QK].astype(BF16)
    k_all = qk[:, M_QK:2 * M_QK] * (M_QK_DIM ** -0.5)

    zt = gt_ref[...] + gbc_ref[:, 0:1]
    lf_all = _log_sigmoid(zt)
    b_all = _cumsum_lanes(lf_all)

    row = lax.broadcasted_iota(jnp.int32, (tc, tc), 0)
    col = lax.broadcasted_iota(jnp.int32, (tc, tc), 1)
    causal = col <= row
    ones_aug = jnp.ones((tc, LANES), BF16)

    for h in range(M_HEADS):
        qcols = slice(h * M_QK_DIM, (h + 1) * M_QK_DIM)
        vcols = slice(h * M_V_DIM, (h + 1) * M_V_DIM)
        f = M_HEADS + h
        q_h = q_all[:, qcols]
        kt = k_all[:, qcols].T
        m_prev = m_s[h][0:1, 0:1]
        c_prev = c_s[h]

        u = zt[h:h + 1, :] - b_all[f:f + 1, :]
        b_tot = b_all[f:f + 1, tc - 1:tc]
        um = jnp.where(causal, u, NEG)
        g = jnp.maximum(jnp.max(um, axis=1, keepdims=True), m_prev)
        b_col = jnp.sum(jnp.where(causal, lf_all[f:f + 1, :], 0.0), axis=1, keepdims=True)
        p = jnp.exp(um - g) * _dot(q_h, kt.astype(BF16))
        a_inter = jnp.exp(m_prev - g)
        v_aug = jnp.concatenate([v_ref[:, vcols].astype(BF16), ones_aug], axis=1)
        na = _dot(p.astype(BF16), v_aug) + a_inter * _dot(q_h, c_prev.astype(BF16))
        num = na[:, 0:M_V_DIM]
        den = na[:, M_V_DIM:M_V_DIM + 1]
        hh = num / jnp.maximum(jnp.abs(den), jnp.exp(-(b_col + g)))
        hh = hh * _rms_scale(hh) * mn_ref[0:1, vcols] * _sigmoid(o_ref[:, vcols])
        out_ref[:, vcols] = hh.astype(BF16)

        w_log = b_tot + u
        m_loc = jnp.max(w_log, axis=1, keepdims=True)
        c_loc = _dot((kt * jnp.exp(w_log - m_loc)).astype(BF16), v_aug)
        m_new = jnp.maximum(b_tot + m_prev, m_loc)
        c_s[h] = jnp.exp(b_tot + m_prev - m_new) * c_prev + jnp.exp(m_loc - m_new) * c_loc
        m_s[h] = jnp.broadcast_to(m_new, (SUBLANES, LANES))
        _interleave(h, M_HEADS, emit, n_parts)


def _mlstm_init(step, scratch_refs):
    qk_buf, c_s, m_s = scratch_refs

    @pl.when(step == 0)
    def _():
        qk_buf[0:SUBLANES, :] = jnp.zeros((SUBLANES, 2 * M_QK), F32)
        c_s[...] = jnp.zeros(c_s.shape, F32)
        m_s[...] = jnp.full(m_s.shape, NEG, F32)


def _mlstm_side(pa, gt, cw, cb, gbc, mn):
    s = pa.shape[0]
    tc = MLSTM_BLOCK
    v_blk = (2 * M_QK) // M_V
    const = lambda i: (0, 0)
    return _Side(
        body=_mlstm_body,
        init=_mlstm_init,
        args=(pa, pa, pa, pa, gt, cw, cb, gbc, mn),
        in_specs=(
            _Spec((tc, M_QK), lambda i: (i, 0)),
            _Spec((tc, M_QK), lambda i: (i, 1)),
            _Spec((tc, M_V), lambda i: (i, v_blk)),
            _Spec((tc, M_V), lambda i: (i, v_blk + 1)),
            _Spec((SUBLANES, tc), lambda i: (0, i)),
            _Spec((M_CONV, 2 * M_QK), const),
            _Spec((1, 2 * M_QK), const),
            _Spec((SUBLANES, LANES), const),
            _Spec((1, M_V), const),
        ),
        out_specs=(_Spec((tc, M_V), lambda i: (i, 0)),),
        out_shapes=(jax.ShapeDtypeStruct((s, M_V), BF16),),
        scratch=(
            pltpu.VMEM((SUBLANES + tc, 2 * M_QK), F32),
            pltpu.VMEM((M_HEADS, M_QK_DIM, M_V_DIM + LANES), F32),
            pltpu.VMEM((M_HEADS, SUBLANES, LANES), F32),
        ),
        steps=s // tc,
    )


ATT_G = 4
ATT_ROWS = ATT_G * CHUNK
ATT_KBLOCKS = (A_PAST_CHUNKS + ATT_G) // ATT_G


def _attn_body(step, in_refs, out_refs, scratch_refs, emit, n_parts):
    del step, scratch_refs
    q_ref, k0_ref, k1_ref, k2_ref, v0_ref, v1_ref, v2_ref, bias_ref = in_refs
    (out_ref,) = out_refs
    scale2 = (A_HEAD_DIM ** -0.5) * LOG2E
    ones = jnp.ones((ATT_ROWS, A_HEAD_DIM), BF16)
    for h in range(A_HEADS):
        cols = slice(h * A_HEAD_DIM, (h + 1) * A_HEAD_DIM)
        q = q_ref[:, cols]
        s = jnp.concatenate([_dot_nt(q, k0_ref[:, cols]), _dot_nt(q, k1_ref[:, cols]),
                             _dot_nt(q, k2_ref[:, cols])], axis=1)
        s = s * scale2 + bias_ref[0, h]
        p = jnp.exp2(s - jnp.max(s, axis=1, keepdims=True)).astype(BF16)
        o = (_dot(p[:, 0:ATT_ROWS], jnp.concatenate([v0_ref[:, cols], ones], axis=1))
             + _dot(p[:, ATT_ROWS:2 * ATT_ROWS], jnp.concatenate([v1_ref[:, cols], ones], axis=1))
             + _dot(p[:, 2 * ATT_ROWS:], jnp.concatenate([v2_ref[:, cols], ones], axis=1)))
        out_ref[:, cols] = (o[:, 0:A_HEAD_DIM] / o[:, A_HEAD_DIM:A_HEAD_DIM + 1]).astype(BF16)
    for k in range(n_parts):
        emit(k)


def _attn_side(pb, bias):
    s = pb.shape[0]
    last = ATT_KBLOCKS - 1

    def kv(group, back):
        return _Spec((ATT_ROWS, A_W), lambda g: (jnp.maximum(g - back, 0), group))

    return _Side(
        body=_attn_body,
        init=None,
        args=(pb,) * 7 + (bias,),
        in_specs=(
            _Spec((ATT_ROWS, A_W), lambda g: (g, 0)),
            kv(1, 2), kv(1, 1), kv(1, 0),
            kv(2, 2), kv(2, 1), kv(2, 0),
            _Spec((1, A_HEADS, ATT_ROWS, ATT_KBLOCKS * ATT_ROWS),
                  lambda g: (jnp.minimum(g, last), 0, 0, 0), pl.Buffered(1)),
        ),
        out_specs=(_Spec((ATT_ROWS, A_W), lambda g: (g, 0)),),
        out_shapes=(jax.ShapeDtypeStruct((s, A_W), BF16),),
        scratch=(),
        steps=s // ATT_ROWS,
    )


def _band_bias(rel_table):
    hds = rel_table.shape[0]
    wcols = ATT_KBLOCKS * ATT_ROWS
    far = A_PAST_CHUNKS * CHUNK - A_MAX_REL + CHUNK
    tab = rel_table.astype(F32)
    e = jnp.concatenate([jnp.broadcast_to(tab[:, 2 * A_MAX_REL:], (hds, far)),
                         tab[:, A_MAX_REL - CHUNK + 1:2 * A_MAX_REL][:, ::-1]], axis=1)
    period = ATT_ROWS + wcols
    lead = ATT_ROWS - CHUNK
    e_pad = jnp.pad(e, ((0, 0), (lead, period - lead - e.shape[1])))
    tiled = jnp.broadcast_to(e_pad[:, None, :], (hds, ATT_ROWS, period)).reshape(hds, -1)
    skew = tiled[:, :ATT_ROWS * (period - 1)].reshape(hds, ATT_ROWS, period - 1)
    toep = skew[:, :, ATT_ROWS - 1:ATT_ROWS - 1 + wcols]
    qc = jnp.arange(ATT_ROWS)[:, None] // CHUNK
    kc = jnp.arange(wcols)[None, :] // CHUNK
    band = (kc >= qc) & (kc <= qc + A_PAST_CHUNKS)
    first_valid = (ATT_KBLOCKS - 1 - jnp.arange(ATT_KBLOCKS)) * ATT_ROWS
    valid = band[None] & (jnp.arange(wcols)[None, None, :] >= first_valid[:, None, None])
    return jnp.where(valid[:, None], (toep * LOG2E)[None], NEG)


def _merge_kernel(hm_ref, ha_ref, gm_ref, ga_ref, x_ref, wm_ref, wa_ref, wo_ref, gb_ref, nf_ref,
                  x1_ref, h2_ref):
    ym = _dot(hm_ref[...], wm_ref[...])
    ya = _dot(ha_ref[...], wa_ref[...])
    merged = (_sigmoid(gm_ref[...].astype(F32) + gb_ref[0:1, :]) * ym
              + _sigmoid(ga_ref[...].astype(F32) + gb_ref[1:2, :]) * ya)
    x1 = x_ref[...] + _dot(merged.astype(BF16), wo_ref[...])
    x1_ref[...] = x1
    h2_ref[...] = (x1 * _rms_scale(x1) * nf_ref[...]).astype(BF16)


def _merge(hm, ha, pc, x, wm, wa, wo, gb, nf, tm=256):
    s, d = x.shape
    const = lambda i: (0, 0)
    resident = pl.Buffered(1)
    return pl.pallas_call(
        _merge_kernel,
        grid=(s // tm,),
        in_specs=[
            pl.BlockSpec((tm, M_V), lambda i: (i, 0)),
            pl.BlockSpec((tm, A_W), lambda i: (i, 0)),
            pl.BlockSpec((tm, d), lambda i: (i, 0)),
            pl.BlockSpec((tm, d), lambda i: (i, 1)),
            pl.BlockSpec((tm, d), lambda i: (i, 0)),
            pl.BlockSpec((M_V, d), const, pipeline_mode=resident),
            pl.BlockSpec((A_W, d), const, pipeline_mode=resident),
            pl.BlockSpec((d, d), const, pipeline_mode=resident),
            pl.BlockSpec((2, d), const),
            pl.BlockSpec((1, d), const),
        ],
        out_specs=[
            pl.BlockSpec((tm, d), lambda i: (i, 0)),
            pl.BlockSpec((tm, d), lambda i: (i, 0)),
        ],
        out_shape=[
            jax.ShapeDtypeStruct((s, d), F32),
            jax.ShapeDtypeStruct((s, d), BF16),
        ],
        compiler_params=_params(("arbitrary",)),
        name="merge",
    )(hm, ha, pc, pc, x, wm, wa, wo, gb, nf)


FFN_SUB_ROWS = 512


def _ffn_up_kernel(h_ref, wg_ref, wv_ref, cw_ref, cb_ref, a_ref, w_bf, ug_buf, *, tm, tf):
    halo = SUBLANES
    rs = FFN_SUB_ROWS

    @pl.when(pl.program_id(1) == 0)
    def _():
        w_bf[:, 0:tf] = wg_ref[...].astype(BF16)
        w_bf[:, tf:2 * tf] = wv_ref[...].astype(BF16)
        ug_buf[0:halo, :] = jnp.zeros((halo, tf), F32)

    for r in range(tm // rs):
        h = h_ref[r * rs:(r + 1) * rs, :]
        ug = _dot(h, w_bf[:, 0:tf])
        uv = _dot(h, w_bf[:, tf:2 * tf])
        base = halo + r * rs
        ug_buf[base:base + rs, :] = ug
        conv = cb_ref[...] + ug * cw_ref[FFN_CONV - 1:FFN_CONV, :]
        for tap in range(FFN_CONV - 1):
            lo = base - (FFN_CONV - 1) + tap
            conv = conv + ug_buf[lo:lo + rs, :] * cw_ref[tap:tap + 1, :]
        a_ref[r * rs:(r + 1) * rs, :] = (conv * _sigmoid(conv) * uv).astype(BF16)
    ug_buf[0:halo, :] = ug_buf[tm:tm + halo, :]


def _ffn_up(h2, w_up, cw, cb, tm=1024, tf=512):
    s, d = h2.shape
    nf = D_FF // tf
    return pl.pallas_call(
        functools.partial(_ffn_up_kernel, tm=tm, tf=tf),
        grid=(nf, s // tm),
        in_specs=[
            pl.BlockSpec((tm, d), lambda j, m: (m, 0)),
            pl.BlockSpec((d, tf), lambda j, m: (0, j)),
            pl.BlockSpec((d, tf), lambda j, m: (0, nf + j)),
            pl.BlockSpec((FFN_CONV, tf), lambda j, m: (0, j)),
            pl.BlockSpec((1, tf), lambda j, m: (0, j)),
        ],
        out_specs=pl.BlockSpec((tm, tf), lambda j, m: (m, j)),
        out_shape=jax.ShapeDtypeStruct((s, D_FF), BF16),
        scratch_shapes=[
            pltpu.VMEM((d, 2 * tf), BF16),
            pltpu.VMEM((SUBLANES + tm, tf), F32),
        ],
        compiler_params=_params(("arbitrary", "arbitrary")),
        name="ffn_up",
    )(h2, w_up, w_up, cw, cb)


def _ffn_down_kernel(a_ref, w_ref, x1_ref, nf_ref, o_ref):
    y = x1_ref[...] + _dot(a_ref[...], w_ref[...])
    o_ref[...] = y * _rms_scale(y) * nf_ref[...]


def _ffn_down(a, w_down, x1, nf, tm=256):
    s, d = x1.shape
    dff = a.shape[1]
    return pl.pallas_call(
        _ffn_down_kernel,
        grid=(s // tm,),
        in_specs=[
            pl.BlockSpec((tm, dff), lambda i: (i, 0)),
            pl.BlockSpec((dff, d), lambda i: (0, 0), pipeline_mode=pl.Buffered(1)),
            pl.BlockSpec((tm, d), lambda i: (i, 0)),
            pl.BlockSpec((1, d), lambda i: (0, 0)),
        ],
        out_specs=pl.BlockSpec((tm, d), lambda i: (i, 0)),
        out_shape=jax.ShapeDtypeStruct((s, d), F32),
        compiler_params=_params(("arbitrary",)),
        name="ffn_down",
    )(a, w_down, x1, nf)


def kernel(x, norm_mix, w_in, conv_qk_w, conv_qk_b, b_igate, b_fgate, m_norm, rel_bias, gate_bias,
           w_branch_m, w_branch_a, w_out, norm_ffn, w_up, conv_ffn_w, conv_ffn_b, w_down, norm_final):
    batch, seq, d = x.shape
    depth = w_in.shape[0]
    assert (batch, seq, d, depth) == (1, SEQ, D_MODEL, 1)
    xs = x[0]
    l = 0

    w_in_t = jnp.swapaxes(w_in[l], 0, 1)
    w_g = jnp.pad(w_in_t[GATE_COL0:GATE_COL0 + N_GATES, :].T, ((0, 0), (0, LANES - N_GATES)))
    gate_b = jnp.concatenate([b_igate[l], b_fgate[l]])
    gb_col = jnp.broadcast_to(gate_b[:, None], (SUBLANES, LANES))
    bias = _band_bias(rel_bias[l])

    h, gt = _prenorm(xs, norm_mix[l][None, :], w_g.astype(BF16))
    na = 2 * M_QK + 2 * M_V
    attn0 = GATE_COL0 + N_GATES
    pa = _proj(h, w_in_t, 0, na, F32, "proj_mlstm")
    mlstm = _mlstm_side(pa, gt, conv_qk_w[l], conv_qk_b[l][None, :], gb_col, m_norm[l][None, :])
    pb, hm = _proj(h, w_in_t, attn0, 3 * A_W, BF16, "proj_attn_mlstm", tn=768, side=mlstm)
    pc, ha = _proj(h, w_in_t, attn0 + 3 * A_W, 2 * D_MODEL, BF16, "proj_gates_attn",
                   side=_attn_side(pb, bias))

    x1, h2 = _merge(hm, ha, pc, xs, w_branch_m[l].astype(BF16), w_branch_a[l].astype(BF16),
                    w_out[l].astype(BF16), gate_bias[l], norm_ffn[l][None, :])
    a = _ffn_up(h2, w_up[l], conv_ffn_w[l], conv_ffn_b[l][None, :])
    out = _ffn_down(a, w_down[l].astype(BF16), x1, norm_final[None, :])
    return out[None]
```

```python
import functools
import math
from typing import Callable, NamedTuple

import jax
import jax.numpy as jnp
from jax import lax
from jax.experimental import pallas as pl
from jax.experimental.pallas import tpu as pltpu

D_MODEL = 2048
SEQ = 8192
CHUNK = 64
M_HEADS = 4
M_QK_DIM = 128
M_V_DIM = 256
M_CONV = 4
A_HEADS = 8
A_HEAD_DIM = 128
A_PAST_CHUNKS = 8
A_MAX_REL = 128
D_FF = 5632
FFN_CONV = 3
EPS = 1e-6
NEG = -1e30
LOG2E = math.log2(math.e)

M_QK = M_HEADS * M_QK_DIM
M_V = M_HEADS * M_V_DIM
A_W = A_HEADS * A_HEAD_DIM
GATE_COL0 = 2 * M_QK + 2 * M_V
N_GATES = 2 * M_HEADS

LANES = 128
SUBLANES = 8
VMEM_LIMIT = 56 * 1024 * 1024

BF16 = jnp.bfloat16
F32 = jnp.float32


def _params(sem, vmem=VMEM_LIMIT, flags=None):
    return pltpu.CompilerParams(dimension_semantics=sem, vmem_limit_bytes=vmem, flags=flags)


def _dot(a, b):
    return jnp.dot(a, b, preferred_element_type=F32)


def _dot_nt(a, b):
    return lax.dot_general(a, b, (((1,), (1,)), ((), ())), preferred_element_type=F32)


def _sigmoid(z):
    return 0.5 * jnp.tanh(0.5 * z) + 0.5


def _log_sigmoid(z):
    return jnp.minimum(z, 0.0) - jnp.log(1.0 + jnp.exp(-jnp.abs(z)))


def _rms_scale(y):
    return lax.rsqrt(jnp.mean(y * y, axis=-1, keepdims=True) + EPS)


def _prenorm_kernel(x_ref, g_ref, wg_ref, h_ref, gt_ref):
    xf = x_ref[...]
    hb = (xf * _rms_scale(xf) * g_ref[...]).astype(BF16)
    h_ref[...] = hb
    gcol = _dot(hb, wg_ref[...])
    gt_ref[...] = gcol.T[0:SUBLANES, :]


def _prenorm(x, g, wg, tm=512):
    s, d = x.shape
    return pl.pallas_call(
        _prenorm_kernel,
        grid=(s // tm,),
        in_specs=[
            pl.BlockSpec((tm, d), lambda i: (i, 0)),
            pl.BlockSpec((1, d), lambda i: (0, 0)),
            pl.BlockSpec((d, LANES), lambda i: (0, 0)),
        ],
        out_specs=[
            pl.BlockSpec((tm, d), lambda i: (i, 0)),
            pl.BlockSpec((SUBLANES, tm), lambda i: (0, i)),
        ],
        out_shape=[
            jax.ShapeDtypeStruct((s, d), BF16),
            jax.ShapeDtypeStruct((SUBLANES, s), F32),
        ],
        compiler_params=_params(("arbitrary",)),
        name="prenorm",
    )(x, g, wg)


class _Spec(NamedTuple):
    block: tuple
    index: Callable
    mode: object = None


class _Side(NamedTuple):
    body: Callable
    init: Callable
    args: tuple
    in_specs: tuple
    out_specs: tuple
    out_shapes: tuple
    scratch: tuple
    steps: int


PROJ_SLICE = 256


def _interleave(i, n_units, emit, n_parts):
    for k in range(i * n_parts // n_units, (i + 1) * n_parts // n_units):
        emit(k)


def _proj_kernel(*refs, shift, tn, side):
    n_w = 3 if shift else 2
    n_si = len(side.args) if side else 0
    n_so = len(side.out_specs) if side else 0
    h_ref, wm_ref = refs[0], refs[1]
    side_in = refs[n_w:n_w + n_si]
    o_ref = refs[n_w + n_si]
    side_out = refs[n_w + n_si + 1:n_w + n_si + 1 + n_so]
    w_bf = refs[n_w + n_si + 1 + n_so]
    side_scratch = refs[n_w + n_si + 2 + n_so:]

    @pl.when(pl.program_id(1) == 0)
    def _():
        if shift:
            w_bf[0:tn - shift, :] = wm_ref[shift:tn, :].astype(BF16)
            w_bf[tn - shift:tn, :] = refs[2][...].astype(BF16)
        else:
            w_bf[...] = wm_ref[...].astype(BF16)

    step = pl.program_id(0) * pl.num_programs(1) + pl.program_id(1)
    if side and side.init:
        side.init(step, side_scratch)

    def emit(k):
        cols = slice(k * PROJ_SLICE, (k + 1) * PROJ_SLICE)
        o_ref[:, cols] = _dot_nt(h_ref[...], w_bf[cols, :]).astype(o_ref.dtype)

    n_parts = tn // PROJ_SLICE
    if side:
        side.body(step, side_in, side_out, side_scratch, emit, n_parts)
    else:
        for k in range(n_parts):
            emit(k)


def _proj(h, wt, col0, ncols, out_dtype, name, tm=1024, tn=1024, side=None):
    s, d = h.shape
    shift = col0 % tn
    base = col0 - shift
    n_m = s // tm
    assert shift % SUBLANES == 0 and ncols % tn == 0
    assert side is None or side.steps == (ncols // tn) * n_m

    def flat(spec):
        return pl.BlockSpec(spec.block, lambda j, m: spec.index(j * n_m + m),
                            pipeline_mode=spec.mode)

    in_specs = [
        pl.BlockSpec((tm, d), lambda j, m: (m, 0)),
        pl.BlockSpec((tn, d), lambda j, m: (base // tn + j, 0)),
    ]
    args = [h, wt]
    if shift:
        in_specs.append(pl.BlockSpec((shift, d), lambda j, m: ((base + (j + 1) * tn) // shift, 0)))
        args.append(wt)
    out_specs = [pl.BlockSpec((tm, tn), lambda j, m: (m, j))]
    out_shapes = [jax.ShapeDtypeStruct((s, ncols), out_dtype)]
    scratch = [pltpu.VMEM((tn, d), BF16)]
    if side:
        in_specs += [flat(sp) for sp in side.in_specs]
        args += list(side.args)
        out_specs += [flat(sp) for sp in side.out_specs]
        out_shapes += list(side.out_shapes)
        scratch += list(side.scratch)
    outs = pl.pallas_call(
        functools.partial(_proj_kernel, shift=shift, tn=tn, side=side),
        grid=(ncols // tn, n_m),
        in_specs=in_specs,
        out_specs=out_specs,
        out_shape=out_shapes,
        scratch_shapes=scratch,
        compiler_params=_params(("arbitrary", "arbitrary")),
        name=name,
    )(*args)
    return outs if side else outs[0]


MLSTM_BLOCK = 256


def _cumsum_lanes(v):
    n = v.shape[-1]
    pos = lax.broadcasted_iota(jnp.int32, v.shape, v.ndim - 1)
    shift = 1
    while shift < n:
        v = v + jnp.where(pos >= shift, pltpu.roll(v, shift, v.ndim - 1), 0.0)
        shift *= 2
    return v


def _mlstm_body(step, in_refs, out_refs, scratch_refs, emit, n_parts):
    q_ref, k_ref, v_ref, o_ref, gt_ref, cw_ref, cb_ref, gbc_ref, mn_ref = in_refs
    (out_ref,) = out_refs
    qk_buf, c_s, m_s = scratch_refs
    del step
    halo = SUBLANES
    tc = MLSTM_BLOCK

    qk_buf[halo:halo + tc, 0:M_QK] = q_ref[...]
    qk_buf[halo:halo + tc, M_QK:2 * M_QK] = k_ref[...]
    acc = jnp.broadcast_to(cb_ref[...], (tc, 2 * M_QK))
    for tap in range(M_CONV):
        lo = halo - (M_CONV - 1) + tap
        acc = acc + qk_buf[lo:lo + tc, :] * cw_ref[tap:tap + 1, :]
    qk_buf[0:halo, :] = qk_buf[tc:tc + halo, :]
    qk = acc * _sigmoid(acc)
    q_all = qk[:, 0:M_QK].astype(BF16)
    k_all = qk[:, M_QK:2 * M_QK] * (M_QK_DIM ** -0.5)

    zt = gt_ref[...] + gbc_ref[:, 0:1]
    lf_all = _log_sigmoid(zt)
    b_all = _cumsum_lanes(lf_all)

    row = lax.broadcasted_iota(jnp.int32, (tc, tc), 0)
    col = lax.broadcasted_iota(jnp.int32, (tc, tc), 1)
    causal = col <= row
    ones_aug = jnp.ones((tc, LANES), BF16)

    def first(h):
        qcols = slice(h * M_QK_DIM, (h + 1) * M_QK_DIM)
        vcols = slice(h * M_V_DIM, (h + 1) * M_V_DIM)
        f = M_HEADS + h
        q_h = q_all[:, qcols]
        kt = k_all[:, qcols].T
        m_prev = m_s[h][0:1, 0:1]
        c_prev = c_s[h]
        u = zt[h:h + 1, :] - b_all[f:f + 1, :]
        b_tot = b_all[f:f + 1, tc - 1:tc]
        v_aug = jnp.concatenate([v_ref[:, vcols].astype(BF16), ones_aug], axis=1)
        qk = _dot(q_h, kt.astype(BF16))
        qc = _dot(q_h, c_prev.astype(BF16))

        w_log = b_tot + u
        m_loc = jnp.max(w_log, axis=1, keepdims=True)
        c_loc = _dot((kt * jnp.exp(w_log - m_loc)).astype(BF16), v_aug)
        m_new = jnp.maximum(b_tot + m_prev, m_loc)
        c_s[h] = jnp.exp(b_tot + m_prev - m_new) * c_prev + jnp.exp(m_loc - m_new) * c_loc
        m_s[h] = jnp.broadcast_to(m_new, (SUBLANES, LANES))
        return u, m_prev, v_aug, qk, qc

    def second(h, u, m_prev, v_aug, qk, qc):
        vcols = slice(h * M_V_DIM, (h + 1) * M_V_DIM)
        f = M_HEADS + h
        um = jnp.where(causal, u, NEG)
        g = jnp.maximum(jnp.max(um, axis=1, keepdims=True), m_prev)
        b_col = jnp.sum(jnp.where(causal, lf_all[f:f + 1, :], 0.0), axis=1, keepdims=True)
        p = jnp.exp(um - g) * qk
        na = _dot(p.astype(BF16), v_aug) + jnp.exp(m_prev - g) * qc
        num = na[:, 0:M_V_DIM]
        den = na[:, M_V_DIM:M_V_DIM + 1]
        hh = num / jnp.maximum(jnp.abs(den), jnp.exp(-(b_col + g)))
        hh = hh * _rms_scale(hh) * mn_ref[0:1, vcols] * _sigmoid(o_ref[:, vcols])
        out_ref[:, vcols] = hh.astype(BF16)

    emit(0)
    groups = n_parts - 1
    per = M_HEADS // groups
    for grp in range(groups):
        heads = range(grp * per, (grp + 1) * per)
        parts = [first(h) for h in heads]
        emit(1 + grp)
        for h, part in zip(heads, parts):
            second(h, *part)


def _mlstm_init(step, scratch_refs):
    qk_buf, c_s, m_s = scratch_refs

    @pl.when(step == 0)
    def _():
        qk_buf[0:SUBLANES, :] = jnp.zeros((SUBLANES, 2 * M_QK), F32)
        c_s[...] = jnp.zeros(c_s.shape, F32)
        m_s[...] = jnp.full(m_s.shape, NEG, F32)


def _mlstm_side(pa, gt, cw, cb, gbc, mn):
    s = pa.shape[0]
    tc = MLSTM_BLOCK
    v_blk = (2 * M_QK) // M_V
    const = lambda i: (0, 0)
    return _Side(
        body=_mlstm_body,
        init=_mlstm_init,
        args=(pa, pa, pa, pa, gt, cw, cb, gbc, mn),
        in_specs=(
            _Spec((tc, M_QK), lambda i: (i, 0)),
            _Spec((tc, M_QK), lambda i: (i, 1)),
            _Spec((tc, M_V), lambda i: (i, v_blk)),
            _Spec((tc, M_V), lambda i: (i, v_blk + 1)),
            _Spec((SUBLANES, tc), lambda i: (0, i)),
            _Spec((M_CONV, 2 * M_QK), const),
            _Spec((1, 2 * M_QK), const),
            _Spec((SUBLANES, LANES), const),
            _Spec((1, M_V), const),
        ),
        out_specs=(_Spec((tc, M_V), lambda i: (i, 0)),),
        out_shapes=(jax.ShapeDtypeStruct((s, M_V), BF16),),
        scratch=(
            pltpu.VMEM((SUBLANES + tc, 2 * M_QK), F32),
            pltpu.VMEM((M_HEADS, M_QK_DIM, M_V_DIM + LANES), F32),
            pltpu.VMEM((M_HEADS, SUBLANES, LANES), F32),
        ),
        steps=s // tc,
    )


ATT_G = 4
ATT_ROWS = ATT_G * CHUNK
ATT_KBLOCKS = (A_PAST_CHUNKS + ATT_G) // ATT_G


def _attn_body(step, in_refs, out_refs, scratch_refs, emit, n_parts):
    del step, scratch_refs
    q_ref, k0_ref, k1_ref, k2_ref, v0_ref, v1_ref, v2_ref, bias_ref = in_refs
    (out_ref,) = out_refs
    scale2 = (A_HEAD_DIM ** -0.5) * LOG2E
    ones = jnp.ones((ATT_KBLOCKS * ATT_ROWS, A_HEAD_DIM), BF16)

    def scores(h):
        cols = slice(h * A_HEAD_DIM, (h + 1) * A_HEAD_DIM)
        k = jnp.concatenate([k0_ref[:, cols], k1_ref[:, cols], k2_ref[:, cols]], axis=0)
        return _dot_nt(q_ref[:, cols], k)

    def finish(h, s):
        cols = slice(h * A_HEAD_DIM, (h + 1) * A_HEAD_DIM)
        s = s * scale2 + bias_ref[0, h]
        p = jnp.exp2(s - jnp.max(s, axis=1, keepdims=True)).astype(BF16)
        v = jnp.concatenate([v0_ref[:, cols], v1_ref[:, cols], v2_ref[:, cols]], axis=0)
        o = _dot(p, jnp.concatenate([v, ones], axis=1))
        out_ref[:, cols] = (o[:, 0:A_HEAD_DIM] / o[:, A_HEAD_DIM:A_HEAD_DIM + 1]).astype(BF16)

    per = A_HEADS // n_parts
    for grp in range(n_parts):
        heads = range(grp * per, (grp + 1) * per)
        s_grp = [scores(h) for h in heads]
        emit(grp)
        for h, s in zip(heads, s_grp):
            finish(h, s)


def _attn_side(pb, bias):
    s = pb.shape[0]
    last = ATT_KBLOCKS - 1

    def kv(group, back):
        return _Spec((ATT_ROWS, A_W), lambda g: (jnp.maximum(g - back, 0), group))

    return _Side(
        body=_attn_body,
        init=None,
        args=(pb,) * 7 + (bias,),
        in_specs=(
            _Spec((ATT_ROWS, A_W), lambda g: (g, 0)),
            kv(1, 2), kv(1, 1), kv(1, 0),
            kv(2, 2), kv(2, 1), kv(2, 0),
            _Spec((1, A_HEADS, ATT_ROWS, ATT_KBLOCKS * ATT_ROWS),
                  lambda g: (jnp.minimum(g, last), 0, 0, 0), pl.Buffered(1)),
        ),
        out_specs=(_Spec((ATT_ROWS, A_W), lambda g: (g, 0)),),
        out_shapes=(jax.ShapeDtypeStruct((s, A_W), BF16),),
        scratch=(),
        steps=s // ATT_ROWS,
    )


def _band_bias(rel_table):
    hds = rel_table.shape[0]
    wcols = ATT_KBLOCKS * ATT_ROWS
    far = A_PAST_CHUNKS * CHUNK - A_MAX_REL + CHUNK
    tab = rel_table.astype(F32)
    e = jnp.concatenate([jnp.broadcast_to(tab[:, 2 * A_MAX_REL:], (hds, far)),
                         tab[:, A_MAX_REL - CHUNK + 1:2 * A_MAX_REL][:, ::-1]], axis=1)
    period = ATT_ROWS + wcols
    lead = ATT_ROWS - CHUNK
    e_pad = jnp.pad(e, ((0, 0), (lead, period - lead - e.shape[1])))
    tiled = jnp.broadcast_to(e_pad[:, None, :], (hds, ATT_ROWS, period)).reshape(hds, -1)
    skew = tiled[:, :ATT_ROWS * (period - 1)].reshape(hds, ATT_ROWS, period - 1)
    toep = skew[:, :, ATT_ROWS - 1:ATT_ROWS - 1 + wcols]
    qc = jnp.arange(ATT_ROWS)[:, None] // CHUNK
    kc = jnp.arange(wcols)[None, :] // CHUNK
    band = (kc >= qc) & (kc <= qc + A_PAST_CHUNKS)
    first_valid = (ATT_KBLOCKS - 1 - jnp.arange(ATT_KBLOCKS)) * ATT_ROWS
    valid = band[None] & (jnp.arange(wcols)[None, None, :] >= first_valid[:, None, None])
    return jnp.where(valid[:, None], (toep * LOG2E)[None], NEG)


def _merge_kernel(hm_ref, ha_ref, gm_ref, ga_ref, x_ref, wm_ref, wa_ref, wo_ref, gb_ref, nf_ref,
                  x1_ref, h2_ref):
    ym = _dot(hm_ref[...], wm_ref[...])
    ya = _dot(ha_ref[...], wa_ref[...])
    merged = (_sigmoid(gm_ref[...].astype(F32) + gb_ref[0:1, :]) * ym
              + _sigmoid(ga_ref[...].astype(F32) + gb_ref[1:2, :]) * ya)
    x1 = x_ref[...] + _dot(merged.astype(BF16), wo_ref[...])
    x1_ref[...] = x1
    h2_ref[...] = (x1 * _rms_scale(x1) * nf_ref[...]).astype(BF16)


def _merge(hm, ha, pc, x, wm, wa, wo, gb, nf, tm=256):
    s, d = x.shape
    const = lambda i: (0, 0)
    resident = pl.Buffered(1)
    return pl.pallas_call(
        _merge_kernel,
        grid=(s // tm,),
        in_specs=[
            pl.BlockSpec((tm, M_V), lambda i: (i, 0)),
            pl.BlockSpec((tm, A_W), lambda i: (i, 0)),
            pl.BlockSpec((tm, d), lambda i: (i, 0)),
            pl.BlockSpec((tm, d), lambda i: (i, 1)),
            pl.BlockSpec((tm, d), lambda i: (i, 0)),
            pl.BlockSpec((M_V, d), const, pipeline_mode=resident),
            pl.BlockSpec((A_W, d), const, pipeline_mode=resident),
            pl.BlockSpec((d, d), const, pipeline_mode=resident),
            pl.BlockSpec((2, d), const),
            pl.BlockSpec((1, d), const),
        ],
        out_specs=[
            pl.BlockSpec((tm, d), lambda i: (i, 0)),
            pl.BlockSpec((tm, d), lambda i: (i, 0)),
        ],
        out_shape=[
            jax.ShapeDtypeStruct((s, d), F32),
            jax.ShapeDtypeStruct((s, d), BF16),
        ],
        compiler_params=_params(("arbitrary",)),
        name="merge",
    )(hm, ha, pc, pc, x, wm, wa, wo, gb, nf)


FFN_SUB_ROWS = 512


def _ffn_up_kernel(h_ref, wg_ref, wv_ref, cw_ref, cb_ref, a_ref, w_bf, ug_buf, *, tm, tf):
    halo = SUBLANES
    rs = FFN_SUB_ROWS

    @pl.when(pl.program_id(1) == 0)
    def _():
        w_bf[:, 0:tf] = wg_ref[...].astype(BF16)
        w_bf[:, tf:2 * tf] = wv_ref[...].astype(BF16)
        ug_buf[0:halo, :] = jnp.zeros((halo, tf), F32)

    for r in range(tm // rs):
        h = h_ref[r * rs:(r + 1) * rs, :]
        ug = _dot(h, w_bf[:, 0:tf])
        uv = _dot(h, w_bf[:, tf:2 * tf])
        base = halo + r * rs
        ug_buf[base:base + rs, :] = ug
        conv = cb_ref[...] + ug * cw_ref[FFN_CONV - 1:FFN_CONV, :]
        for tap in range(FFN_CONV - 1):
            lo = base - (FFN_CONV - 1) + tap
            conv = conv + ug_buf[lo:lo + rs, :] * cw_ref[tap:tap + 1, :]
        a_ref[r * rs:(r + 1) * rs, :] = (conv * _sigmoid(conv) * uv).astype(BF16)
    ug_buf[0:halo, :] = ug_buf[tm:tm + halo, :]


def _ffn_up(h2, w_up, cw, cb, tm=2048, tf=512):
    s, d = h2.shape
    nf = D_FF // tf
    return pl.pallas_call(
        functools.partial(_ffn_up_kernel, tm=tm, tf=tf),
        grid=(nf, s // tm),
        in_specs=[
            pl.BlockSpec((tm, d), lambda j, m: (m, 0)),
            pl.BlockSpec((d, tf), lambda j, m: (0, j)),
            pl.BlockSpec((d, tf), lambda j, m: (0, nf + j)),
            pl.BlockSpec((FFN_CONV, tf), lambda j, m: (0, j)),
            pl.BlockSpec((1, tf), lambda j, m: (0, j)),
        ],
        out_specs=pl.BlockSpec((tm, tf), lambda j, m: (m, j)),
        out_shape=jax.ShapeDtypeStruct((s, D_FF), BF16),
        scratch_shapes=[
            pltpu.VMEM((d, 2 * tf), BF16),
            pltpu.VMEM((SUBLANES + tm, tf), F32),
        ],
        compiler_params=_params(("arbitrary", "arbitrary")),
        name="ffn_up",
    )(h2, w_up, w_up, cw, cb)


def _ffn_down_kernel(a_ref, w_ref, x1_ref, nf_ref, o_ref):
    y = x1_ref[...] + _dot(a_ref[...], w_ref[...])
    o_ref[...] = y * _rms_scale(y) * nf_ref[...]


def _ffn_down(a, w_down, x1, nf, tm=256):
    s, d = x1.shape
    dff = a.shape[1]
    return pl.pallas_call(
        _ffn_down_kernel,
        grid=(s // tm,),
        in_specs=[
            pl.BlockSpec((tm, dff), lambda i: (i, 0)),
            pl.BlockSpec((dff, d), lambda i: (0, 0), pipeline_mode=pl.Buffered(1)),
            pl.BlockSpec((tm, d), lambda i: (i, 0)),
            pl.BlockSpec((1, d), lambda i: (0, 0)),
        ],
        out_specs=pl.BlockSpec((tm, d), lambda i: (i, 0)),
        out_shape=jax.ShapeDtypeStruct((s, d), F32),
        compiler_params=_params(("arbitrary",)),
        name="ffn_down",
    )(a, w_down, x1, nf)


def kernel(x, norm_mix, w_in, conv_qk_w, conv_qk_b, b_igate, b_fgate, m_norm, rel_bias, gate_bias,
           w_branch_m, w_branch_a, w_out, norm_ffn, w_up, conv_ffn_w, conv_ffn_b, w_down, norm_final):
    batch, seq, d = x.shape
    depth = w_in.shape[0]
    assert (batch, seq, d, depth) == (1, SEQ, D_MODEL, 1)
    xs = x[0]
    l = 0

    w_in_t = jnp.swapaxes(w_in[l], 0, 1)
    w_g = jnp.pad(w_in_t[GATE_COL0:GATE_COL0 + N_GATES, :].T, ((0, 0), (0, LANES - N_GATES)))
    gate_b = jnp.concatenate([b_igate[l], b_fgate[l]])
    gb_col = jnp.broadcast_to(gate_b[:, None], (SUBLANES, LANES))
    bias = _band_bias(rel_bias[l])

    h, gt = _prenorm(xs, norm_mix[l][None, :], w_g.astype(BF16))
    na = 2 * M_QK + 2 * M_V
    attn0 = GATE_COL0 + N_GATES
    pa = _proj(h, w_in_t, 0, na, F32, "proj_mlstm")
    mlstm = _mlstm_side(pa, gt, conv_qk_w[l], conv_qk_b[l][None, :], gb_col, m_norm[l][None, :])
    pb, hm = _proj(h, w_in_t, attn0, 3 * A_W, BF16, "proj_attn_mlstm", tn=768, side=mlstm)
    pc, ha = _proj(h, w_in_t, attn0 + 3 * A_W, 2 * D_MODEL, BF16, "proj_gates_attn",
                   side=_attn_side(pb, bias))

    x1, h2 = _merge(hm, ha, pc, xs, w_branch_m[l].astype(BF16), w_branch_a[l].astype(BF16),
                    w_out[l].astype(BF16), gate_bias[l], norm_ffn[l][None, :])
    a = _ffn_up(h2, w_up[l], conv_ffn_w[l], conv_ffn_b[l][None, :])
    out = _ffn_down(a, w_down[l].astype(BF16), x1, norm_final[None, :])
    return out[None]
```

```python
import functools
import math
from typing import Callable, NamedTuple

import jax
import jax.numpy as jnp
from jax import lax
from jax.experimental import pallas as pl
from jax.experimental.pallas import tpu as pltpu

D_MODEL = 2048
SEQ = 8192
CHUNK = 64
M_HEADS = 4
M_QK_DIM = 128
M_V_DIM = 256
M_CONV = 4
A_HEADS = 8
A_HEAD_DIM = 128
A_PAST_CHUNKS = 8
A_MAX_REL = 128
D_FF = 5632
FFN_CONV = 3
EPS = 1e-6
NEG = -1e30
LOG2E = math.log2(math.e)

M_QK = M_HEADS * M_QK_DIM
M_V = M_HEADS * M_V_DIM
A_W = A_HEADS * A_HEAD_DIM
GATE_COL0 = 2 * M_QK + 2 * M_V
N_GATES = 2 * M_HEADS

LANES = 128
SUBLANES = 8
VMEM_LIMIT = 56 * 1024 * 1024

BF16 = jnp.bfloat16
F32 = jnp.float32


def _params(sem, vmem=VMEM_LIMIT, flags=None):
    return pltpu.CompilerParams(dimension_semantics=sem, vmem_limit_bytes=vmem, flags=flags)


def _dot(a, b):
    return jnp.dot(a, b, preferred_element_type=F32)


def _dot_nt(a, b):
    return lax.dot_general(a, b, (((1,), (1,)), ((), ())), preferred_element_type=F32)


def _sigmoid(z):
    return 0.5 * jnp.tanh(0.5 * z) + 0.5


def _log_sigmoid(z):
    return jnp.minimum(z, 0.0) - jnp.log(1.0 + jnp.exp(-jnp.abs(z)))


def _rms_scale(y):
    return lax.rsqrt(jnp.mean(y * y, axis=-1, keepdims=True) + EPS)


def _prenorm_kernel(x_ref, g_ref, wg_ref, h_ref, gt_ref):
    xf = x_ref[...]
    hb = (xf * _rms_scale(xf) * g_ref[...]).astype(BF16)
    h_ref[...] = hb
    gcol = _dot(hb, wg_ref[...])
    gt_ref[...] = gcol.T[0:SUBLANES, :]


def _prenorm(x, g, wg, tm=512):
    s, d = x.shape
    return pl.pallas_call(
        _prenorm_kernel,
        grid=(s // tm,),
        in_specs=[
            pl.BlockSpec((tm, d), lambda i: (i, 0)),
            pl.BlockSpec((1, d), lambda i: (0, 0)),
            pl.BlockSpec((d, LANES), lambda i: (0, 0)),
        ],
        out_specs=[
            pl.BlockSpec((tm, d), lambda i: (i, 0)),
            pl.BlockSpec((SUBLANES, tm), lambda i: (0, i)),
        ],
        out_shape=[
            jax.ShapeDtypeStruct((s, d), BF16),
            jax.ShapeDtypeStruct((SUBLANES, s), F32),
        ],
        compiler_params=_params(("arbitrary",)),
        name="prenorm",
    )(x, g, wg)


class _Spec(NamedTuple):
    block: tuple
    index: Callable
    mode: object = None


class _Side(NamedTuple):
    body: Callable
    init: Callable
    args: tuple
    in_specs: tuple
    out_specs: tuple
    out_shapes: tuple
    scratch: tuple
    steps: int
    clamped: bool = False
    places: bool = False


PROJ_SLICE = 256


def _interleave(i, n_units, emit, n_parts):
    for k in range(i * n_parts // n_units, (i + 1) * n_parts // n_units):
        emit(k)


def _proj_kernel(*refs, shift, tn, sides):
    n_w = 3 if shift else 2
    h_ref, wm_ref = refs[0], refs[1]
    pos = n_w
    side_in = []
    for sd in sides:
        side_in.append(refs[pos:pos + len(sd.args)])
        pos += len(sd.args)
    o_ref = refs[pos]
    pos += 1
    side_out = []
    for sd in sides:
        side_out.append(refs[pos:pos + len(sd.out_specs)])
        pos += len(sd.out_specs)
    w_bf = refs[pos]
    pos += 1
    side_scratch = []
    for sd in sides:
        side_scratch.append(refs[pos:pos + len(sd.scratch)])
        pos += len(sd.scratch)

    @pl.when(pl.program_id(1) == 0)
    def _():
        if shift:
            w_bf[0:tn - shift, :] = wm_ref[shift:tn, :].astype(BF16)
            w_bf[tn - shift:tn, :] = refs[2][...].astype(BF16)
        else:
            w_bf[...] = wm_ref[...].astype(BF16)

    step = pl.program_id(0) * pl.num_programs(1) + pl.program_id(1)
    for sd, scr in zip(sides, side_scratch):
        if sd.init:
            sd.init(step, scr)

    def emit(k):
        cols = slice(k * PROJ_SLICE, (k + 1) * PROJ_SLICE)
        o_ref[:, cols] = _dot_nt(h_ref[...], w_bf[cols, :]).astype(o_ref.dtype)

    n_parts = tn // PROJ_SLICE
    placers = [idx for idx, sd in enumerate(sides) if sd.places]
    assert len(placers) <= 1
    for idx, sd in enumerate(sides):
        if not sd.places:
            sd.body(step, side_in[idx], side_out[idx], side_scratch[idx], None, 0)
    if placers:
        idx = placers[0]
        sides[idx].body(step, side_in[idx], side_out[idx], side_scratch[idx], emit, n_parts)
    else:
        for k in range(n_parts):
            emit(k)


def _proj(h, wt, col0, ncols, out_dtype, name, tm=1024, tn=1024, sides=()):
    s, d = h.shape
    shift = col0 % tn
    base = col0 - shift
    n_m = s // tm
    n_steps = (ncols // tn) * n_m
    assert shift % SUBLANES == 0 and ncols % tn == 0
    assert all(sd.steps == n_steps or (sd.clamped and sd.steps < n_steps) for sd in sides)

    def flat(spec):
        return pl.BlockSpec(spec.block, lambda j, m: spec.index(j * n_m + m),
                            pipeline_mode=spec.mode)

    in_specs = [
        pl.BlockSpec((tm, d), lambda j, m: (m, 0)),
        pl.BlockSpec((tn, d), lambda j, m: (base // tn + j, 0)),
    ]
    args = [h, wt]
    if shift:
        in_specs.append(pl.BlockSpec((shift, d), lambda j, m: ((base + (j + 1) * tn) // shift, 0)))
        args.append(wt)
    out_specs = [pl.BlockSpec((tm, tn), lambda j, m: (m, j))]
    out_shapes = [jax.ShapeDtypeStruct((s, ncols), out_dtype)]
    scratch = [pltpu.VMEM((tn, d), BF16)]
    for sd in sides:
        in_specs += [flat(sp) for sp in sd.in_specs]
        args += list(sd.args)
    for sd in sides:
        out_specs += [flat(sp) for sp in sd.out_specs]
        out_shapes += list(sd.out_shapes)
    for sd in sides:
        scratch += list(sd.scratch)
    return pl.pallas_call(
        functools.partial(_proj_kernel, shift=shift, tn=tn, sides=tuple(sides)),
        grid=(ncols // tn, n_m),
        in_specs=in_specs,
        out_specs=out_specs,
        out_shape=out_shapes,
        scratch_shapes=scratch,
        compiler_params=_params(("arbitrary", "arbitrary")),
        name=name,
    )(*args)


def _cast_body(step, in_refs, out_refs, scratch_refs, emit, n_parts):
    del step, scratch_refs, emit, n_parts
    out_refs[0][...] = in_refs[0][...].astype(BF16)


def _cast_side(w, rows):
    n, d = w.shape
    nb = n // rows
    spec = _Spec((rows, d), lambda i: (jnp.minimum(i, nb - 1), 0))
    return _Side(body=_cast_body, init=None, args=(w,), in_specs=(spec,), out_specs=(spec,),
                 out_shapes=(jax.ShapeDtypeStruct((n, d), BF16),), scratch=(), steps=nb,
                 clamped=True)


MLSTM_BLOCK = 256


def _cumsum_lanes(v):
    n = v.shape[-1]
    pos = lax.broadcasted_iota(jnp.int32, v.shape, v.ndim - 1)
    shift = 1
    while shift < n:
        v = v + jnp.where(pos >= shift, pltpu.roll(v, shift, v.ndim - 1), 0.0)
        shift *= 2
    return v


def _mlstm_body(step, in_refs, out_refs, scratch_refs, emit, n_parts):
    q_ref, k_ref, v_ref, o_ref, gt_ref, cw_ref, cb_ref, gbc_ref, mn_ref = in_refs
    (out_ref,) = out_refs
    qk_buf, c_s, m_s = scratch_refs
    del step
    halo = SUBLANES
    tc = MLSTM_BLOCK

    qk_buf[halo:halo + tc, 0:M_QK] = q_ref[...]
    qk_buf[halo:halo + tc, M_QK:2 * M_QK] = k_ref[...]
    acc = jnp.broadcast_to(cb_ref[...], (tc, 2 * M_QK))
    for tap in range(M_CONV):
        lo = halo - (M_CONV - 1) + tap
        acc = acc + qk_buf[lo:lo + tc, :] * cw_ref[tap:tap + 1, :]
    qk_buf[0:halo, :] = qk_buf[tc:tc + halo, :]
    qk = acc * _sigmoid(acc)
    q_all = qk[:, 0:M_QK].astype(BF16)
    k_all = qk[:, M_QK:2 * M_QK] * (M_QK_DIM ** -0.5)

    zt = gt_ref[...] + gbc_ref[:, 0:1]
    lf_all = _log_sigmoid(zt)
    b_all = _cumsum_lanes(lf_all)

    row = lax.broadcasted_iota(jnp.int32, (tc, tc), 0)
    col = lax.broadcasted_iota(jnp.int32, (tc, tc), 1)
    causal = col <= row
    ones_aug = jnp.ones((tc, LANES), BF16)

    def first(h):
        qcols = slice(h * M_QK_DIM, (h + 1) * M_QK_DIM)
        vcols = slice(h * M_V_DIM, (h + 1) * M_V_DIM)
        f = M_HEADS + h
        q_h = q_all[:, qcols]
        kt = k_all[:, qcols].T
        m_prev = m_s[h][0:1, 0:1]
        c_prev = c_s[h]
        u = zt[h:h + 1, :] - b_all[f:f + 1, :]
        b_tot = b_all[f:f + 1, tc - 1:tc]
        v_aug = jnp.concatenate([v_ref[:, vcols].astype(BF16), ones_aug], axis=1)
        qk = _dot(q_h, kt.astype(BF16))
        qc = _dot(q_h, c_prev.astype(BF16))

        w_log = b_tot + u
        m_loc = jnp.max(w_log, axis=1, keepdims=True)
        c_loc = _dot((kt * jnp.exp(w_log - m_loc)).astype(BF16), v_aug)
        m_new = jnp.maximum(b_tot + m_prev, m_loc)
        c_s[h] = jnp.exp(b_tot + m_prev - m_new) * c_prev + jnp.exp(m_loc - m_new) * c_loc
        m_s[h] = jnp.broadcast_to(m_new, (SUBLANES, LANES))
        return u, m_prev, v_aug, qk, qc

    def second(h, u, m_prev, v_aug, qk, qc):
        vcols = slice(h * M_V_DIM, (h + 1) * M_V_DIM)
        f = M_HEADS + h
        um = jnp.where(causal, u, NEG)
        g = jnp.maximum(jnp.max(um, axis=1, keepdims=True), m_prev)
        b_col = jnp.sum(jnp.where(causal, lf_all[f:f + 1, :], 0.0), axis=1, keepdims=True)
        p = jnp.exp(um - g) * qk
        na = _dot(p.astype(BF16), v_aug) + jnp.exp(m_prev - g) * qc
        num = na[:, 0:M_V_DIM]
        den = na[:, M_V_DIM:M_V_DIM + 1]
        hh = num / jnp.maximum(jnp.abs(den), jnp.exp(-(b_col + g)))
        hh = hh * _rms_scale(hh) * mn_ref[0:1, vcols] * _sigmoid(o_ref[:, vcols])
        out_ref[:, vcols] = hh.astype(BF16)

    emit(0)
    groups = n_parts - 1
    per = M_HEADS // groups
    for grp in range(groups):
        heads = range(grp * per, (grp + 1) * per)
        parts = [first(h) for h in heads]
        emit(1 + grp)
        for h, part in zip(heads, parts):
            second(h, *part)


def _mlstm_init(step, scratch_refs):
    qk_buf, c_s, m_s = scratch_refs

    @pl.when(step == 0)
    def _():
        qk_buf[0:SUBLANES, :] = jnp.zeros((SUBLANES, 2 * M_QK), F32)
        c_s[...] = jnp.zeros(c_s.shape, F32)
        m_s[...] = jnp.full(m_s.shape, NEG, F32)


def _mlstm_side(pa, gt, cw, cb, gbc, mn):
    s = pa.shape[0]
    tc = MLSTM_BLOCK
    v_blk = (2 * M_QK) // M_V
    const = lambda i: (0, 0)
    return _Side(
        body=_mlstm_body,
        init=_mlstm_init,
        args=(pa, pa, pa, pa, gt, cw, cb, gbc, mn),
        in_specs=(
            _Spec((tc, M_QK), lambda i: (i, 0)),
            _Spec((tc, M_QK), lambda i: (i, 1)),
            _Spec((tc, M_V), lambda i: (i, v_blk)),
            _Spec((tc, M_V), lambda i: (i, v_blk + 1)),
            _Spec((SUBLANES, tc), lambda i: (0, i)),
            _Spec((M_CONV, 2 * M_QK), const),
            _Spec((1, 2 * M_QK), const),
            _Spec((SUBLANES, LANES), const),
            _Spec((1, M_V), const),
        ),
        out_specs=(_Spec((tc, M_V), lambda i: (i, 0)),),
        out_shapes=(jax.ShapeDtypeStruct((s, M_V), BF16),),
        scratch=(
            pltpu.VMEM((SUBLANES + tc, 2 * M_QK), F32),
            pltpu.VMEM((M_HEADS, M_QK_DIM, M_V_DIM + LANES), F32),
            pltpu.VMEM((M_HEADS, SUBLANES, LANES), F32),
        ),
        steps=s // tc,
        places=True,
    )


ATT_G = 4
ATT_ROWS = ATT_G * CHUNK
ATT_KBLOCKS = (A_PAST_CHUNKS + ATT_G) // ATT_G


def _attn_body(step, in_refs, out_refs, scratch_refs, emit, n_parts):
    del step, scratch_refs
    q_ref, k0_ref, k1_ref, k2_ref, v0_ref, v1_ref, v2_ref, bias_ref = in_refs
    (out_ref,) = out_refs
    scale2 = (A_HEAD_DIM ** -0.5) * LOG2E
    ones = jnp.ones((ATT_KBLOCKS * ATT_ROWS, A_HEAD_DIM), BF16)

    def scores(h):
        cols = slice(h * A_HEAD_DIM, (h + 1) * A_HEAD_DIM)
        k = jnp.concatenate([k0_ref[:, cols], k1_ref[:, cols], k2_ref[:, cols]], axis=0)
        return _dot_nt(q_ref[:, cols], k)

    def finish(h, s):
        cols = slice(h * A_HEAD_DIM, (h + 1) * A_HEAD_DIM)
        s = s * scale2 + bias_ref[0, h]
        p = jnp.exp2(s - jnp.max(s, axis=1, keepdims=True)).astype(BF16)
        v = jnp.concatenate([v0_ref[:, cols], v1_ref[:, cols], v2_ref[:, cols]], axis=0)
        o = _dot(p, jnp.concatenate([v, ones], axis=1))
        out_ref[:, cols] = (o[:, 0:A_HEAD_DIM] / o[:, A_HEAD_DIM:A_HEAD_DIM + 1]).astype(BF16)

    per = A_HEADS // n_parts
    for grp in range(n_parts):
        heads = range(grp * per, (grp + 1) * per)
        s_grp = [scores(h) for h in heads]
        emit(grp)
        for h, s in zip(heads, s_grp):
            finish(h, s)


def _attn_side(pb, bias):
    s = pb.shape[0]
    last = ATT_KBLOCKS - 1

    def kv(group, back):
        return _Spec((ATT_ROWS, A_W), lambda g: (jnp.maximum(g - back, 0), group))

    return _Side(
        body=_attn_body,
        init=None,
        args=(pb,) * 7 + (bias,),
        in_specs=(
            _Spec((ATT_ROWS, A_W), lambda g: (g, 0)),
            kv(1, 2), kv(1, 1), kv(1, 0),
            kv(2, 2), kv(2, 1), kv(2, 0),
            _Spec((1, A_HEADS, ATT_ROWS, ATT_KBLOCKS * ATT_ROWS),
                  lambda g: (jnp.minimum(g, last), 0, 0, 0), pl.Buffered(1)),
        ),
        out_specs=(_Spec((ATT_ROWS, A_W), lambda g: (g, 0)),),
        out_shapes=(jax.ShapeDtypeStruct((s, A_W), BF16),),
        scratch=(),
        steps=s // ATT_ROWS,
        places=True,
    )


def _band_bias(rel_table):
    hds = rel_table.shape[0]
    wcols = ATT_KBLOCKS * ATT_ROWS
    band_w = (A_PAST_CHUNKS + 1) * CHUNK
    far = A_PAST_CHUNKS * CHUNK - A_MAX_REL + CHUNK
    tab = rel_table.astype(F32) * LOG2E
    e = jnp.concatenate([jnp.broadcast_to(tab[:, 2 * A_MAX_REL:], (hds, far)),
                         tab[:, A_MAX_REL - CHUNK + 1:2 * A_MAX_REL][:, ::-1],
                         jnp.zeros((hds, 1), F32)], axis=1)
    period = e.shape[1]
    tiled = jnp.broadcast_to(e[:, None, :], (hds, CHUNK, period)).reshape(hds, -1)
    skew = tiled[:, :CHUNK * (period - 1)].reshape(hds, CHUNK, period - 1)
    chunk_bias = skew[:, :, CHUNK - 1:CHUNK - 1 + band_w]
    rows = [jnp.pad(chunk_bias, ((0, 0), (0, 0), (ci * CHUNK, wcols - band_w - ci * CHUNK)),
                    constant_values=NEG) for ci in range(ATT_G)]
    base = jnp.concatenate(rows, axis=1)
    first_valid = (ATT_KBLOCKS - 1 - jnp.arange(ATT_KBLOCKS)) * ATT_ROWS
    valid = jnp.arange(wcols)[None, :] >= first_valid[:, None]
    return jnp.where(valid[:, None, None, :], base[None], NEG)


def _merge_kernel(hm_ref, ha_ref, gm_ref, ga_ref, x_ref, wm_ref, wa_ref, wo_ref, gb_ref, nf_ref,
                  x1_ref, h2_ref):
    ym = _dot(hm_ref[...], wm_ref[...])
    ya = _dot(ha_ref[...], wa_ref[...])
    merged = (_sigmoid(gm_ref[...].astype(F32) + gb_ref[0:1, :]) * ym
              + _sigmoid(ga_ref[...].astype(F32) + gb_ref[1:2, :]) * ya)
    x1 = x_ref[...] + _dot(merged.astype(BF16), wo_ref[...])
    x1_ref[...] = x1
    h2_ref[...] = (x1 * _rms_scale(x1) * nf_ref[...]).astype(BF16)


def _merge(hm, ha, pc, x, wm, wa, wo, gb, nf, tm=256):
    s, d = x.shape
    const = lambda i: (0, 0)
    resident = pl.Buffered(1)
    return pl.pallas_call(
        _merge_kernel,
        grid=(s // tm,),
        in_specs=[
            pl.BlockSpec((tm, M_V), lambda i: (i, 0)),
            pl.BlockSpec((tm, A_W), lambda i: (i, 0)),
            pl.BlockSpec((tm, d), lambda i: (i, 0)),
            pl.BlockSpec((tm, d), lambda i: (i, 1)),
            pl.BlockSpec((tm, d), lambda i: (i, 0)),
            pl.BlockSpec((M_V, d), const, pipeline_mode=resident),
            pl.BlockSpec((A_W, d), const, pipeline_mode=resident),
            pl.BlockSpec((d, d), const, pipeline_mode=resident),
            pl.BlockSpec((2, d), const),
            pl.BlockSpec((1, d), const),
        ],
        out_specs=[
            pl.BlockSpec((tm, d), lambda i: (i, 0)),
            pl.BlockSpec((tm, d), lambda i: (i, 0)),
        ],
        out_shape=[
            jax.ShapeDtypeStruct((s, d), F32),
            jax.ShapeDtypeStruct((s, d), BF16),
        ],
        compiler_params=_params(("arbitrary",)),
        name="merge",
    )(hm, ha, pc, pc, x, wm, wa, wo, gb, nf)


FFN_SUB_ROWS = 512


def _ffn_up_kernel(h_ref, wg_ref, wv_ref, cw_ref, cb_ref, a_ref, w_bf, ug_buf, *, tm, tf):
    halo = SUBLANES
    rs = FFN_SUB_ROWS

    @pl.when(pl.program_id(1) == 0)
    def _():
        w_bf[:, 0:tf] = wg_ref[...].astype(BF16)
        w_bf[:, tf:2 * tf] = wv_ref[...].astype(BF16)
        ug_buf[0:halo, :] = jnp.zeros((halo, tf), F32)

    for r in range(tm // rs):
        h = h_ref[r * rs:(r + 1) * rs, :]
        ug = _dot(h, w_bf[:, 0:tf])
        uv = _dot(h, w_bf[:, tf:2 * tf])
        base = halo + r * rs
        ug_buf[base:base + rs, :] = ug
        conv = cb_ref[...] + ug * cw_ref[FFN_CONV - 1:FFN_CONV, :]
        for tap in range(FFN_CONV - 1):
            lo = base - (FFN_CONV - 1) + tap
            conv = conv + ug_buf[lo:lo + rs, :] * cw_ref[tap:tap + 1, :]
        a_ref[r * rs:(r + 1) * rs, :] = (conv * _sigmoid(conv) * uv).astype(BF16)
    ug_buf[0:halo, :] = ug_buf[tm:tm + halo, :]


def _ffn_up(h2, w_up, cw, cb, tm=2048, tf=512):
    s, d = h2.shape
    nf = D_FF // tf
    return pl.pallas_call(
        functools.partial(_ffn_up_kernel, tm=tm, tf=tf),
        grid=(nf, s // tm),
        in_specs=[
            pl.BlockSpec((tm, d), lambda j, m: (m, 0)),
            pl.BlockSpec((d, tf), lambda j, m: (0, j)),
            pl.BlockSpec((d, tf), lambda j, m: (0, nf + j)),
            pl.BlockSpec((FFN_CONV, tf), lambda j, m: (0, j)),
            pl.BlockSpec((1, tf), lambda j, m: (0, j)),
        ],
        out_specs=pl.BlockSpec((tm, tf), lambda j, m: (m, j)),
        out_shape=jax.ShapeDtypeStruct((s, D_FF), BF16),
        scratch_shapes=[
            pltpu.VMEM((d, 2 * tf), BF16),
            pltpu.VMEM((SUBLANES + tm, tf), F32),
        ],
        compiler_params=_params(("arbitrary", "arbitrary")),
        name="ffn_up",
    )(h2, w_up, w_up, cw, cb)


def _ffn_down_kernel(a_ref, w_ref, x1_ref, nf_ref, o_ref):
    y = x1_ref[...] + _dot(a_ref[...], w_ref[...])
    o_ref[...] = y * _rms_scale(y) * nf_ref[...]


def _ffn_down(a, w_down, x1, nf, tm=256):
    s, d = x1.shape
    dff = a.shape[1]
    return pl.pallas_call(
        _ffn_down_kernel,
        grid=(s // tm,),
        in_specs=[
            pl.BlockSpec((tm, dff), lambda i: (i, 0)),
            pl.BlockSpec((dff, d), lambda i: (0, 0), pipeline_mode=pl.Buffered(1)),
            pl.BlockSpec((tm, d), lambda i: (i, 0)),
            pl.BlockSpec((1, d), lambda i: (0, 0)),
        ],
        out_specs=pl.BlockSpec((tm, d), lambda i: (i, 0)),
        out_shape=jax.ShapeDtypeStruct((s, d), F32),
        compiler_params=_params(("arbitrary",)),
        name="ffn_down",
    )(a, w_down, x1, nf)


def kernel(x, norm_mix, w_in, conv_qk_w, conv_qk_b, b_igate, b_fgate, m_norm, rel_bias, gate_bias,
           w_branch_m, w_branch_a, w_out, norm_ffn, w_up, conv_ffn_w, conv_ffn_b, w_down, norm_final):
    batch, seq, d = x.shape
    depth = w_in.shape[0]
    assert (batch, seq, d, depth) == (1, SEQ, D_MODEL, 1)
    xs = x[0]
    l = 0

    w_in_t = jnp.swapaxes(w_in[l], 0, 1)
    w_g = jnp.pad(w_in_t[GATE_COL0:GATE_COL0 + N_GATES, :].T, ((0, 0), (0, LANES - N_GATES)))
    gate_b = jnp.concatenate([b_igate[l], b_fgate[l]])
    gb_col = jnp.broadcast_to(gate_b[:, None], (SUBLANES, LANES))
    bias = _band_bias(rel_bias[l])

    h, gt = _prenorm(xs, norm_mix[l][None, :], w_g.astype(BF16))
    na = 2 * M_QK + 2 * M_V
    attn0 = GATE_COL0 + N_GATES
    pa, w_down_bf = _proj(h, w_in_t, 0, na, F32, "proj_mlstm", sides=[_cast_side(w_down[l], 256)])
    mlstm = _mlstm_side(pa, gt, conv_qk_w[l], conv_qk_b[l][None, :], gb_col, m_norm[l][None, :])
    pb, hm, w_out_bf, w_m_bf, w_a_bf = _proj(
        h, w_in_t, attn0, 3 * A_W, BF16, "proj_attn_mlstm", tn=768,
        sides=[mlstm, _cast_side(w_out[l], 128), _cast_side(w_branch_m[l], 128),
               _cast_side(w_branch_a[l], 128)])
    pc, ha = _proj(h, w_in_t, attn0 + 3 * A_W, 2 * D_MODEL, BF16, "proj_gates_attn",
                   sides=[_attn_side(pb, bias)])

    x1, h2 = _merge(hm, ha, pc, xs, w_m_bf, w_a_bf, w_out_bf, gate_bias[l], norm_ffn[l][None, :])
    a = _ffn_up(h2, w_up[l], conv_ffn_w[l], conv_ffn_b[l][None, :])
    out = _ffn_down(a, w_down_bf, x1, norm_final[None, :])
    return out[None]
```

```python
import functools
import math
from typing import Callable, NamedTuple

import jax
import jax.numpy as jnp
from jax import lax
from jax.experimental import pallas as pl
from jax.experimental.pallas import tpu as pltpu

D_MODEL = 2048
SEQ = 8192
CHUNK = 64
M_HEADS = 4
M_QK_DIM = 128
M_V_DIM = 256
M_CONV = 4
A_HEADS = 8
A_HEAD_DIM = 128
A_PAST_CHUNKS = 8
A_MAX_REL = 128
D_FF = 5632
FFN_CONV = 3
EPS = 1e-6
NEG = -1e30
LOG2E = math.log2(math.e)

M_QK = M_HEADS * M_QK_DIM
M_V = M_HEADS * M_V_DIM
A_W = A_HEADS * A_HEAD_DIM
GATE_COL0 = 2 * M_QK + 2 * M_V
N_GATES = 2 * M_HEADS

LANES = 128
SUBLANES = 8
VMEM_LIMIT = 56 * 1024 * 1024

BF16 = jnp.bfloat16
F32 = jnp.float32


def _params(sem, vmem=VMEM_LIMIT, flags=None):
    return pltpu.CompilerParams(dimension_semantics=sem, vmem_limit_bytes=vmem, flags=flags)


def _dot(a, b):
    return jnp.dot(a, b, preferred_element_type=F32)


def _dot_nt(a, b):
    return lax.dot_general(a, b, (((1,), (1,)), ((), ())), preferred_element_type=F32)


def _sigmoid(z):
    return 0.5 * jnp.tanh(0.5 * z) + 0.5


def _log_sigmoid(z):
    return jnp.minimum(z, 0.0) - jnp.log(1.0 + jnp.exp(-jnp.abs(z)))


def _rms_scale(y):
    return lax.rsqrt(jnp.mean(y * y, axis=-1, keepdims=True) + EPS)


def _prenorm_kernel(x_ref, g_ref, wg_ref, h_ref, gt_ref):
    xf = x_ref[...]
    hb = (xf * _rms_scale(xf) * g_ref[...]).astype(BF16)
    h_ref[...] = hb
    gcol = _dot(hb, wg_ref[...])
    gt_ref[...] = gcol.T[0:SUBLANES, :]


def _prenorm(x, g, wg, tm=512):
    s, d = x.shape
    return pl.pallas_call(
        _prenorm_kernel,
        grid=(s // tm,),
        in_specs=[
            pl.BlockSpec((tm, d), lambda i: (i, 0)),
            pl.BlockSpec((1, d), lambda i: (0, 0)),
            pl.BlockSpec((d, LANES), lambda i: (0, 0)),
        ],
        out_specs=[
            pl.BlockSpec((tm, d), lambda i: (i, 0)),
            pl.BlockSpec((SUBLANES, tm), lambda i: (0, i)),
        ],
        out_shape=[
            jax.ShapeDtypeStruct((s, d), BF16),
            jax.ShapeDtypeStruct((SUBLANES, s), F32),
        ],
        compiler_params=_params(("arbitrary",)),
        name="prenorm",
    )(x, g, wg)


class _Spec(NamedTuple):
    block: tuple
    index: Callable
    mode: object = None


class _Side(NamedTuple):
    body: Callable
    init: Callable
    args: tuple
    in_specs: tuple
    out_specs: tuple
    out_shapes: tuple
    scratch: tuple
    steps: int
    clamped: bool = False
    places: bool = False


PROJ_SLICE = 256


def _interleave(i, n_units, emit, n_parts):
    for k in range(i * n_parts // n_units, (i + 1) * n_parts // n_units):
        emit(k)


def _proj_kernel(*refs, shift, tn, sides):
    n_w = 3 if shift else 2
    h_ref, wm_ref = refs[0], refs[1]
    pos = n_w
    side_in = []
    for sd in sides:
        side_in.append(refs[pos:pos + len(sd.args)])
        pos += len(sd.args)
    o_ref = refs[pos]
    pos += 1
    side_out = []
    for sd in sides:
        side_out.append(refs[pos:pos + len(sd.out_specs)])
        pos += len(sd.out_specs)
    w_bf = refs[pos]
    pos += 1
    side_scratch = []
    for sd in sides:
        side_scratch.append(refs[pos:pos + len(sd.scratch)])
        pos += len(sd.scratch)

    @pl.when(pl.program_id(1) == 0)
    def _():
        if shift:
            w_bf[0:tn - shift, :] = wm_ref[shift:tn, :].astype(BF16)
            w_bf[tn - shift:tn, :] = refs[2][...].astype(BF16)
        else:
            w_bf[...] = wm_ref[...].astype(BF16)

    step = pl.program_id(0) * pl.num_programs(1) + pl.program_id(1)
    for sd, scr in zip(sides, side_scratch):
        if sd.init:
            sd.init(step, scr)

    def emit(k):
        cols = slice(k * PROJ_SLICE, (k + 1) * PROJ_SLICE)
        o_ref[:, cols] = _dot_nt(h_ref[...], w_bf[cols, :]).astype(o_ref.dtype)

    n_parts = tn // PROJ_SLICE
    placers = [idx for idx, sd in enumerate(sides) if sd.places]
    assert len(placers) <= 1
    for idx, sd in enumerate(sides):
        if not sd.places:
            sd.body(step, side_in[idx], side_out[idx], side_scratch[idx], None, 0)
    if placers:
        idx = placers[0]
        sides[idx].body(step, side_in[idx], side_out[idx], side_scratch[idx], emit, n_parts)
    else:
        for k in range(n_parts):
            emit(k)


def _proj(h, wt, col0, ncols, out_dtype, name, tm=1024, tn=1024, sides=()):
    s, d = h.shape
    shift = col0 % tn
    base = col0 - shift
    n_m = s // tm
    n_steps = (ncols // tn) * n_m
    assert shift % SUBLANES == 0 and ncols % tn == 0
    assert all(sd.steps == n_steps or (sd.clamped and sd.steps < n_steps) for sd in sides)

    def flat(spec):
        return pl.BlockSpec(spec.block, lambda j, m: spec.index(j * n_m + m),
                            pipeline_mode=spec.mode)

    in_specs = [
        pl.BlockSpec((tm, d), lambda j, m: (m, 0)),
        pl.BlockSpec((tn, d), lambda j, m: (base // tn + j, 0)),
    ]
    args = [h, wt]
    if shift:
        in_specs.append(pl.BlockSpec((shift, d), lambda j, m: ((base + (j + 1) * tn) // shift, 0)))
        args.append(wt)
    out_specs = [pl.BlockSpec((tm, tn), lambda j, m: (m, j))]
    out_shapes = [jax.ShapeDtypeStruct((s, ncols), out_dtype)]
    scratch = [pltpu.VMEM((tn, d), BF16)]
    for sd in sides:
        in_specs += [flat(sp) for sp in sd.in_specs]
        args += list(sd.args)
    for sd in sides:
        out_specs += [flat(sp) for sp in sd.out_specs]
        out_shapes += list(sd.out_shapes)
    for sd in sides:
        scratch += list(sd.scratch)
    return pl.pallas_call(
        functools.partial(_proj_kernel, shift=shift, tn=tn, sides=tuple(sides)),
        grid=(ncols // tn, n_m),
        in_specs=in_specs,
        out_specs=out_specs,
        out_shape=out_shapes,
        scratch_shapes=scratch,
        compiler_params=_params(("arbitrary", "arbitrary")),
        name=name,
    )(*args)


def _cast_body(step, in_refs, out_refs, scratch_refs, emit, n_parts):
    del step, scratch_refs, emit, n_parts
    out_refs[0][...] = in_refs[0][...].astype(BF16)


def _cast_side(w, rows):
    n, d = w.shape
    nb = n // rows
    spec = _Spec((rows, d), lambda i: (jnp.minimum(i, nb - 1), 0))
    return _Side(body=_cast_body, init=None, args=(w,), in_specs=(spec,), out_specs=(spec,),
                 out_shapes=(jax.ShapeDtypeStruct((n, d), BF16),), scratch=(), steps=nb,
                 clamped=True)


MLSTM_BLOCK = 256


def _cumsum_lanes(v):
    n = v.shape[-1]
    pos = lax.broadcasted_iota(jnp.int32, v.shape, v.ndim - 1)
    shift = 1
    while shift < n:
        v = v + jnp.where(pos >= shift, pltpu.roll(v, shift, v.ndim - 1), 0.0)
        shift *= 2
    return v


def _mlstm_body(step, in_refs, out_refs, scratch_refs, emit, n_parts):
    q_ref, k_ref, v_ref, o_ref, gt_ref, cw_ref, cb_ref, gbc_ref, mn_ref = in_refs
    (out_ref,) = out_refs
    qk_buf, c_s, m_s = scratch_refs
    del step
    halo = SUBLANES
    tc = MLSTM_BLOCK

    qk_buf[halo:halo + tc, 0:M_QK] = q_ref[...]
    qk_buf[halo:halo + tc, M_QK:2 * M_QK] = k_ref[...]
    acc = jnp.broadcast_to(cb_ref[...], (tc, 2 * M_QK))
    for tap in range(M_CONV):
        lo = halo - (M_CONV - 1) + tap
        acc = acc + qk_buf[lo:lo + tc, :] * cw_ref[tap:tap + 1, :]
    qk_buf[0:halo, :] = qk_buf[tc:tc + halo, :]
    qk = acc * _sigmoid(acc)
    q_all = qk[:, 0:M_QK].astype(BF16)
    k_all = qk[:, M_QK:2 * M_QK] * (M_QK_DIM ** -0.5)

    zt = gt_ref[...] + gbc_ref[:, 0:1]
    lf_all = _log_sigmoid(zt)
    b_all = _cumsum_lanes(lf_all)

    row = lax.broadcasted_iota(jnp.int32, (tc, tc), 0)
    col = lax.broadcasted_iota(jnp.int32, (tc, tc), 1)
    causal = col <= row
    ones_aug = jnp.ones((tc, LANES), BF16)

    def first(h):
        qcols = slice(h * M_QK_DIM, (h + 1) * M_QK_DIM)
        vcols = slice(h * M_V_DIM, (h + 1) * M_V_DIM)
        f = M_HEADS + h
        q_h = q_all[:, qcols]
        kt = k_all[:, qcols].T
        m_prev = m_s[h][0:1, 0:1]
        c_prev = c_s[h]
        u = zt[h:h + 1, :] - b_all[f:f + 1, :]
        b_tot = b_all[f:f + 1, tc - 1:tc]
        v_aug = jnp.concatenate([v_ref[:, vcols].astype(BF16), ones_aug], axis=1)
        qk = _dot(q_h, kt.astype(BF16))
        qc = _dot(q_h, c_prev.astype(BF16))
        w_log = b_tot + u
        m_loc = jnp.max(w_log, axis=1, keepdims=True)
        wkt = (kt * jnp.exp(w_log - m_loc)).astype(BF16)
        m_new = jnp.maximum(b_tot + m_prev, m_loc)
        decay = jnp.exp(b_tot + m_prev - m_new)
        gain = jnp.exp(m_loc - m_new)
        m_s[h] = jnp.broadcast_to(m_new, (SUBLANES, LANES))
        return (u, m_prev, v_aug, qk, qc), (wkt, v_aug, c_prev, decay, gain)

    def third(h, wkt, v_aug, c_prev, decay, gain):
        c_s[h] = decay * c_prev + gain * _dot(wkt, v_aug)

    def second(h, u, m_prev, v_aug, qk, qc):
        vcols = slice(h * M_V_DIM, (h + 1) * M_V_DIM)
        f = M_HEADS + h
        um = jnp.where(causal, u, NEG)
        g = jnp.maximum(jnp.max(um, axis=1, keepdims=True), m_prev)
        b_col = jnp.sum(jnp.where(causal, lf_all[f:f + 1, :], 0.0), axis=1, keepdims=True)
        p = jnp.exp(um - g) * qk
        na = _dot(p.astype(BF16), v_aug) + jnp.exp(m_prev - g) * qc
        num = na[:, 0:M_V_DIM]
        den = na[:, M_V_DIM:M_V_DIM + 1]
        hh = num / jnp.maximum(jnp.abs(den), jnp.exp(-(b_col + g)))
        hh = hh * _rms_scale(hh) * mn_ref[0:1, vcols] * _sigmoid(o_ref[:, vcols])
        out_ref[:, vcols] = hh.astype(BF16)

    emit(0)
    parts = [first(h) for h in range(M_HEADS)]
    groups = n_parts - 1
    per = M_HEADS // groups
    for grp in range(groups):
        emit(1 + grp)
        for h in range(grp * per, (grp + 1) * per):
            second(h, *parts[h][0])
    for h in range(M_HEADS):
        third(h, *parts[h][1])


def _mlstm_init(step, scratch_refs):
    qk_buf, c_s, m_s = scratch_refs

    @pl.when(step == 0)
    def _():
        qk_buf[0:SUBLANES, :] = jnp.zeros((SUBLANES, 2 * M_QK), F32)
        c_s[...] = jnp.zeros(c_s.shape, F32)
        m_s[...] = jnp.full(m_s.shape, NEG, F32)


def _mlstm_side(pa, gt, cw, cb, gbc, mn):
    s = pa.shape[0]
    tc = MLSTM_BLOCK
    v_blk = (2 * M_QK) // M_V
    const = lambda i: (0, 0)
    return _Side(
        body=_mlstm_body,
        init=_mlstm_init,
        args=(pa, pa, pa, pa, gt, cw, cb, gbc, mn),
        in_specs=(
            _Spec((tc, M_QK), lambda i: (i, 0)),
            _Spec((tc, M_QK), lambda i: (i, 1)),
            _Spec((tc, M_V), lambda i: (i, v_blk)),
            _Spec((tc, M_V), lambda i: (i, v_blk + 1)),
            _Spec((SUBLANES, tc), lambda i: (0, i)),
            _Spec((M_CONV, 2 * M_QK), const),
            _Spec((1, 2 * M_QK), const),
            _Spec((SUBLANES, LANES), const),
            _Spec((1, M_V), const),
        ),
        out_specs=(_Spec((tc, M_V), lambda i: (i, 0)),),
        out_shapes=(jax.ShapeDtypeStruct((s, M_V), BF16),),
        scratch=(
            pltpu.VMEM((SUBLANES + tc, 2 * M_QK), F32),
            pltpu.VMEM((M_HEADS, M_QK_DIM, M_V_DIM + LANES), F32),
            pltpu.VMEM((M_HEADS, SUBLANES, LANES), F32),
        ),
        steps=s // tc,
        places=True,
    )


ATT_G = 4
ATT_ROWS = ATT_G * CHUNK
ATT_KBLOCKS = (A_PAST_CHUNKS + ATT_G) // ATT_G


def _attn_body(step, in_refs, out_refs, scratch_refs, emit, n_parts):
    del step, scratch_refs
    q_ref, k0_ref, k1_ref, k2_ref, v0_ref, v1_ref, v2_ref, bias_ref = in_refs
    (out_ref,) = out_refs
    scale2 = (A_HEAD_DIM ** -0.5) * LOG2E
    ones = jnp.ones((ATT_KBLOCKS * ATT_ROWS, A_HEAD_DIM), BF16)

    def scores(h):
        cols = slice(h * A_HEAD_DIM, (h + 1) * A_HEAD_DIM)
        k = jnp.concatenate([k0_ref[:, cols], k1_ref[:, cols], k2_ref[:, cols]], axis=0)
        return _dot_nt(q_ref[:, cols], k)

    def finish(h, s):
        cols = slice(h * A_HEAD_DIM, (h + 1) * A_HEAD_DIM)
        s = s * scale2 + bias_ref[0, h]
        p = jnp.exp2(s - jnp.max(s, axis=1, keepdims=True)).astype(BF16)
        v = jnp.concatenate([v0_ref[:, cols], v1_ref[:, cols], v2_ref[:, cols]], axis=0)
        o = _dot(p, jnp.concatenate([v, ones], axis=1))
        out_ref[:, cols] = (o[:, 0:A_HEAD_DIM] / o[:, A_HEAD_DIM:A_HEAD_DIM + 1]).astype(BF16)

    per = A_HEADS // n_parts
    s_next = [scores(h) for h in range(per)]
    for grp in range(n_parts):
        s_grp = s_next
        if grp + 1 < n_parts:
            s_next = [scores(h) for h in range((grp + 1) * per, (grp + 2) * per)]
        emit(grp)
        for h, s in zip(range(grp * per, (grp + 1) * per), s_grp):
            finish(h, s)


def _attn_side(pb, bias):
    s = pb.shape[0]
    last = ATT_KBLOCKS - 1

    def kv(group, back):
        return _Spec((ATT_ROWS, A_W), lambda g: (jnp.maximum(g - back, 0), group))

    return _Side(
        body=_attn_body,
        init=None,
        args=(pb,) * 7 + (bias,),
        in_specs=(
            _Spec((ATT_ROWS, A_W), lambda g: (g, 0)),
            kv(1, 2), kv(1, 1), kv(1, 0),
            kv(2, 2), kv(2, 1), kv(2, 0),
            _Spec((1, A_HEADS, ATT_ROWS, ATT_KBLOCKS * ATT_ROWS),
                  lambda g: (jnp.minimum(g, last), 0, 0, 0), pl.Buffered(1)),
        ),
        out_specs=(_Spec((ATT_ROWS, A_W), lambda g: (g, 0)),),
        out_shapes=(jax.ShapeDtypeStruct((s, A_W), BF16),),
        scratch=(),
        steps=s // ATT_ROWS,
        places=True,
    )


def _band_bias(rel_table):
    hds = rel_table.shape[0]
    wcols = ATT_KBLOCKS * ATT_ROWS
    band_w = (A_PAST_CHUNKS + 1) * CHUNK
    far = A_PAST_CHUNKS * CHUNK - A_MAX_REL + CHUNK
    tab = rel_table.astype(F32) * LOG2E
    e = jnp.concatenate([jnp.broadcast_to(tab[:, 2 * A_MAX_REL:], (hds, far)),
                         tab[:, A_MAX_REL - CHUNK + 1:2 * A_MAX_REL][:, ::-1],
                         jnp.zeros((hds, 1), F32)], axis=1)
    period = e.shape[1]
    tiled = jnp.broadcast_to(e[:, None, :], (hds, CHUNK, period)).reshape(hds, -1)
    skew = tiled[:, :CHUNK * (period - 1)].reshape(hds, CHUNK, period - 1)
    chunk_bias = skew[:, :, CHUNK - 1:CHUNK - 1 + band_w]
    rows = [jnp.pad(chunk_bias, ((0, 0), (0, 0), (ci * CHUNK, wcols - band_w - ci * CHUNK)),
                    constant_values=NEG) for ci in range(ATT_G)]
    base = jnp.concatenate(rows, axis=1)
    first_valid = (ATT_KBLOCKS - 1 - jnp.arange(ATT_KBLOCKS)) * ATT_ROWS
    valid = jnp.arange(wcols)[None, :] >= first_valid[:, None]
    return jnp.where(valid[:, None, None, :], base[None], NEG)


def _merge_kernel(hm_ref, ha_ref, gm_ref, ga_ref, x_ref, wm_ref, wa_ref, wo_ref, gb_ref, nf_ref,
                  x1_ref, h2_ref, merged_s):
    d = x_ref.shape[1]
    for n in range(d // MERGE_COLS):
        cols = slice(n * MERGE_COLS, (n + 1) * MERGE_COLS)
        ym = _dot(hm_ref[...], wm_ref[:, cols])
        ya = _dot(ha_ref[...], wa_ref[:, cols])
        merged_s[:, cols] = (_sigmoid(gm_ref[:, cols].astype(F32) + gb_ref[0:1, cols]) * ym
                             + _sigmoid(ga_ref[:, cols].astype(F32) + gb_ref[1:2, cols]) * ya
                             ).astype(BF16)
    ssq = jnp.zeros((x_ref.shape[0], 1), F32)
    for n in range(d // MERGE_COLS):
        cols = slice(n * MERGE_COLS, (n + 1) * MERGE_COLS)
        x1 = x_ref[:, cols] + _dot(merged_s[...], wo_ref[:, cols])
        x1_ref[:, cols] = x1
        ssq = ssq + jnp.sum(x1 * x1, axis=-1, keepdims=True)
    scale = lax.rsqrt(ssq * (1.0 / d) + EPS)
    h2_ref[...] = (x1_ref[...] * scale * nf_ref[...]).astype(BF16)


MERGE_COLS = 512


def _merge(hm, ha, pc, x, wm, wa, wo, gb, nf, tm=512):
    s, d = x.shape
    const = lambda i: (0, 0)
    resident = pl.Buffered(1)
    return pl.pallas_call(
        _merge_kernel,
        grid=(s // tm,),
        in_specs=[
            pl.BlockSpec((tm, M_V), lambda i: (i, 0)),
            pl.BlockSpec((tm, A_W), lambda i: (i, 0)),
            pl.BlockSpec((tm, d), lambda i: (i, 0)),
            pl.BlockSpec((tm, d), lambda i: (i, 1)),
            pl.BlockSpec((tm, d), lambda i: (i, 0)),
            pl.BlockSpec((M_V, d), const, pipeline_mode=resident),
            pl.BlockSpec((A_W, d), const, pipeline_mode=resident),
            pl.BlockSpec((d, d), const, pipeline_mode=resident),
            pl.BlockSpec((2, d), const),
            pl.BlockSpec((1, d), const),
        ],
        out_specs=[
            pl.BlockSpec((tm, d), lambda i: (i, 0)),
            pl.BlockSpec((tm, d), lambda i: (i, 0)),
        ],
        out_shape=[
            jax.ShapeDtypeStruct((s, d), F32),
            jax.ShapeDtypeStruct((s, d), BF16),
        ],
        scratch_shapes=[pltpu.VMEM((tm, d), BF16)],
        compiler_params=_params(("arbitrary",)),
        name="merge",
    )(hm, ha, pc, pc, x, wm, wa, wo, gb, nf)


FFN_SUB_ROWS = 512


def _ffn_up_kernel(h_ref, wg_ref, wv_ref, cw_ref, cb_ref, a_ref, w_bf, ug_buf, *, tm, tf):
    halo = SUBLANES
    rs = FFN_SUB_ROWS

    @pl.when(pl.program_id(1) == 0)
    def _():
        w_bf[:, 0:tf] = wg_ref[...].astype(BF16)
        w_bf[:, tf:2 * tf] = wv_ref[...].astype(BF16)
        ug_buf[0:halo, :] = jnp.zeros((halo, tf), F32)

    for r in range(tm // rs):
        h = h_ref[r * rs:(r + 1) * rs, :]
        ug = _dot(h, w_bf[:, 0:tf])
        uv = _dot(h, w_bf[:, tf:2 * tf])
        base = halo + r * rs
        ug_buf[base:base + rs, :] = ug
        conv = cb_ref[...] + ug * cw_ref[FFN_CONV - 1:FFN_CONV, :]
        for tap in range(FFN_CONV - 1):
            lo = base - (FFN_CONV - 1) + tap
            conv = conv + ug_buf[lo:lo + rs, :] * cw_ref[tap:tap + 1, :]
        a_ref[r * rs:(r + 1) * rs, :] = (conv * _sigmoid(conv) * uv).astype(BF16)
    ug_buf[0:halo, :] = ug_buf[tm:tm + halo, :]


def _ffn_up(h2, w_up, cw, cb, tm=2048, tf=512):
    s, d = h2.shape
    nf = D_FF // tf
    return pl.pallas_call(
        functools.partial(_ffn_up_kernel, tm=tm, tf=tf),
        grid=(nf, s // tm),
        in_specs=[
            pl.BlockSpec((tm, d), lambda j, m: (m, 0)),
            pl.BlockSpec((d, tf), lambda j, m: (0, j)),
            pl.BlockSpec((d, tf), lambda j, m: (0, nf + j)),
            pl.BlockSpec((FFN_CONV, tf), lambda j, m: (0, j)),
            pl.BlockSpec((1, tf), lambda j, m: (0, j)),
        ],
        out_specs=pl.BlockSpec((tm, tf), lambda j, m: (m, j)),
        out_shape=jax.ShapeDtypeStruct((s, D_FF), BF16),
        scratch_shapes=[
            pltpu.VMEM((d, 2 * tf), BF16),
            pltpu.VMEM((SUBLANES + tm, tf), F32),
        ],
        compiler_params=_params(("arbitrary", "arbitrary")),
        name="ffn_up",
    )(h2, w_up, w_up, cw, cb)


def _ffn_down_kernel(a_ref, w_ref, x1_ref, nf_ref, o_ref):
    y = x1_ref[...] + _dot(a_ref[...], w_ref[...])
    o_ref[...] = y * _rms_scale(y) * nf_ref[...]


def _ffn_down(a, w_down, x1, nf, tm=256):
    s, d = x1.shape
    dff = a.shape[1]
    return pl.pallas_call(
        _ffn_down_kernel,
        grid=(s // tm,),
        in_specs=[
            pl.BlockSpec((tm, dff), lambda i: (i, 0)),
            pl.BlockSpec((dff, d), lambda i: (0, 0), pipeline_mode=pl.Buffered(1)),
            pl.BlockSpec((tm, d), lambda i: (i, 0)),
            pl.BlockSpec((1, d), lambda i: (0, 0)),
        ],
        out_specs=pl.BlockSpec((tm, d), lambda i: (i, 0)),
        out_shape=jax.ShapeDtypeStruct((s, d), F32),
        compiler_params=_params(("arbitrary",)),
        name="ffn_down",
    )(a, w_down, x1, nf)


def kernel(x, norm_mix, w_in, conv_qk_w, conv_qk_b, b_igate, b_fgate, m_norm, rel_bias, gate_bias,
           w_branch_m, w_branch_a, w_out, norm_ffn, w_up, conv_ffn_w, conv_ffn_b, w_down, norm_final):
    batch, seq, d = x.shape
    depth = w_in.shape[0]
    assert (batch, seq, d, depth) == (1, SEQ, D_MODEL, 1)
    xs = x[0]
    l = 0

    w_in_t = jnp.swapaxes(w_in[l], 0, 1)
    w_g = jnp.pad(w_in_t[GATE_COL0:GATE_COL0 + N_GATES, :].T, ((0, 0), (0, LANES - N_GATES)))
    gate_b = jnp.concatenate([b_igate[l], b_fgate[l]])
    gb_col = jnp.broadcast_to(gate_b[:, None], (SUBLANES, LANES))
    bias = _band_bias(rel_bias[l])

    h, gt = _prenorm(xs, norm_mix[l][None, :], w_g.astype(BF16))
    na = 2 * M_QK + 2 * M_V
    attn0 = GATE_COL0 + N_GATES
    pa, w_down_bf = _proj(h, w_in_t, 0, na, F32, "proj_mlstm", sides=[_cast_side(w_down[l], 256)])
    mlstm = _mlstm_side(pa, gt, conv_qk_w[l], conv_qk_b[l][None, :], gb_col, m_norm[l][None, :])
    pb, hm, w_out_bf, w_m_bf, w_a_bf = _proj(
        h, w_in_t, attn0, 3 * A_W, BF16, "proj_attn_mlstm", tn=768,
        sides=[mlstm, _cast_side(w_out[l], 128), _cast_side(w_branch_m[l], 128),
               _cast_side(w_branch_a[l], 128)])
    pc, ha = _proj(h, w_in_t, attn0 + 3 * A_W, 2 * D_MODEL, BF16, "proj_gates_attn",
                   sides=[_attn_side(pb, bias)])

    x1, h2 = _merge(hm, ha, pc, xs, w_m_bf, w_a_bf, w_out_bf, gate_bias[l], norm_ffn[l][None, :])
    a = _ffn_up(h2, w_up[l], conv_ffn_w[l], conv_ffn_b[l][None, :])
    out = _ffn_down(a, w_down_bf, x1, norm_final[None, :])
    return out[None]
```

```python
import functools
import math
from typing import Callable, NamedTuple

import jax
import jax.numpy as jnp
from jax import lax
from jax.experimental import pallas as pl
from jax.experimental.pallas import tpu as pltpu

D_MODEL = 2048
SEQ = 8192
CHUNK = 64
M_HEADS = 4
M_QK_DIM = 128
M_V_DIM = 256
M_CONV = 4
A_HEADS = 8
A_HEAD_DIM = 128
A_PAST_CHUNKS = 8
A_MAX_REL = 128
D_FF = 5632
FFN_CONV = 3
EPS = 1e-6
NEG = -1e30
LOG2E = math.log2(math.e)

M_QK = M_HEADS * M_QK_DIM
M_V = M_HEADS * M_V_DIM
A_W = A_HEADS * A_HEAD_DIM
GATE_COL0 = 2 * M_QK + 2 * M_V
N_GATES = 2 * M_HEADS

LANES = 128
SUBLANES = 8
VMEM_LIMIT = 56 * 1024 * 1024

BF16 = jnp.bfloat16
F32 = jnp.float32


def _params(sem, vmem=VMEM_LIMIT, flags=None):
    return pltpu.CompilerParams(dimension_semantics=sem, vmem_limit_bytes=vmem, flags=flags)


def _dot(a, b):
    return jnp.dot(a, b, preferred_element_type=F32)


def _dot_nt(a, b):
    return lax.dot_general(a, b, (((1,), (1,)), ((), ())), preferred_element_type=F32)


def _sigmoid(z):
    return 0.5 * jnp.tanh(0.5 * z) + 0.5


def _log_sigmoid(z):
    return jnp.minimum(z, 0.0) - jnp.log(1.0 + jnp.exp(-jnp.abs(z)))


def _rms_scale(y):
    return lax.rsqrt(jnp.mean(y * y, axis=-1, keepdims=True) + EPS)


def _prenorm_kernel(x_ref, g_ref, wg_ref, h_ref, gt_ref):
    xf = x_ref[...]
    hb = (xf * _rms_scale(xf) * g_ref[...]).astype(BF16)
    h_ref[...] = hb
    gcol = _dot(hb, wg_ref[...])
    gt_ref[...] = gcol.T[0:SUBLANES, :]


def _prenorm(x, g, wg, tm=512):
    s, d = x.shape
    return pl.pallas_call(
        _prenorm_kernel,
        grid=(s // tm,),
        in_specs=[
            pl.BlockSpec((tm, d), lambda i: (i, 0)),
            pl.BlockSpec((1, d), lambda i: (0, 0)),
            pl.BlockSpec((d, LANES), lambda i: (0, 0)),
        ],
        out_specs=[
            pl.BlockSpec((tm, d), lambda i: (i, 0)),
            pl.BlockSpec((SUBLANES, tm), lambda i: (0, i)),
        ],
        out_shape=[
            jax.ShapeDtypeStruct((s, d), BF16),
            jax.ShapeDtypeStruct((SUBLANES, s), F32),
        ],
        compiler_params=_params(("arbitrary",)),
        name="prenorm",
    )(x, g, wg)


class _Spec(NamedTuple):
    block: tuple
    index: Callable
    mode: object = None


class _Side(NamedTuple):
    body: Callable
    init: Callable
    args: tuple
    in_specs: tuple
    out_specs: tuple
    out_shapes: tuple
    scratch: tuple
    steps: int
    clamped: bool = False
    places: int = 0


PROJ_SLICE = 256


def _interleave(i, n_units, emit, n_parts):
    for k in range(i * n_parts // n_units, (i + 1) * n_parts // n_units):
        emit(k)


def _proj_kernel(*refs, shift, tn, sides):
    n_w = 3 if shift else 2
    h_ref, wm_ref = refs[0], refs[1]
    pos = n_w
    side_in = []
    for sd in sides:
        side_in.append(refs[pos:pos + len(sd.args)])
        pos += len(sd.args)
    o_ref = refs[pos]
    pos += 1
    side_out = []
    for sd in sides:
        side_out.append(refs[pos:pos + len(sd.out_specs)])
        pos += len(sd.out_specs)
    w_bf = refs[pos]
    pos += 1
    side_scratch = []
    for sd in sides:
        side_scratch.append(refs[pos:pos + len(sd.scratch)])
        pos += len(sd.scratch)

    @pl.when(pl.program_id(1) == 0)
    def _():
        if shift:
            w_bf[0:tn - shift, :] = wm_ref[shift:tn, :].astype(BF16)
            w_bf[tn - shift:tn, :] = refs[2][...].astype(BF16)
        else:
            w_bf[...] = wm_ref[...].astype(BF16)

    step = pl.program_id(0) * pl.num_programs(1) + pl.program_id(1)
    for sd, scr in zip(sides, side_scratch):
        if sd.init:
            sd.init(step, scr)

    placers = [idx for idx, sd in enumerate(sides) if sd.places]
    assert len(placers) <= 1
    n_parts = sides[placers[0]].places if placers else 1
    width = tn // n_parts
    assert width % PROJ_SLICE == 0

    def emit(k):
        cols = slice(k * width, (k + 1) * width)
        o_ref[:, cols] = _dot_nt(h_ref[...], w_bf[cols, :]).astype(o_ref.dtype)

    for idx, sd in enumerate(sides):
        if not sd.places:
            sd.body(step, side_in[idx], side_out[idx], side_scratch[idx], None, 0)
    if placers:
        idx = placers[0]
        sides[idx].body(step, side_in[idx], side_out[idx], side_scratch[idx], emit, n_parts)
    else:
        emit(0)


def _proj(h, wt, col0, ncols, out_dtype, name, tm=1024, tn=1024, sides=()):
    s, d = h.shape
    shift = col0 % tn
    base = col0 - shift
    n_m = s // tm
    n_steps = (ncols // tn) * n_m
    assert shift % SUBLANES == 0 and ncols % tn == 0
    assert all(sd.steps == n_steps or (sd.clamped and sd.steps < n_steps) for sd in sides)

    def flat(spec):
        return pl.BlockSpec(spec.block, lambda j, m: spec.index(j * n_m + m),
                            pipeline_mode=spec.mode)

    in_specs = [
        pl.BlockSpec((tm, d), lambda j, m: (m, 0)),
        pl.BlockSpec((tn, d), lambda j, m: (base // tn + j, 0)),
    ]
    args = [h, wt]
    if shift:
        in_specs.append(pl.BlockSpec((shift, d), lambda j, m: ((base + (j + 1) * tn) // shift, 0)))
        args.append(wt)
    out_specs = [pl.BlockSpec((tm, tn), lambda j, m: (m, j))]
    out_shapes = [jax.ShapeDtypeStruct((s, ncols), out_dtype)]
    scratch = [pltpu.VMEM((tn, d), BF16)]
    for sd in sides:
        in_specs += [flat(sp) for sp in sd.in_specs]
        args += list(sd.args)
    for sd in sides:
        out_specs += [flat(sp) for sp in sd.out_specs]
        out_shapes += list(sd.out_shapes)
    for sd in sides:
        scratch += list(sd.scratch)
    return pl.pallas_call(
        functools.partial(_proj_kernel, shift=shift, tn=tn, sides=tuple(sides)),
        grid=(ncols // tn, n_m),
        in_specs=in_specs,
        out_specs=out_specs,
        out_shape=out_shapes,
        scratch_shapes=scratch,
        compiler_params=_params(("arbitrary", "arbitrary")),
        name=name,
    )(*args)


def _cast_body(step, in_refs, out_refs, scratch_refs, emit, n_parts):
    del step, scratch_refs, emit, n_parts
    out_refs[0][...] = in_refs[0][...].astype(BF16)


def _cast_side(w, rows):
    n, d = w.shape
    nb = n // rows
    spec = _Spec((rows, d), lambda i: (jnp.minimum(i, nb - 1), 0))
    return _Side(body=_cast_body, init=None, args=(w,), in_specs=(spec,), out_specs=(spec,),
                 out_shapes=(jax.ShapeDtypeStruct((n, d), BF16),), scratch=(), steps=nb,
                 clamped=True)


MLSTM_BLOCK = 256


def _cumsum_lanes(v):
    n = v.shape[-1]
    pos = lax.broadcasted_iota(jnp.int32, v.shape, v.ndim - 1)
    shift = 1
    while shift < n:
        v = v + jnp.where(pos >= shift, pltpu.roll(v, shift, v.ndim - 1), 0.0)
        shift *= 2
    return v


def _mlstm_body(step, in_refs, out_refs, scratch_refs, emit, n_parts):
    pa_ref, gt_ref, cw_ref, cb_ref, gbc_ref, mn_ref = in_refs
    v_off = 2 * M_QK
    o_off = 2 * M_QK + M_V
    (out_ref,) = out_refs
    qk_buf, c_s, m_s = scratch_refs
    del step
    halo = SUBLANES
    tc = MLSTM_BLOCK

    qk_buf[halo:halo + tc, :] = pa_ref[:, 0:2 * M_QK].astype(F32)
    cw_half = cw_ref[...] * 0.5
    t = jnp.broadcast_to(cb_ref[...] * 0.5, (tc, 2 * M_QK))
    for tap in range(M_CONV):
        lo = halo - (M_CONV - 1) + tap
        t = t + qk_buf[lo:lo + tc, :] * cw_half[tap:tap + 1, :]
    qk_buf[0:halo, :] = qk_buf[tc:tc + halo, :]
    qk = t * (1.0 + jnp.tanh(t))
    q_all = qk[:, 0:M_QK].astype(BF16)
    k_all = qk[:, M_QK:2 * M_QK] * (M_QK_DIM ** -0.5)

    zt = gt_ref[...] + gbc_ref[:, 0:1]
    lf_all = _log_sigmoid(zt)
    b_all = _cumsum_lanes(lf_all)

    row = lax.broadcasted_iota(jnp.int32, (tc, tc), 0)
    col = lax.broadcasted_iota(jnp.int32, (tc, tc), 1)
    causal = col <= row
    ones_aug = jnp.ones((tc, LANES), BF16)

    def first(h):
        qcols = slice(h * M_QK_DIM, (h + 1) * M_QK_DIM)
        vcols = slice(h * M_V_DIM, (h + 1) * M_V_DIM)
        f = M_HEADS + h
        q_h = q_all[:, qcols]
        kt = k_all[:, qcols].T
        m_prev = m_s[h][0:1, 0:1]
        c_prev = c_s[h]
        u = zt[h:h + 1, :] - b_all[f:f + 1, :]
        b_tot = b_all[f:f + 1, tc - 1:tc]
        v_aug = jnp.concatenate([pa_ref[:, v_off + h * M_V_DIM:v_off + (h + 1) * M_V_DIM],
                                 ones_aug], axis=1)
        qk = _dot(q_h, kt.astype(BF16))
        qc = _dot(q_h, c_prev.astype(BF16))
        w_log = b_tot + u
        m_loc = jnp.max(w_log, axis=1, keepdims=True)
        wkt = (kt * jnp.exp(w_log - m_loc)).astype(BF16)
        m_new = jnp.maximum(b_tot + m_prev, m_loc)
        decay = jnp.exp(b_tot + m_prev - m_new)
        gain = jnp.exp(m_loc - m_new)
        m_s[h] = jnp.broadcast_to(m_new, (SUBLANES, LANES))
        return (u, m_prev, v_aug, qk, qc), (wkt, v_aug, c_prev, decay, gain)

    def third(h, wkt, v_aug, c_prev, decay, gain):
        c_s[h] = decay * c_prev + gain * _dot(wkt, v_aug)

    def second(h, u, m_prev, v_aug, qk, qc):
        vcols = slice(h * M_V_DIM, (h + 1) * M_V_DIM)
        f = M_HEADS + h
        um = jnp.where(causal, u, NEG)
        g = jnp.maximum(jnp.max(um, axis=1, keepdims=True), m_prev)
        b_col = jnp.sum(jnp.where(causal, lf_all[f:f + 1, :], 0.0), axis=1, keepdims=True)
        p = jnp.exp(um - g) * qk
        na = _dot(p.astype(BF16), v_aug) + jnp.exp(m_prev - g) * qc
        num = na[:, 0:M_V_DIM]
        den = na[:, M_V_DIM:M_V_DIM + 1]
        hh = num / jnp.maximum(jnp.abs(den), jnp.exp(-(b_col + g)))
        o_gate = pa_ref[:, o_off + h * M_V_DIM:o_off + (h + 1) * M_V_DIM].astype(F32)
        hh = hh * _rms_scale(hh) * mn_ref[0:1, vcols] * _sigmoid(o_gate)
        out_ref[:, vcols] = hh.astype(BF16)

    emit(0)
    parts = [first(h) for h in range(M_HEADS)]
    groups = n_parts - 1
    per = M_HEADS // groups
    for grp in range(groups):
        emit(1 + grp)
        for h in range(grp * per, (grp + 1) * per):
            second(h, *parts[h][0])
    for h in range(M_HEADS):
        third(h, *parts[h][1])


def _mlstm_init(step, scratch_refs):
    qk_buf, c_s, m_s = scratch_refs

    @pl.when(step == 0)
    def _():
        qk_buf[0:SUBLANES, :] = jnp.zeros((SUBLANES, 2 * M_QK), F32)
        c_s[...] = jnp.zeros(c_s.shape, F32)
        m_s[...] = jnp.full(m_s.shape, NEG, F32)


def _mlstm_side(pa, gt, cw, cb, gbc, mn):
    s = pa.shape[0]
    tc = MLSTM_BLOCK
    const = lambda i: (0, 0)
    return _Side(
        body=_mlstm_body,
        init=_mlstm_init,
        args=(pa, gt, cw, cb, gbc, mn),
        in_specs=(
            _Spec((tc, pa.shape[1]), lambda i: (i, 0)),
            _Spec((SUBLANES, tc), lambda i: (0, i)),
            _Spec((M_CONV, 2 * M_QK), const),
            _Spec((1, 2 * M_QK), const),
            _Spec((SUBLANES, LANES), const),
            _Spec((1, M_V), const),
        ),
        out_specs=(_Spec((tc, M_V), lambda i: (i, 0)),),
        out_shapes=(jax.ShapeDtypeStruct((s, M_V), BF16),),
        scratch=(
            pltpu.VMEM((SUBLANES + tc, 2 * M_QK), F32),
            pltpu.VMEM((M_HEADS, M_QK_DIM, M_V_DIM + LANES), F32),
            pltpu.VMEM((M_HEADS, SUBLANES, LANES), F32),
        ),
        steps=s // tc,
        places=3,
    )


ATT_G = 4
ATT_ROWS = ATT_G * CHUNK
ATT_KBLOCKS = (A_PAST_CHUNKS + ATT_G) // ATT_G


def _attn_body(step, in_refs, out_refs, scratch_refs, emit, n_parts):
    del step, scratch_refs
    q_ref, k_ref, v_ref, bias_ref = in_refs
    (out_ref,) = out_refs
    scale2 = (A_HEAD_DIM ** -0.5) * LOG2E
    ones = jnp.ones((ATT_KBLOCKS * ATT_ROWS, A_HEAD_DIM), BF16)

    def scores(h):
        cols = slice(h * A_HEAD_DIM, (h + 1) * A_HEAD_DIM)
        return _dot_nt(q_ref[:, cols], k_ref[:, cols])

    def finish(h, s):
        cols = slice(h * A_HEAD_DIM, (h + 1) * A_HEAD_DIM)
        s = s * scale2 + bias_ref[0, h]
        p = jnp.exp2(s - jnp.max(s, axis=1, keepdims=True)).astype(BF16)
        o = _dot(p, jnp.concatenate([v_ref[:, cols], ones], axis=1))
        out_ref[:, cols] = (o[:, 0:A_HEAD_DIM] / o[:, A_HEAD_DIM:A_HEAD_DIM + 1]).astype(BF16)

    per = A_HEADS // n_parts
    s_next = [scores(h) for h in range(per)]
    for grp in range(n_parts):
        s_grp = s_next
        if grp + 1 < n_parts:
            s_next = [scores(h) for h in range((grp + 1) * per, (grp + 2) * per)]
        emit(grp)
        for h, s in zip(range(grp * per, (grp + 1) * per), s_grp):
            finish(h, s)


def _attn_side(pb, bias):
    s = pb.shape[0]
    last = ATT_KBLOCKS - 1

    def kv(group):
        return _Spec((pl.Element(ATT_KBLOCKS * ATT_ROWS), pl.Element(A_W)),
                     lambda g: (jnp.maximum(g - last, 0) * ATT_ROWS, group * A_W))

    return _Side(
        body=_attn_body,
        init=None,
        args=(pb,) * 3 + (bias,),
        in_specs=(
            _Spec((ATT_ROWS, A_W), lambda g: (g, 0)),
            kv(1), kv(2),
            _Spec((1, A_HEADS, ATT_ROWS, ATT_KBLOCKS * ATT_ROWS),
                  lambda g: (jnp.minimum(g, last), 0, 0, 0), pl.Buffered(1)),
        ),
        out_specs=(_Spec((ATT_ROWS, A_W), lambda g: (g, 0)),),
        out_shapes=(jax.ShapeDtypeStruct((s, A_W), BF16),),
        scratch=(),
        steps=s // ATT_ROWS,
        places=4,
    )


def _band_bias(rel_table):
    hds = rel_table.shape[0]
    wcols = ATT_KBLOCKS * ATT_ROWS
    band_w = (A_PAST_CHUNKS + 1) * CHUNK
    far = A_PAST_CHUNKS * CHUNK - A_MAX_REL + CHUNK
    tab = rel_table.astype(F32) * LOG2E
    e = jnp.concatenate([jnp.broadcast_to(tab[:, 2 * A_MAX_REL:], (hds, far)),
                         tab[:, A_MAX_REL - CHUNK + 1:2 * A_MAX_REL][:, ::-1],
                         jnp.zeros((hds, 1), F32)], axis=1)
    period = e.shape[1]
    tiled = jnp.broadcast_to(e[:, None, :], (hds, CHUNK, period)).reshape(hds, -1)
    skew = tiled[:, :CHUNK * (period - 1)].reshape(hds, CHUNK, period - 1)
    chunk_bias = skew[:, :, CHUNK - 1:CHUNK - 1 + band_w]
    rows = [jnp.pad(chunk_bias, ((0, 0), (0, 0), (ci * CHUNK, wcols - band_w - ci * CHUNK)),
                    constant_values=NEG) for ci in range(ATT_G)]
    base = jnp.concatenate(rows, axis=1)
    variants = []
    for v in range(ATT_KBLOCKS):
        lead = (ATT_KBLOCKS - 1 - v) * ATT_ROWS
        variants.append(jnp.pad(base[:, :, lead:], ((0, 0), (0, 0), (0, lead)), constant_values=NEG))
    return jnp.stack(variants)


def _merge_kernel(hm_ref, ha_ref, gm_ref, ga_ref, x_ref, wm_ref, wa_ref, wo_ref, gb_ref, nf_ref,
                  x1_ref, h2_ref):
    ym = _dot(hm_ref[...], wm_ref[...])
    ya = _dot(ha_ref[...], wa_ref[...])
    merged = (_sigmoid(gm_ref[...].astype(F32) + gb_ref[0:1, :]) * ym
              + _sigmoid(ga_ref[...].astype(F32) + gb_ref[1:2, :]) * ya)
    x1 = x_ref[...] + _dot(merged.astype(BF16), wo_ref[...])
    x1_ref[...] = x1
    h2_ref[...] = (x1 * _rms_scale(x1) * nf_ref[...]).astype(BF16)


def _merge(hm, ha, pc, x, wm, wa, wo, gb, nf, tm=256):
    s, d = x.shape
    const = lambda i: (0, 0)
    resident = pl.Buffered(1)
    return pl.pallas_call(
        _merge_kernel,
        grid=(s // tm,),
        in_specs=[
            pl.BlockSpec((tm, M_V), lambda i: (i, 0)),
            pl.BlockSpec((tm, A_W), lambda i: (i, 0)),
            pl.BlockSpec((tm, d), lambda i: (i, 0)),
            pl.BlockSpec((tm, d), lambda i: (i, 1)),
            pl.BlockSpec((tm, d), lambda i: (i, 0)),
            pl.BlockSpec((M_V, d), const, pipeline_mode=resident),
            pl.BlockSpec((A_W, d), const, pipeline_mode=resident),
            pl.BlockSpec((d, d), const, pipeline_mode=resident),
            pl.BlockSpec((2, d), const),
            pl.BlockSpec((1, d), const),
        ],
        out_specs=[
            pl.BlockSpec((tm, d), lambda i: (i, 0)),
            pl.BlockSpec((tm, d), lambda i: (i, 0)),
        ],
        out_shape=[
            jax.ShapeDtypeStruct((s, d), F32),
            jax.ShapeDtypeStruct((s, d), BF16),
        ],
        compiler_params=_params(("arbitrary",)),
        name="merge",
    )(hm, ha, pc, pc, x, wm, wa, wo, gb, nf)


FFN_SUB_ROWS = 512


def _ffn_up_kernel(h_ref, wg_ref, wv_ref, cw_ref, cb_ref, wd_ref, a_ref, wd_bf_ref, w_bf, ug_buf,
                   *, tm, tf):
    halo = SUBLANES
    rs = FFN_SUB_ROWS
    wd_bf_ref[...] = wd_ref[...].astype(BF16)

    @pl.when(pl.program_id(1) == 0)
    def _():
        w_bf[:, 0:tf] = wg_ref[...].astype(BF16)
        w_bf[:, tf:2 * tf] = wv_ref[...].astype(BF16)
        ug_buf[0:halo, :] = jnp.zeros((halo, tf), F32)

    for r in range(tm // rs):
        h = h_ref[r * rs:(r + 1) * rs, :]
        u = _dot(h, w_bf[...])
        ug = u[:, 0:tf]
        uv = u[:, tf:2 * tf]
        base = halo + r * rs
        ug_buf[base:base + rs, :] = ug
        conv = cb_ref[...] + ug * cw_ref[FFN_CONV - 1:FFN_CONV, :]
        for tap in range(FFN_CONV - 1):
            lo = base - (FFN_CONV - 1) + tap
            conv = conv + ug_buf[lo:lo + rs, :] * cw_ref[tap:tap + 1, :]
        a_ref[r * rs:(r + 1) * rs, :] = (conv * _sigmoid(conv) * uv).astype(BF16)
    ug_buf[0:halo, :] = ug_buf[tm:tm + halo, :]


def _ffn_up(h2, w_up, cw, cb, w_down, tm=2048, tf=512):
    s, d = h2.shape
    nf = D_FF // tf
    n_m = s // tm
    wd_rows = D_FF // (nf * n_m)
    assert wd_rows * nf * n_m == D_FF and wd_rows % (2 * SUBLANES) == 0
    wd_spec = pl.BlockSpec((wd_rows, d), lambda j, m: (j * n_m + m, 0))
    return pl.pallas_call(
        functools.partial(_ffn_up_kernel, tm=tm, tf=tf),
        grid=(nf, n_m),
        in_specs=[
            pl.BlockSpec((tm, d), lambda j, m: (m, 0)),
            pl.BlockSpec((d, tf), lambda j, m: (0, j)),
            pl.BlockSpec((d, tf), lambda j, m: (0, nf + j)),
            pl.BlockSpec((FFN_CONV, tf), lambda j, m: (0, j)),
            pl.BlockSpec((1, tf), lambda j, m: (0, j)),
            wd_spec,
        ],
        out_specs=[pl.BlockSpec((tm, tf), lambda j, m: (m, j)), wd_spec],
        out_shape=[jax.ShapeDtypeStruct((s, D_FF), BF16),
                   jax.ShapeDtypeStruct((D_FF, d), BF16)],
        scratch_shapes=[
            pltpu.VMEM((d, 2 * tf), BF16),
            pltpu.VMEM((SUBLANES + tm, tf), F32),
        ],
        compiler_params=_params(("arbitrary", "arbitrary")),
        name="ffn_up",
    )(h2, w_up, w_up, cw, cb, w_down)


FFN_DOWN_COLS = 512


def _ffn_down_kernel(a_ref, w_ref, x1_ref, nf_ref, o_ref):
    d = o_ref.shape[1]
    ssq = jnp.zeros((o_ref.shape[0], 1), F32)
    for n in range(d // FFN_DOWN_COLS):
        cols = slice(n * FFN_DOWN_COLS, (n + 1) * FFN_DOWN_COLS)
        y = x1_ref[:, cols] + _dot(a_ref[...], w_ref[:, cols])
        o_ref[:, cols] = y
        ssq = ssq + jnp.sum(y * y, axis=-1, keepdims=True)
    o_ref[...] = o_ref[...] * lax.rsqrt(ssq * (1.0 / d) + EPS) * nf_ref[...]


def _ffn_down(a, w_down, x1, nf, tm=512):
    s, d = x1.shape
    dff = a.shape[1]
    return pl.pallas_call(
        _ffn_down_kernel,
        grid=(s // tm,),
        in_specs=[
            pl.BlockSpec((tm, dff), lambda i: (i, 0)),
            pl.BlockSpec((dff, d), lambda i: (0, 0), pipeline_mode=pl.Buffered(1)),
            pl.BlockSpec((tm, d), lambda i: (i, 0)),
            pl.BlockSpec((1, d), lambda i: (0, 0)),
        ],
        out_specs=pl.BlockSpec((tm, d), lambda i: (i, 0)),
        out_shape=jax.ShapeDtypeStruct((s, d), F32),
        compiler_params=_params(("arbitrary",)),
        name="ffn_down",
    )(a, w_down, x1, nf)


def kernel(x, norm_mix, w_in, conv_qk_w, conv_qk_b, b_igate, b_fgate, m_norm, rel_bias, gate_bias,
           w_branch_m, w_branch_a, w_out, norm_ffn, w_up, conv_ffn_w, conv_ffn_b, w_down, norm_final):
    batch, seq, d = x.shape
    depth = w_in.shape[0]
    assert (batch, seq, d, depth) == (1, SEQ, D_MODEL, 1)
    xs = x[0]
    l = 0

    w_in_t = jnp.swapaxes(w_in[l], 0, 1)
    w_g = jnp.pad(w_in_t[GATE_COL0:GATE_COL0 + N_GATES, :].T, ((0, 0), (0, LANES - N_GATES)))
    gate_b = jnp.concatenate([b_igate[l], b_fgate[l]])
    gb_col = jnp.broadcast_to(gate_b[:, None], (SUBLANES, LANES))
    bias = _band_bias(rel_bias[l])

    h, gt = _prenorm(xs, norm_mix[l][None, :], w_g.astype(BF16))
    na = 2 * M_QK + 2 * M_V
    attn0 = GATE_COL0 + N_GATES
    pa, w_m_bf, w_a_bf = _proj(h, w_in_t, 0, na, BF16, "proj_mlstm",
                               sides=[_cast_side(w_branch_m[l], 128), _cast_side(w_branch_a[l], 128)])
    mlstm = _mlstm_side(pa, gt, conv_qk_w[l], conv_qk_b[l][None, :], gb_col, m_norm[l][None, :])
    pb, hm, w_out_bf = _proj(h, w_in_t, attn0, 3 * A_W, BF16, "proj_attn_mlstm", tn=768,
                             sides=[mlstm, _cast_side(w_out[l], 128)])
    pc, ha = _proj(h, w_in_t, attn0 + 3 * A_W, 2 * D_MODEL, BF16, "proj_gates_attn",
                   sides=[_attn_side(pb, bias)])

    x1, h2 = _merge(hm, ha, pc, xs, w_m_bf, w_a_bf, w_out_bf, gate_bias[l], norm_ffn[l][None, :])
    a, w_down_bf = _ffn_up(h2, w_up[l], conv_ffn_w[l], conv_ffn_b[l][None, :], w_down[l])
    out = _ffn_down(a, w_down_bf, x1, norm_final[None, :])
    return out[None]
```

```python
import functools
import math
from typing import Callable, NamedTuple

import jax
import jax.numpy as jnp
from jax import lax
from jax.experimental import pallas as pl
from jax.experimental.pallas import tpu as pltpu

D_MODEL = 2048
SEQ = 8192
CHUNK = 64
M_HEADS = 4
M_QK_DIM = 128
M_V_DIM = 256
M_CONV = 4
A_HEADS = 8
A_HEAD_DIM = 128
A_PAST_CHUNKS = 8
A_MAX_REL = 128
D_FF = 5632
FFN_CONV = 3
EPS = 1e-6
NEG = -1e30
LOG2E = math.log2(math.e)

M_QK = M_HEADS * M_QK_DIM
M_V = M_HEADS * M_V_DIM
A_W = A_HEADS * A_HEAD_DIM
GATE_COL0 = 2 * M_QK + 2 * M_V
N_GATES = 2 * M_HEADS

LANES = 128
SUBLANES = 8
VMEM_LIMIT = 56 * 1024 * 1024

BF16 = jnp.bfloat16
F32 = jnp.float32


def _params(sem, vmem=VMEM_LIMIT, flags=None):
    return pltpu.CompilerParams(dimension_semantics=sem, vmem_limit_bytes=vmem, flags=flags)


def _dot(a, b):
    return jnp.dot(a, b, preferred_element_type=F32)


def _dot_nt(a, b):
    return lax.dot_general(a, b, (((1,), (1,)), ((), ())), preferred_element_type=F32)


def _sigmoid(z):
    return 0.5 * jnp.tanh(0.5 * z) + 0.5


def _log_sigmoid(z):
    return jnp.minimum(z, 0.0) - jnp.log(1.0 + jnp.exp(-jnp.abs(z)))


def _rms_scale(y):
    return lax.rsqrt(jnp.mean(y * y, axis=-1, keepdims=True) + EPS)


def _prenorm_kernel(x_ref, g_ref, wg_ref, h_ref, gt_ref):
    xf = x_ref[...]
    hb = (xf * _rms_scale(xf) * g_ref[...]).astype(BF16)
    h_ref[...] = hb
    gcol = _dot(hb, wg_ref[...])
    gt_ref[...] = gcol.T[0:SUBLANES, :]


def _prenorm(x, g, wg, tm=1024):
    s, d = x.shape
    return pl.pallas_call(
        _prenorm_kernel,
        grid=(s // tm,),
        in_specs=[
            pl.BlockSpec((tm, d), lambda i: (i, 0)),
            pl.BlockSpec((1, d), lambda i: (0, 0)),
            pl.BlockSpec((d, LANES), lambda i: (0, 0)),
        ],
        out_specs=[
            pl.BlockSpec((tm, d), lambda i: (i, 0)),
            pl.BlockSpec((SUBLANES, tm), lambda i: (0, i)),
        ],
        out_shape=[
            jax.ShapeDtypeStruct((s, d), BF16),
            jax.ShapeDtypeStruct((SUBLANES, s), F32),
        ],
        compiler_params=_params(("arbitrary",)),
        name="prenorm",
    )(x, g, wg)


class _Spec(NamedTuple):
    block: tuple
    index: Callable
    mode: object = None


class _Side(NamedTuple):
    body: Callable
    init: Callable
    args: tuple
    in_specs: tuple
    out_specs: tuple
    out_shapes: tuple
    scratch: tuple
    steps: int
    clamped: bool = False
    places: tuple = ()


PROJ_SLICE = 256


def _interleave(i, n_units, emit, n_parts):
    for k in range(i * n_parts // n_units, (i + 1) * n_parts // n_units):
        emit(k)


def _proj_kernel(*refs, shift, tn, sides):
    n_w = 3 if shift else 2
    h_ref, wm_ref = refs[0], refs[1]
    pos = n_w
    side_in = []
    for sd in sides:
        side_in.append(refs[pos:pos + len(sd.args)])
        pos += len(sd.args)
    o_ref = refs[pos]
    pos += 1
    side_out = []
    for sd in sides:
        side_out.append(refs[pos:pos + len(sd.out_specs)])
        pos += len(sd.out_specs)
    w_bf = refs[pos]
    pos += 1
    side_scratch = []
    for sd in sides:
        side_scratch.append(refs[pos:pos + len(sd.scratch)])
        pos += len(sd.scratch)

    @pl.when(pl.program_id(1) == 0)
    def _():
        if shift:
            w_bf[0:tn - shift, :] = wm_ref[shift:tn, :].astype(BF16)
            w_bf[tn - shift:tn, :] = refs[2][...].astype(BF16)
        else:
            w_bf[...] = wm_ref[...].astype(BF16)

    step = pl.program_id(0) * pl.num_programs(1) + pl.program_id(1)
    for sd, scr in zip(sides, side_scratch):
        if sd.init:
            sd.init(step, scr)

    placers = [idx for idx, sd in enumerate(sides) if sd.places]
    assert len(placers) <= 1
    widths = sides[placers[0]].places if placers else (tn,)
    n_parts = len(widths)
    assert sum(widths) == tn and all(w % PROJ_SLICE == 0 for w in widths)

    def emit(k):
        lo = sum(widths[:k])
        cols = slice(lo, lo + widths[k])
        o_ref[:, cols] = _dot_nt(h_ref[...], w_bf[cols, :]).astype(o_ref.dtype)

    for idx, sd in enumerate(sides):
        if not sd.places:
            sd.body(step, side_in[idx], side_out[idx], side_scratch[idx], None, 0)
    if placers:
        idx = placers[0]
        sides[idx].body(step, side_in[idx], side_out[idx], side_scratch[idx], emit, n_parts)
    else:
        emit(0)


def _proj(h, wt, col0, ncols, out_dtype, name, tm=1024, tn=1024, sides=()):
    s, d = h.shape
    shift = col0 % tn
    base = col0 - shift
    n_m = s // tm
    n_steps = (ncols // tn) * n_m
    assert shift % SUBLANES == 0 and ncols % tn == 0
    assert all(sd.steps == n_steps or (sd.clamped and sd.steps < n_steps) for sd in sides)

    def flat(spec):
        return pl.BlockSpec(spec.block, lambda j, m: spec.index(j * n_m + m),
                            pipeline_mode=spec.mode)

    in_specs = [
        pl.BlockSpec((tm, d), lambda j, m: (m, 0)),
        pl.BlockSpec((tn, d), lambda j, m: (base // tn + j, 0)),
    ]
    args = [h, wt]
    if shift:
        in_specs.append(pl.BlockSpec((shift, d), lambda j, m: ((base + (j + 1) * tn) // shift, 0)))
        args.append(wt)
    out_specs = [pl.BlockSpec((tm, tn), lambda j, m: (m, j))]
    out_shapes = [jax.ShapeDtypeStruct((s, ncols), out_dtype)]
    scratch = [pltpu.VMEM((tn, d), BF16)]
    for sd in sides:
        in_specs += [flat(sp) for sp in sd.in_specs]
        args += list(sd.args)
    for sd in sides:
        out_specs += [flat(sp) for sp in sd.out_specs]
        out_shapes += list(sd.out_shapes)
    for sd in sides:
        scratch += list(sd.scratch)
    return pl.pallas_call(
        functools.partial(_proj_kernel, shift=shift, tn=tn, sides=tuple(sides)),
        grid=(ncols // tn, n_m),
        in_specs=in_specs,
        out_specs=out_specs,
        out_shape=out_shapes,
        scratch_shapes=scratch,
        compiler_params=_params(("arbitrary", "arbitrary")),
        name=name,
    )(*args)


def _cast_body(step, in_refs, out_refs, scratch_refs, emit, n_parts):
    del step, scratch_refs, emit, n_parts
    out_refs[0][...] = in_refs[0][...].astype(BF16)


def _cast_side(w, rows):
    n, d = w.shape
    nb = n // rows
    spec = _Spec((rows, d), lambda i: (jnp.minimum(i, nb - 1), 0))
    return _Side(body=_cast_body, init=None, args=(w,), in_specs=(spec,), out_specs=(spec,),
                 out_shapes=(jax.ShapeDtypeStruct((n, d), BF16),), scratch=(), steps=nb,
                 clamped=True)


MLSTM_BLOCK = 256


def _cumsum_lanes(v):
    n = v.shape[-1]
    pos = lax.broadcasted_iota(jnp.int32, v.shape, v.ndim - 1)
    shift = 1
    while shift < n:
        v = v + jnp.where(pos >= shift, pltpu.roll(v, shift, v.ndim - 1), 0.0)
        shift *= 2
    return v


def _mlstm_body(step, in_refs, out_refs, scratch_refs, emit, n_parts):
    pa_ref, gt_ref, cw_ref, cb_ref, gbc_ref, mn_ref = in_refs
    v_off = 2 * M_QK
    o_off = 2 * M_QK + M_V
    (out_ref,) = out_refs
    qk_buf, c_s, m_s = scratch_refs
    del step
    halo = SUBLANES
    tc = MLSTM_BLOCK

    qk_buf[halo:halo + tc, :] = pa_ref[:, 0:2 * M_QK].astype(F32)
    cw_half = cw_ref[...] * 0.5
    t = jnp.broadcast_to(cb_ref[...] * 0.5, (tc, 2 * M_QK))
    for tap in range(M_CONV):
        lo = halo - (M_CONV - 1) + tap
        t = t + qk_buf[lo:lo + tc, :] * cw_half[tap:tap + 1, :]
    qk_buf[0:halo, :] = qk_buf[tc:tc + halo, :]
    qk = t * (1.0 + jnp.tanh(t))
    q_all = qk[:, 0:M_QK].astype(BF16)
    k_all = qk[:, M_QK:2 * M_QK] * (M_QK_DIM ** -0.5)

    zt = gt_ref[...] + gbc_ref[:, 0:1]
    lf_all = _log_sigmoid(zt)
    b_all = _cumsum_lanes(lf_all)

    row = lax.broadcasted_iota(jnp.int32, (tc, tc), 0)
    col = lax.broadcasted_iota(jnp.int32, (tc, tc), 1)
    causal = col <= row
    ones_aug = jnp.ones((tc, LANES), BF16)

    def first(h):
        qcols = slice(h * M_QK_DIM, (h + 1) * M_QK_DIM)
        vcols = slice(h * M_V_DIM, (h + 1) * M_V_DIM)
        f = M_HEADS + h
        q_h = q_all[:, qcols]
        kt = k_all[:, qcols].T
        m_prev = m_s[h][0:1, 0:1]
        c_prev = c_s[h]
        u = zt[h:h + 1, :] - b_all[f:f + 1, :]
        b_tot = b_all[f:f + 1, tc - 1:tc]
        v_aug = jnp.concatenate([pa_ref[:, v_off + h * M_V_DIM:v_off + (h + 1) * M_V_DIM],
                                 ones_aug], axis=1)
        qk = _dot(q_h, kt.astype(BF16))
        qc = _dot(q_h, c_prev.astype(BF16))
        w_log = b_tot + u
        m_loc = jnp.max(w_log, axis=1, keepdims=True)
        wkt = (kt * jnp.exp(w_log - m_loc)).astype(BF16)
        m_new = jnp.maximum(b_tot + m_prev, m_loc)
        decay = jnp.exp(b_tot + m_prev - m_new)
        gain = jnp.exp(m_loc - m_new)
        m_s[h] = jnp.broadcast_to(m_new, (SUBLANES, LANES))
        return (u, m_prev, v_aug, qk, qc), (wkt, v_aug, c_prev, decay, gain)

    def third(h, wkt, v_aug, c_prev, decay, gain):
        c_s[h] = decay * c_prev + gain * _dot(wkt, v_aug)

    def second(h, u, m_prev, v_aug, qk, qc):
        vcols = slice(h * M_V_DIM, (h + 1) * M_V_DIM)
        f = M_HEADS + h
        um = jnp.where(causal, u, NEG)
        g = jnp.maximum(jnp.max(um, axis=1, keepdims=True), m_prev)
        b_col = jnp.sum(jnp.where(causal, lf_all[f:f + 1, :], 0.0), axis=1, keepdims=True)
        p = jnp.exp(um - g) * qk
        na = _dot(p.astype(BF16), v_aug) + jnp.exp(m_prev - g) * qc
        num = na[:, 0:M_V_DIM]
        den = na[:, M_V_DIM:M_V_DIM + 1]
        hh = num / jnp.maximum(jnp.abs(den), jnp.exp(-(b_col + g)))
        o_gate = pa_ref[:, o_off + h * M_V_DIM:o_off + (h + 1) * M_V_DIM].astype(F32)
        hh = hh * _rms_scale(hh) * mn_ref[0:1, vcols] * _sigmoid(o_gate)
        out_ref[:, vcols] = hh.astype(BF16)

    emit(0)
    parts = [first(h) for h in range(M_HEADS)]
    groups = n_parts - 1
    per = M_HEADS // groups
    for grp in range(groups):
        emit(1 + grp)
        for h in range(grp * per, (grp + 1) * per):
            second(h, *parts[h][0])
    for h in range(M_HEADS):
        third(h, *parts[h][1])
    assert groups * per == M_HEADS


def _mlstm_init(step, scratch_refs):
    qk_buf, c_s, m_s = scratch_refs

    @pl.when(step == 0)
    def _():
        qk_buf[0:SUBLANES, :] = jnp.zeros((SUBLANES, 2 * M_QK), F32)
        c_s[...] = jnp.zeros(c_s.shape, F32)
        m_s[...] = jnp.full(m_s.shape, NEG, F32)


def _mlstm_side(pa, gt, cw, cb, gbc, mn):
    s = pa.shape[0]
    tc = MLSTM_BLOCK
    const = lambda i: (0, 0)
    return _Side(
        body=_mlstm_body,
        init=_mlstm_init,
        args=(pa, gt, cw, cb, gbc, mn),
        in_specs=(
            _Spec((tc, pa.shape[1]), lambda i: (i, 0)),
            _Spec((SUBLANES, tc), lambda i: (0, i)),
            _Spec((M_CONV, 2 * M_QK), const),
            _Spec((1, 2 * M_QK), const),
            _Spec((SUBLANES, LANES), const),
            _Spec((1, M_V), const),
        ),
        out_specs=(_Spec((tc, M_V), lambda i: (i, 0)),),
        out_shapes=(jax.ShapeDtypeStruct((s, M_V), BF16),),
        scratch=(
            pltpu.VMEM((SUBLANES + tc, 2 * M_QK), F32),
            pltpu.VMEM((M_HEADS, M_QK_DIM, M_V_DIM + LANES), F32),
            pltpu.VMEM((M_HEADS, SUBLANES, LANES), F32),
        ),
        steps=s // tc,
        places=(256, 256, 256),
    )


ATT_G = 4
ATT_ROWS = ATT_G * CHUNK
ATT_KBLOCKS = (A_PAST_CHUNKS + ATT_G) // ATT_G


def _attn_body(step, in_refs, out_refs, scratch_refs, emit, n_parts):
    del step, scratch_refs
    q_ref, k_ref, v_ref, bias_ref = in_refs
    (out_ref,) = out_refs
    scale2 = (A_HEAD_DIM ** -0.5) * LOG2E
    ones = jnp.ones((ATT_KBLOCKS * ATT_ROWS, A_HEAD_DIM), BF16)

    def scores(h):
        cols = slice(h * A_HEAD_DIM, (h + 1) * A_HEAD_DIM)
        return _dot_nt(q_ref[:, cols], k_ref[:, cols])

    def finish(h, s):
        cols = slice(h * A_HEAD_DIM, (h + 1) * A_HEAD_DIM)
        s = s * scale2 + bias_ref[0, h]
        p = jnp.exp2(s - jnp.max(s, axis=1, keepdims=True)).astype(BF16)
        o = _dot(p, jnp.concatenate([v_ref[:, cols], ones], axis=1))
        out_ref[:, cols] = (o[:, 0:A_HEAD_DIM] / o[:, A_HEAD_DIM:A_HEAD_DIM + 1]).astype(BF16)

    per = A_HEADS // n_parts
    s_next = [scores(h) for h in range(per)]
    for grp in range(n_parts):
        s_grp = s_next
        if grp + 1 < n_parts:
            s_next = [scores(h) for h in range((grp + 1) * per, (grp + 2) * per)]
        emit(grp)
        for h, s in zip(range(grp * per, (grp + 1) * per), s_grp):
            finish(h, s)


def _attn_side(pb, bias):
    s = pb.shape[0]
    last = ATT_KBLOCKS - 1

    def kv(group):
        return _Spec((pl.Element(ATT_KBLOCKS * ATT_ROWS), pl.Element(A_W)),
                     lambda g: (jnp.maximum(g - last, 0) * ATT_ROWS, group * A_W))

    return _Side(
        body=_attn_body,
        init=None,
        args=(pb,) * 3 + (bias,),
        in_specs=(
            _Spec((ATT_ROWS, A_W), lambda g: (g, 0)),
            kv(1), kv(2),
            _Spec((1, A_HEADS, ATT_ROWS, ATT_KBLOCKS * ATT_ROWS),
                  lambda g: (jnp.minimum(g, last), 0, 0, 0)),
        ),
        out_specs=(_Spec((ATT_ROWS, A_W), lambda g: (g, 0)),),
        out_shapes=(jax.ShapeDtypeStruct((s, A_W), BF16),),
        scratch=(),
        steps=s // ATT_ROWS,
        places=(256, 256, 256, 256),
    )


def _band_bias(rel_table):
    hds = rel_table.shape[0]
    wcols = ATT_KBLOCKS * ATT_ROWS
    band_w = (A_PAST_CHUNKS + 1) * CHUNK
    far = A_PAST_CHUNKS * CHUNK - A_MAX_REL + CHUNK
    tab = rel_table.astype(F32) * LOG2E
    e = jnp.concatenate([jnp.broadcast_to(tab[:, 2 * A_MAX_REL:], (hds, far)),
                         tab[:, A_MAX_REL - CHUNK + 1:2 * A_MAX_REL][:, ::-1],
                         jnp.zeros((hds, 1), F32)], axis=1)
    period = e.shape[1]
    tiled = jnp.broadcast_to(e[:, None, :], (hds, CHUNK, period)).reshape(hds, -1)
    skew = tiled[:, :CHUNK * (period - 1)].reshape(hds, CHUNK, period - 1)
    chunk_bias = skew[:, :, CHUNK - 1:CHUNK - 1 + band_w]
    rows = [jnp.pad(chunk_bias, ((0, 0), (0, 0), (ci * CHUNK, wcols - band_w - ci * CHUNK)),
                    constant_values=NEG) for ci in range(ATT_G)]
    base = jnp.concatenate(rows, axis=1)
    variants = []
    for v in range(ATT_KBLOCKS):
        lead = (ATT_KBLOCKS - 1 - v) * ATT_ROWS
        variants.append(jnp.pad(base[:, :, lead:], ((0, 0), (0, 0), (0, lead)), constant_values=NEG))
    return jnp.stack(variants)


def _merge_kernel(hm_ref, ha_ref, gm_ref, ga_ref, x_ref, wm_ref, wa_ref, wo_ref, gb_ref, nf_ref,
                  x1_ref, h2_ref):
    ym = _dot(hm_ref[...], wm_ref[...])
    ya = _dot(ha_ref[...], wa_ref[...])
    merged = (_sigmoid(gm_ref[...].astype(F32) + gb_ref[0:1, :]) * ym
              + _sigmoid(ga_ref[...].astype(F32) + gb_ref[1:2, :]) * ya)
    x1 = x_ref[...] + _dot(merged.astype(BF16), wo_ref[...])
    x1_ref[...] = x1
    h2_ref[...] = (x1 * _rms_scale(x1) * nf_ref[...]).astype(BF16)


def _merge(hm, ha, pc, x, wm, wa, wo, gb, nf, tm=256):
    s, d = x.shape
    const = lambda i: (0, 0)
    resident = pl.Buffered(1)
    return pl.pallas_call(
        _merge_kernel,
        grid=(s // tm,),
        in_specs=[
            pl.BlockSpec((tm, M_V), lambda i: (i, 0)),
            pl.BlockSpec((tm, A_W), lambda i: (i, 0)),
            pl.BlockSpec((tm, d), lambda i: (i, 0)),
            pl.BlockSpec((tm, d), lambda i: (i, 1)),
            pl.BlockSpec((tm, d), lambda i: (i, 0)),
            pl.BlockSpec((M_V, d), const, pipeline_mode=resident),
            pl.BlockSpec((A_W, d), const, pipeline_mode=resident),
            pl.BlockSpec((d, d), const, pipeline_mode=resident),
            pl.BlockSpec((2, d), const),
            pl.BlockSpec((1, d), const),
        ],
        out_specs=[
            pl.BlockSpec((tm, d), lambda i: (i, 0)),
            pl.BlockSpec((tm, d), lambda i: (i, 0)),
        ],
        out_shape=[
            jax.ShapeDtypeStruct((s, d), F32),
            jax.ShapeDtypeStruct((s, d), BF16),
        ],
        compiler_params=_params(("arbitrary",)),
        name="merge",
    )(hm, ha, pc, pc, x, wm, wa, wo, gb, nf)


FFN_SUB_ROWS = 512


def _ffn_up_kernel(h_ref, wg_ref, wv_ref, cw_ref, cb_ref, wd_ref, a_ref, wd_bf_ref, w_bf, ug_buf,
                   *, tm, tf):
    halo = SUBLANES
    rs = FFN_SUB_ROWS
    wd_bf_ref[...] = wd_ref[...].astype(BF16)

    @pl.when(pl.program_id(1) == 0)
    def _():
        w_bf[:, 0:tf] = wg_ref[...].astype(BF16)
        w_bf[:, tf:2 * tf] = wv_ref[...].astype(BF16)
        ug_buf[0:halo, :] = jnp.zeros((halo, tf), F32)

    for r in range(tm // rs):
        h = h_ref[r * rs:(r + 1) * rs, :]
        u = _dot(h, w_bf[...])
        ug = u[:, 0:tf]
        uv = u[:, tf:2 * tf]
        base = halo + r * rs
        ug_buf[base:base + rs, :] = ug
        conv = cb_ref[...] + ug * cw_ref[FFN_CONV - 1:FFN_CONV, :]
        for tap in range(FFN_CONV - 1):
            lo = base - (FFN_CONV - 1) + tap
            conv = conv + ug_buf[lo:lo + rs, :] * cw_ref[tap:tap + 1, :]
        a_ref[r * rs:(r + 1) * rs, :] = (conv * _sigmoid(conv) * uv).astype(BF16)
    ug_buf[0:halo, :] = ug_buf[tm:tm + halo, :]


def _ffn_up(h2, w_up, cw, cb, w_down, tm=2048, tf=512):
    s, d = h2.shape
    nf = D_FF // tf
    n_m = s // tm
    wd_rows = D_FF // (nf * n_m)
    assert wd_rows * nf * n_m == D_FF and wd_rows % (2 * SUBLANES) == 0
    wd_spec = pl.BlockSpec((wd_rows, d), lambda j, m: (j * n_m + m, 0))
    return pl.pallas_call(
        functools.partial(_ffn_up_kernel, tm=tm, tf=tf),
        grid=(nf, n_m),
        in_specs=[
            pl.BlockSpec((tm, d), lambda j, m: (m, 0)),
            pl.BlockSpec((d, tf), lambda j, m: (0, j)),
            pl.BlockSpec((d, tf), lambda j, m: (0, nf + j)),
            pl.BlockSpec((FFN_CONV, tf), lambda j, m: (0, j)),
            pl.BlockSpec((1, tf), lambda j, m: (0, j)),
            wd_spec,
        ],
        out_specs=[pl.BlockSpec((tm, tf), lambda j, m: (m, j)), wd_spec],
        out_shape=[jax.ShapeDtypeStruct((s, D_FF), BF16),
                   jax.ShapeDtypeStruct((D_FF, d), BF16)],
        scratch_shapes=[
            pltpu.VMEM((d, 2 * tf), BF16),
            pltpu.VMEM((SUBLANES + tm, tf), F32),
        ],
        compiler_params=_params(("arbitrary", "arbitrary")),
        name="ffn_up",
    )(h2, w_up, w_up, cw, cb, w_down)


FFN_DOWN_COLS = 512


def _ffn_down_kernel(a_ref, w_ref, x1_ref, nf_ref, o_ref):
    d = o_ref.shape[1]
    ssq = jnp.zeros((o_ref.shape[0], 1), F32)
    for n in range(d // FFN_DOWN_COLS):
        cols = slice(n * FFN_DOWN_COLS, (n + 1) * FFN_DOWN_COLS)
        y = x1_ref[:, cols] + _dot(a_ref[...], w_ref[:, cols])
        o_ref[:, cols] = y
        ssq = ssq + jnp.sum(y * y, axis=-1, keepdims=True)
    o_ref[...] = o_ref[...] * lax.rsqrt(ssq * (1.0 / d) + EPS) * nf_ref[...]


def _ffn_down(a, w_down, x1, nf, tm=512):
    s, d = x1.shape
    dff = a.shape[1]
    return pl.pallas_call(
        _ffn_down_kernel,
        grid=(s // tm,),
        in_specs=[
            pl.BlockSpec((tm, dff), lambda i: (i, 0)),
            pl.BlockSpec((dff, d), lambda i: (0, 0), pipeline_mode=pl.Buffered(1)),
            pl.BlockSpec((tm, d), lambda i: (i, 0)),
            pl.BlockSpec((1, d), lambda i: (0, 0)),
        ],
        out_specs=pl.BlockSpec((tm, d), lambda i: (i, 0)),
        out_shape=jax.ShapeDtypeStruct((s, d), F32),
        compiler_params=_params(("arbitrary",)),
        name="ffn_down",
    )(a, w_down, x1, nf)


def kernel(x, norm_mix, w_in, conv_qk_w, conv_qk_b, b_igate, b_fgate, m_norm, rel_bias, gate_bias,
           w_branch_m, w_branch_a, w_out, norm_ffn, w_up, conv_ffn_w, conv_ffn_b, w_down, norm_final):
    batch, seq, d = x.shape
    depth = w_in.shape[0]
    assert (batch, seq, d, depth) == (1, SEQ, D_MODEL, 1)
    xs = x[0]
    l = 0

    w_in_t = jnp.swapaxes(w_in[l], 0, 1)
    w_g = jnp.pad(w_in_t[GATE_COL0:GATE_COL0 + N_GATES, :].T, ((0, 0), (0, LANES - N_GATES)))
    gate_b = jnp.concatenate([b_igate[l], b_fgate[l]])
    gb_col = jnp.broadcast_to(gate_b[:, None], (SUBLANES, LANES))
    bias = _band_bias(rel_bias[l])

    h, gt = _prenorm(xs, norm_mix[l][None, :], w_g.astype(BF16))
    na = 2 * M_QK + 2 * M_V
    attn0 = GATE_COL0 + N_GATES
    pa, w_m_bf, w_a_bf = _proj(h, w_in_t, 0, na, BF16, "proj_mlstm",
                               sides=[_cast_side(w_branch_m[l], 128), _cast_side(w_branch_a[l], 128)])
    mlstm = _mlstm_side(pa, gt, conv_qk_w[l], conv_qk_b[l][None, :], gb_col, m_norm[l][None, :])
    pb, hm, w_out_bf = _proj(h, w_in_t, attn0, 3 * A_W, BF16, "proj_attn_mlstm", tn=768,
                             sides=[mlstm, _cast_side(w_out[l], 128)])
    pc, ha = _proj(h, w_in_t, attn0 + 3 * A_W, 2 * D_MODEL, BF16, "proj_gates_attn",
                   sides=[_attn_side(pb, bias)])

    x1, h2 = _merge(hm, ha, pc, xs, w_m_bf, w_a_bf, w_out_bf, gate_bias[l], norm_ffn[l][None, :])
    a, w_down_bf = _ffn_up(h2, w_up[l], conv_ffn_w[l], conv_ffn_b[l][None, :], w_down[l])
    out = _ffn_down(a, w_down_bf, x1, norm_final[None, :])
    return out[None]
```

```python
import functools
import math
from typing import Callable, NamedTuple

import jax
import jax.numpy as jnp
from jax import lax
from jax.experimental import pallas as pl
from jax.experimental.pallas import tpu as pltpu

D_MODEL = 2048
SEQ = 8192
CHUNK = 64
M_HEADS = 4
M_QK_DIM = 128
M_V_DIM = 256
M_CONV = 4
A_HEADS = 8
A_HEAD_DIM = 128
A_PAST_CHUNKS = 8
A_MAX_REL = 128
D_FF = 5632
FFN_CONV = 3
EPS = 1e-6
NEG = -1e30
LOG2E = math.log2(math.e)

M_QK = M_HEADS * M_QK_DIM
M_V = M_HEADS * M_V_DIM
A_W = A_HEADS * A_HEAD_DIM
GATE_COL0 = 2 * M_QK + 2 * M_V
N_GATES = 2 * M_HEADS

LANES = 128
SUBLANES = 8
MXU_WIDTH = 256
VMEM_LIMIT = 56 * 1024 * 1024

BF16 = jnp.bfloat16
F32 = jnp.float32


def _params(sem):
    return pltpu.CompilerParams(dimension_semantics=sem, vmem_limit_bytes=VMEM_LIMIT)


def _dot(a, b):
    return jnp.dot(a, b, preferred_element_type=F32)


def _dot_nt(a, b):
    return lax.dot_general(a, b, (((1,), (1,)), ((), ())), preferred_element_type=F32)


def _sigmoid(z):
    return 0.5 * jnp.tanh(0.5 * z) + 0.5


def _log_sigmoid(z):
    return jnp.minimum(z, 0.0) - jnp.log(1.0 + jnp.exp(-jnp.abs(z)))


def _rms_scale(y):
    return lax.rsqrt(jnp.mean(y * y, axis=-1, keepdims=True) + EPS)


def _prenorm_kernel(x_ref, g_ref, wg_ref, h_ref, gt_ref):
    xf = x_ref[...]
    hb = (xf * _rms_scale(xf) * g_ref[...]).astype(BF16)
    h_ref[...] = hb
    gcol = _dot(hb, wg_ref[...])
    gt_ref[...] = gcol.T[0:SUBLANES, :]


def _prenorm(x, g, wg, tm=1024):
    s, d = x.shape
    return pl.pallas_call(
        _prenorm_kernel,
        grid=(s // tm,),
        in_specs=[
            pl.BlockSpec((tm, d), lambda i: (i, 0)),
            pl.BlockSpec((1, d), lambda i: (0, 0)),
            pl.BlockSpec((d, LANES), lambda i: (0, 0)),
        ],
        out_specs=[
            pl.BlockSpec((tm, d), lambda i: (i, 0)),
            pl.BlockSpec((SUBLANES, tm), lambda i: (0, i)),
        ],
        out_shape=[
            jax.ShapeDtypeStruct((s, d), BF16),
            jax.ShapeDtypeStruct((SUBLANES, s), F32),
        ],
        compiler_params=_params(("arbitrary",)),
        name="prenorm",
    )(x, g, wg)


class _Spec(NamedTuple):
    block: tuple
    index: Callable
    mode: object = None


class _Side(NamedTuple):
    body: Callable
    init: Callable
    args: tuple
    in_specs: tuple
    out_specs: tuple
    out_shapes: tuple
    scratch: tuple
    steps: int
    clamped: bool = False
    places: tuple = ()


def _proj_kernel(*refs, shift, tn, sides):
    n_w = 3 if shift else 2
    h_ref, wm_ref = refs[0], refs[1]
    pos = n_w
    side_in = []
    for sd in sides:
        side_in.append(refs[pos:pos + len(sd.args)])
        pos += len(sd.args)
    o_ref = refs[pos]
    pos += 1
    side_out = []
    for sd in sides:
        side_out.append(refs[pos:pos + len(sd.out_specs)])
        pos += len(sd.out_specs)
    w_bf = refs[pos]
    pos += 1
    side_scratch = []
    for sd in sides:
        side_scratch.append(refs[pos:pos + len(sd.scratch)])
        pos += len(sd.scratch)

    @pl.when(pl.program_id(1) == 0)
    def _():
        if shift:
            w_bf[0:tn - shift, :] = wm_ref[shift:tn, :].astype(BF16)
            w_bf[tn - shift:tn, :] = refs[2][...].astype(BF16)
        else:
            w_bf[...] = wm_ref[...].astype(BF16)

    step = pl.program_id(0) * pl.num_programs(1) + pl.program_id(1)
    for sd, scr in zip(sides, side_scratch):
        if sd.init:
            sd.init(step, scr)

    placers = [idx for idx, sd in enumerate(sides) if sd.places]
    assert len(placers) <= 1
    widths = sides[placers[0]].places if placers else (tn,)
    n_parts = len(widths)
    assert sum(widths) == tn and all(w % MXU_WIDTH == 0 for w in widths)

    def emit(k):
        lo = sum(widths[:k])
        cols = slice(lo, lo + widths[k])
        o_ref[:, cols] = _dot_nt(h_ref[...], w_bf[cols, :]).astype(o_ref.dtype)

    for idx, sd in enumerate(sides):
        if not sd.places:
            sd.body(step, side_in[idx], side_out[idx], side_scratch[idx], None, 0)
    if placers:
        idx = placers[0]
        sides[idx].body(step, side_in[idx], side_out[idx], side_scratch[idx], emit, n_parts)
    else:
        emit(0)


def _proj(h, wt, col0, ncols, out_dtype, name, tm=1024, tn=1024, sides=()):
    s, d = h.shape
    shift = col0 % tn
    base = col0 - shift
    n_m = s // tm
    n_steps = (ncols // tn) * n_m
    assert shift % SUBLANES == 0 and ncols % tn == 0
    assert all(sd.steps == n_steps or (sd.clamped and sd.steps < n_steps) for sd in sides)

    def flat(spec):
        return pl.BlockSpec(spec.block, lambda j, m: spec.index(j * n_m + m),
                            pipeline_mode=spec.mode)

    in_specs = [
        pl.BlockSpec((tm, d), lambda j, m: (m, 0)),
        pl.BlockSpec((tn, d), lambda j, m: (base // tn + j, 0)),
    ]
    args = [h, wt]
    if shift:
        in_specs.append(pl.BlockSpec((shift, d), lambda j, m: ((base + (j + 1) * tn) // shift, 0)))
        args.append(wt)
    out_specs = [pl.BlockSpec((tm, tn), lambda j, m: (m, j))]
    out_shapes = [jax.ShapeDtypeStruct((s, ncols), out_dtype)]
    scratch = [pltpu.VMEM((tn, d), BF16)]
    for sd in sides:
        in_specs += [flat(sp) for sp in sd.in_specs]
        args += list(sd.args)
    for sd in sides:
        out_specs += [flat(sp) for sp in sd.out_specs]
        out_shapes += list(sd.out_shapes)
    for sd in sides:
        scratch += list(sd.scratch)
    return pl.pallas_call(
        functools.partial(_proj_kernel, shift=shift, tn=tn, sides=tuple(sides)),
        grid=(ncols // tn, n_m),
        in_specs=in_specs,
        out_specs=out_specs,
        out_shape=out_shapes,
        scratch_shapes=scratch,
        compiler_params=_params(("arbitrary", "arbitrary")),
        name=name,
    )(*args)


def _cast_body(step, in_refs, out_refs, scratch_refs, emit, n_parts):
    del step, scratch_refs, emit, n_parts
    out_refs[0][...] = in_refs[0][...].astype(BF16)


def _cast_side(w, rows):
    n, d = w.shape
    nb = n // rows
    spec = _Spec((rows, d), lambda i: (jnp.minimum(i, nb - 1), 0))
    return _Side(body=_cast_body, init=None, args=(w,), in_specs=(spec,), out_specs=(spec,),
                 out_shapes=(jax.ShapeDtypeStruct((n, d), BF16),), scratch=(), steps=nb,
                 clamped=True)


MLSTM_BLOCK = 256


def _cumsum_lanes(v):
    n = v.shape[-1]
    pos = lax.broadcasted_iota(jnp.int32, v.shape, v.ndim - 1)
    shift = 1
    while shift < n:
        v = v + jnp.where(pos >= shift, pltpu.roll(v, shift, v.ndim - 1), 0.0)
        shift *= 2
    return v


def _mlstm_body(step, in_refs, out_refs, scratch_refs, emit, n_parts):
    pa_ref, gt_ref, cw_ref, cb_ref, gbc_ref, mn_ref = in_refs
    v_off = 2 * M_QK
    o_off = 2 * M_QK + M_V
    (out_ref,) = out_refs
    qk_buf, c_s, m_s = scratch_refs
    del step
    halo = SUBLANES
    tc = MLSTM_BLOCK

    qk_buf[halo:halo + tc, :] = pa_ref[:, 0:2 * M_QK].astype(F32)
    cw_half = cw_ref[...] * 0.5
    t = jnp.broadcast_to(cb_ref[...] * 0.5, (tc, 2 * M_QK))
    for tap in range(M_CONV):
        lo = halo - (M_CONV - 1) + tap
        t = t + qk_buf[lo:lo + tc, :] * cw_half[tap:tap + 1, :]
    qk_buf[0:halo, :] = qk_buf[tc:tc + halo, :]
    qk = t * (1.0 + jnp.tanh(t))
    q_all = qk[:, 0:M_QK].astype(BF16)
    k_all = qk[:, M_QK:2 * M_QK] * (M_QK_DIM ** -0.5)

    zt = gt_ref[...] + gbc_ref[:, 0:1]
    lf_all = _log_sigmoid(zt)
    b_all = _cumsum_lanes(lf_all)

    row = lax.broadcasted_iota(jnp.int32, (tc, tc), 0)
    col = lax.broadcasted_iota(jnp.int32, (tc, tc), 1)
    causal = col <= row
    ones_aug = jnp.ones((tc, LANES), BF16)

    def first(h):
        qcols = slice(h * M_QK_DIM, (h + 1) * M_QK_DIM)
        f = M_HEADS + h
        q_h = q_all[:, qcols]
        kt = k_all[:, qcols].T
        m_prev = m_s[h][0:1, 0:1]
        c_prev = c_s[h]
        u = zt[h:h + 1, :] - b_all[f:f + 1, :]
        b_tot = b_all[f:f + 1, tc - 1:tc]
        v_aug = jnp.concatenate([pa_ref[:, v_off + h * M_V_DIM:v_off + (h + 1) * M_V_DIM],
                                 ones_aug], axis=1)
        qk = _dot(q_h, kt.astype(BF16))
        qc = _dot(q_h, c_prev.astype(BF16))
        w_log = b_tot + u
        m_loc = jnp.max(w_log, axis=1, keepdims=True)
        wkt = (kt * jnp.exp(w_log - m_loc)).astype(BF16)
        m_new = jnp.maximum(b_tot + m_prev, m_loc)
        decay = jnp.exp(b_tot + m_prev - m_new)
        gain = jnp.exp(m_loc - m_new)
        m_s[h] = jnp.broadcast_to(m_new, (SUBLANES, LANES))
        return (u, m_prev, v_aug, qk, qc), (wkt, v_aug, c_prev, decay, gain)

    def third(h, wkt, v_aug, c_prev, decay, gain):
        c_s[h] = decay * c_prev + gain * _dot(wkt, v_aug)

    def second(h, u, m_prev, v_aug, qk, qc):
        vcols = slice(h * M_V_DIM, (h + 1) * M_V_DIM)
        f = M_HEADS + h
        um = jnp.where(causal, u, NEG)
        g = jnp.maximum(jnp.max(um, axis=1, keepdims=True), m_prev)
        b_col = jnp.sum(jnp.where(causal, lf_all[f:f + 1, :], 0.0), axis=1, keepdims=True)
        p = jnp.exp(um - g) * qk
        na = _dot(p.astype(BF16), v_aug) + jnp.exp(m_prev - g) * qc
        num = na[:, 0:M_V_DIM]
        den = na[:, M_V_DIM:M_V_DIM + 1]
        hh = num / jnp.maximum(jnp.abs(den), jnp.exp(-(b_col + g)))
        o_gate = pa_ref[:, o_off + h * M_V_DIM:o_off + (h + 1) * M_V_DIM].astype(F32)
        hh = hh * _rms_scale(hh) * mn_ref[0:1, vcols] * _sigmoid(o_gate)
        out_ref[:, vcols] = hh.astype(BF16)

    emit(0)
    parts = [first(h) for h in range(M_HEADS)]
    groups = n_parts - 1
    per = M_HEADS // groups
    for grp in range(groups):
        emit(1 + grp)
        for h in range(grp * per, (grp + 1) * per):
            second(h, *parts[h][0])
    for h in range(M_HEADS):
        third(h, *parts[h][1])
    assert groups * per == M_HEADS


def _mlstm_init(step, scratch_refs):
    qk_buf, c_s, m_s = scratch_refs

    @pl.when(step == 0)
    def _():
        qk_buf[0:SUBLANES, :] = jnp.zeros((SUBLANES, 2 * M_QK), F32)
        c_s[...] = jnp.zeros(c_s.shape, F32)
        m_s[...] = jnp.full(m_s.shape, NEG, F32)


def _mlstm_side(pa, gt, cw, cb, gbc, mn):
    s = pa.shape[0]
    tc = MLSTM_BLOCK
    const = lambda i: (0, 0)
    return _Side(
        body=_mlstm_body,
        init=_mlstm_init,
        args=(pa, gt, cw, cb, gbc, mn),
        in_specs=(
            _Spec((tc, pa.shape[1]), lambda i: (i, 0)),
            _Spec((SUBLANES, tc), lambda i: (0, i)),
            _Spec((M_CONV, 2 * M_QK), const),
            _Spec((1, 2 * M_QK), const),
            _Spec((SUBLANES, LANES), const),
            _Spec((1, M_V), const),
        ),
        out_specs=(_Spec((tc, M_V), lambda i: (i, 0)),),
        out_shapes=(jax.ShapeDtypeStruct((s, M_V), BF16),),
        scratch=(
            pltpu.VMEM((SUBLANES + tc, 2 * M_QK), F32),
            pltpu.VMEM((M_HEADS, M_QK_DIM, M_V_DIM + LANES), F32),
            pltpu.VMEM((M_HEADS, SUBLANES, LANES), F32),
        ),
        steps=s // tc,
        places=(MXU_WIDTH,) * 3,
    )


ATT_G = 4
ATT_ROWS = ATT_G * CHUNK
ATT_KBLOCKS = (A_PAST_CHUNKS + ATT_G) // ATT_G


def _attn_body(step, in_refs, out_refs, scratch_refs, emit, n_parts):
    del step, scratch_refs
    q_ref, k_ref, v_ref, bias_ref = in_refs
    (out_ref,) = out_refs
    scale2 = (A_HEAD_DIM ** -0.5) * LOG2E
    ones = jnp.ones((ATT_KBLOCKS * ATT_ROWS, A_HEAD_DIM), BF16)

    def scores(h):
        cols = slice(h * A_HEAD_DIM, (h + 1) * A_HEAD_DIM)
        return _dot_nt(q_ref[:, cols], k_ref[:, cols])

    def finish(h, s):
        cols = slice(h * A_HEAD_DIM, (h + 1) * A_HEAD_DIM)
        s = s * scale2 + bias_ref[0, h]
        p = jnp.exp2(s - jnp.max(s, axis=1, keepdims=True)).astype(BF16)
        o = _dot(p, jnp.concatenate([v_ref[:, cols], ones], axis=1))
        out_ref[:, cols] = (o[:, 0:A_HEAD_DIM] / o[:, A_HEAD_DIM:A_HEAD_DIM + 1]).astype(BF16)

    per = A_HEADS // n_parts
    s_next = [scores(h) for h in range(per)]
    for grp in range(n_parts):
        s_grp = s_next
        if grp + 1 < n_parts:
            s_next = [scores(h) for h in range((grp + 1) * per, (grp + 2) * per)]
        emit(grp)
        for h, s in zip(range(grp * per, (grp + 1) * per), s_grp):
            finish(h, s)


def _attn_side(pb, bias):
    s = pb.shape[0]
    last = ATT_KBLOCKS - 1

    def kv(group):
        return _Spec((pl.Element(ATT_KBLOCKS * ATT_ROWS), pl.Element(A_W)),
                     lambda g: (jnp.maximum(g - last, 0) * ATT_ROWS, group * A_W))

    return _Side(
        body=_attn_body,
        init=None,
        args=(pb,) * 3 + (bias,),
        in_specs=(
            _Spec((ATT_ROWS, A_W), lambda g: (g, 0)),
            kv(1), kv(2),
            _Spec((1, A_HEADS, ATT_ROWS, ATT_KBLOCKS * ATT_ROWS),
                  lambda g: (jnp.minimum(g, last), 0, 0, 0)),
        ),
        out_specs=(_Spec((ATT_ROWS, A_W), lambda g: (g, 0)),),
        out_shapes=(jax.ShapeDtypeStruct((s, A_W), BF16),),
        scratch=(),
        steps=s // ATT_ROWS,
        places=(MXU_WIDTH,) * 4,
    )


def _band_bias(rel_table):
    hds = rel_table.shape[0]
    wcols = ATT_KBLOCKS * ATT_ROWS
    band_w = (A_PAST_CHUNKS + 1) * CHUNK
    far = A_PAST_CHUNKS * CHUNK - A_MAX_REL + CHUNK
    tab = rel_table.astype(F32) * LOG2E
    e = jnp.concatenate([jnp.broadcast_to(tab[:, 2 * A_MAX_REL:], (hds, far)),
                         tab[:, A_MAX_REL - CHUNK + 1:2 * A_MAX_REL][:, ::-1],
                         jnp.zeros((hds, 1), F32)], axis=1)
    period = e.shape[1]
    tiled = jnp.broadcast_to(e[:, None, :], (hds, CHUNK, period)).reshape(hds, -1)
    skew = tiled[:, :CHUNK * (period - 1)].reshape(hds, CHUNK, period - 1)
    chunk_bias = skew[:, :, CHUNK - 1:CHUNK - 1 + band_w]
    rows = [jnp.pad(chunk_bias, ((0, 0), (0, 0), (ci * CHUNK, wcols - band_w - ci * CHUNK)),
                    constant_values=NEG) for ci in range(ATT_G)]
    base = jnp.concatenate(rows, axis=1)
    variants = []
    for v in range(ATT_KBLOCKS):
        lead = (ATT_KBLOCKS - 1 - v) * ATT_ROWS
        variants.append(jnp.pad(base[:, :, lead:], ((0, 0), (0, 0), (0, lead)), constant_values=NEG))
    return jnp.stack(variants)


def _merge_kernel(hm_ref, ha_ref, gm_ref, ga_ref, x_ref, wm_ref, wa_ref, wo_ref, gb_ref, nf_ref,
                  x1_ref, h2_ref):
    ym = _dot(hm_ref[...], wm_ref[...])
    ya = _dot(ha_ref[...], wa_ref[...])
    merged = (_sigmoid(gm_ref[...].astype(F32) + gb_ref[0:1, :]) * ym
              + _sigmoid(ga_ref[...].astype(F32) + gb_ref[1:2, :]) * ya)
    x1 = x_ref[...] + _dot(merged.astype(BF16), wo_ref[...])
    x1_ref[...] = x1
    h2_ref[...] = (x1 * _rms_scale(x1) * nf_ref[...]).astype(BF16)


def _merge(hm, ha, pc, x, wm, wa, wo, gb, nf, tm=256):
    s, d = x.shape
    const = lambda i: (0, 0)
    resident = pl.Buffered(1)
    return pl.pallas_call(
        _merge_kernel,
        grid=(s // tm,),
        in_specs=[
            pl.BlockSpec((tm, M_V), lambda i: (i, 0)),
            pl.BlockSpec((tm, A_W), lambda i: (i, 0)),
            pl.BlockSpec((tm, d), lambda i: (i, 0)),
            pl.BlockSpec((tm, d), lambda i: (i, 1)),
            pl.BlockSpec((tm, d), lambda i: (i, 0)),
            pl.BlockSpec((M_V, d), const, pipeline_mode=resident),
            pl.BlockSpec((A_W, d), const, pipeline_mode=resident),
            pl.BlockSpec((d, d), const, pipeline_mode=resident),
            pl.BlockSpec((2, d), const),
            pl.BlockSpec((1, d), const),
        ],
        out_specs=[
            pl.BlockSpec((tm, d), lambda i: (i, 0)),
            pl.BlockSpec((tm, d), lambda i: (i, 0)),
        ],
        out_shape=[
            jax.ShapeDtypeStruct((s, d), F32),
            jax.ShapeDtypeStruct((s, d), BF16),
        ],
        compiler_params=_params(("arbitrary",)),
        name="merge",
    )(hm, ha, pc, pc, x, wm, wa, wo, gb, nf)


FFN_SUB_ROWS = 512


def _ffn_up_kernel(h_ref, wg_ref, wv_ref, cw_ref, cb_ref, wd_ref, a_ref, wd_bf_ref, w_bf, ug_buf,
                   *, tm, tf):
    halo = SUBLANES
    rs = FFN_SUB_ROWS
    wd_bf_ref[...] = wd_ref[...].astype(BF16)

    @pl.when(pl.program_id(1) == 0)
    def _():
        w_bf[:, 0:tf] = wg_ref[...].astype(BF16)
        w_bf[:, tf:2 * tf] = wv_ref[...].astype(BF16)
        ug_buf[0:halo, :] = jnp.zeros((halo, tf), F32)

    for r in range(tm // rs):
        h = h_ref[r * rs:(r + 1) * rs, :]
        u = _dot(h, w_bf[...])
        ug = u[:, 0:tf]
        uv = u[:, tf:2 * tf]
        base = halo + r * rs
        ug_buf[base:base + rs, :] = ug
        conv = cb_ref[...] + ug * cw_ref[FFN_CONV - 1:FFN_CONV, :]
        for tap in range(FFN_CONV - 1):
            lo = base - (FFN_CONV - 1) + tap
            conv = conv + ug_buf[lo:lo + rs, :] * cw_ref[tap:tap + 1, :]
        a_ref[r * rs:(r + 1) * rs, :] = (conv * _sigmoid(conv) * uv).astype(BF16)
    ug_buf[0:halo, :] = ug_buf[tm:tm + halo, :]


def _ffn_up(h2, w_up, cw, cb, w_down, tm=2048, tf=512):
    s, d = h2.shape
    nf = D_FF // tf
    n_m = s // tm
    wd_rows = D_FF // (nf * n_m)
    assert wd_rows * nf * n_m == D_FF and wd_rows % (2 * SUBLANES) == 0
    wd_spec = pl.BlockSpec((wd_rows, d), lambda j, m: (j * n_m + m, 0))
    return pl.pallas_call(
        functools.partial(_ffn_up_kernel, tm=tm, tf=tf),
        grid=(nf, n_m),
        in_specs=[
            pl.BlockSpec((tm, d), lambda j, m: (m, 0)),
            pl.BlockSpec((d, tf), lambda j, m: (0, j)),
            pl.BlockSpec((d, tf), lambda j, m: (0, nf + j)),
            pl.BlockSpec((FFN_CONV, tf), lambda j, m: (0, j)),
            pl.BlockSpec((1, tf), lambda j, m: (0, j)),
            wd_spec,
        ],
        out_specs=[pl.BlockSpec((tm, tf), lambda j, m: (m, j)), wd_spec],
        out_shape=[jax.ShapeDtypeStruct((s, D_FF), BF16),
                   jax.ShapeDtypeStruct((D_FF, d), BF16)],
        scratch_shapes=[
            pltpu.VMEM((d, 2 * tf), BF16),
            pltpu.VMEM((SUBLANES + tm, tf), F32),
        ],
        compiler_params=_params(("arbitrary", "arbitrary")),
        name="ffn_up",
    )(h2, w_up, w_up, cw, cb, w_down)


FFN_DOWN_COLS = 512


def _ffn_down_kernel(a_ref, w_ref, x1_ref, nf_ref, o_ref):
    d = o_ref.shape[1]
    ssq = jnp.zeros((o_ref.shape[0], 1), F32)
    for n in range(d // FFN_DOWN_COLS):
        cols = slice(n * FFN_DOWN_COLS, (n + 1) * FFN_DOWN_COLS)
        y = x1_ref[:, cols] + _dot(a_ref[...], w_ref[:, cols])
        o_ref[:, cols] = y
        ssq = ssq + jnp.sum(y * y, axis=-1, keepdims=True)
    o_ref[...] = o_ref[...] * lax.rsqrt(ssq * (1.0 / d) + EPS) * nf_ref[...]


def _ffn_down(a, w_down, x1, nf, tm=512):
    s, d = x1.shape
    dff = a.shape[1]
    return pl.pallas_call(
        _ffn_down_kernel,
        grid=(s // tm,),
        in_specs=[
            pl.BlockSpec((tm, dff), lambda i: (i, 0)),
            pl.BlockSpec((dff, d), lambda i: (0, 0), pipeline_mode=pl.Buffered(1)),
            pl.BlockSpec((tm, d), lambda i: (i, 0)),
            pl.BlockSpec((1, d), lambda i: (0, 0)),
        ],
        out_specs=pl.BlockSpec((tm, d), lambda i: (i, 0)),
        out_shape=jax.ShapeDtypeStruct((s, d), F32),
        compiler_params=_params(("arbitrary",)),
        name="ffn_down",
    )(a, w_down, x1, nf)


def kernel(x, norm_mix, w_in, conv_qk_w, conv_qk_b, b_igate, b_fgate, m_norm, rel_bias, gate_bias,
           w_branch_m, w_branch_a, w_out, norm_ffn, w_up, conv_ffn_w, conv_ffn_b, w_down, norm_final):
    batch, seq, d = x.shape
    depth = w_in.shape[0]
    assert (batch, seq, d, depth) == (1, SEQ, D_MODEL, 1)
    xs = x[0]
    l = 0

    w_in_t = jnp.swapaxes(w_in[l], 0, 1)
    w_g = jnp.pad(w_in_t[GATE_COL0:GATE_COL0 + N_GATES, :].T, ((0, 0), (0, LANES - N_GATES)))
    gate_b = jnp.concatenate([b_igate[l], b_fgate[l]])
    gb_col = jnp.broadcast_to(gate_b[:, None], (SUBLANES, LANES))
    bias = _band_bias(rel_bias[l])

    h, gt = _prenorm(xs, norm_mix[l][None, :], w_g.astype(BF16))
    na = 2 * M_QK + 2 * M_V
    attn0 = GATE_COL0 + N_GATES
    pa, w_m_bf, w_a_bf = _proj(h, w_in_t, 0, na, BF16, "proj_mlstm",
                               sides=[_cast_side(w_branch_m[l], 128), _cast_side(w_branch_a[l], 128)])
    mlstm = _mlstm_side(pa, gt, conv_qk_w[l], conv_qk_b[l][None, :], gb_col, m_norm[l][None, :])
    pb, hm, w_out_bf = _proj(h, w_in_t, attn0, 3 * A_W, BF16, "proj_attn_mlstm", tn=768,
                             sides=[mlstm, _cast_side(w_out[l], 128)])
    pc, ha = _proj(h, w_in_t, attn0 + 3 * A_W, 2 * D_MODEL, BF16, "proj_gates_attn",
                   sides=[_attn_side(pb, bias)])

    x1, h2 = _merge(hm, ha, pc, xs, w_m_bf, w_a_bf, w_out_bf, gate_bias[l], norm_ffn[l][None, :])
    a, w_down_bf = _ffn_up(h2, w_up[l], conv_ffn_w[l], conv_ffn_b[l][None, :], w_down[l])
    out = _ffn_down(a, w_down_bf, x1, norm_final[None, :])
    return out[None]
```

```python
import functools
import math
from typing import Callable, NamedTuple

import jax
import jax.numpy as jnp
from jax import lax
from jax.experimental import pallas as pl
from jax.experimental.pallas import tpu as pltpu

D_MODEL = 2048
SEQ = 8192
CHUNK = 64
M_HEADS = 4
M_QK_DIM = 128
M_V_DIM = 256
M_CONV = 4
A_HEADS = 8
A_HEAD_DIM = 128
A_PAST_CHUNKS = 8
A_MAX_REL = 128
D_FF = 5632
FFN_CONV = 3
EPS = 1e-6
NEG = -1e30
LOG2E = math.log2(math.e)

M_QK = M_HEADS * M_QK_DIM
M_V = M_HEADS * M_V_DIM
A_W = A_HEADS * A_HEAD_DIM
GATE_COL0 = 2 * M_QK + 2 * M_V
N_GATES = 2 * M_HEADS

LANES = 128
SUBLANES = 8
MXU_WIDTH = 256
VMEM_LIMIT = 56 * 1024 * 1024

BF16 = jnp.bfloat16
F32 = jnp.float32


def _params(sem):
    return pltpu.CompilerParams(dimension_semantics=sem, vmem_limit_bytes=VMEM_LIMIT)


def _dot(a, b):
    return jnp.dot(a, b, preferred_element_type=F32)


def _dot_nt(a, b):
    return lax.dot_general(a, b, (((1,), (1,)), ((), ())), preferred_element_type=F32)


def _sigmoid(z):
    return 0.5 * jnp.tanh(0.5 * z) + 0.5


def _log_sigmoid(z):
    return jnp.minimum(z, 0.0) - jnp.log(1.0 + jnp.exp(-jnp.abs(z)))


def _rms_scale(y):
    return lax.rsqrt(jnp.mean(y * y, axis=-1, keepdims=True) + EPS)


def _prenorm_kernel(x_ref, g_ref, wg_ref, h_ref, gt_ref):
    xf = x_ref[...]
    hb = (xf * _rms_scale(xf) * g_ref[...]).astype(BF16)
    h_ref[...] = hb
    gcol = _dot(hb, wg_ref[...])
    gt_ref[...] = gcol.T[0:SUBLANES, :]


def _prenorm(x, g, wg, tm=1024):
    s, d = x.shape
    return pl.pallas_call(
        _prenorm_kernel,
        grid=(s // tm,),
        in_specs=[
            pl.BlockSpec((tm, d), lambda i: (i, 0)),
            pl.BlockSpec((1, d), lambda i: (0, 0)),
            pl.BlockSpec((d, LANES), lambda i: (0, 0)),
        ],
        out_specs=[
            pl.BlockSpec((tm, d), lambda i: (i, 0)),
            pl.BlockSpec((SUBLANES, tm), lambda i: (0, i)),
        ],
        out_shape=[
            jax.ShapeDtypeStruct((s, d), BF16),
            jax.ShapeDtypeStruct((SUBLANES, s), F32),
        ],
        compiler_params=_params(("arbitrary",)),
        name="prenorm",
    )(x, g, wg)


class _Spec(NamedTuple):
    block: tuple
    index: Callable
    mode: object = None


class _Side(NamedTuple):
    body: Callable
    init: Callable
    args: tuple
    in_specs: tuple
    out_specs: tuple
    out_shapes: tuple
    scratch: tuple
    steps: int
    clamped: bool = False
    places: tuple = ()


def _proj_kernel(*refs, shift, tn, sides):
    n_w = 3 if shift else 2
    h_ref, wm_ref = refs[0], refs[1]
    pos = n_w
    side_in = []
    for sd in sides:
        side_in.append(refs[pos:pos + len(sd.args)])
        pos += len(sd.args)
    o_ref = refs[pos]
    pos += 1
    side_out = []
    for sd in sides:
        side_out.append(refs[pos:pos + len(sd.out_specs)])
        pos += len(sd.out_specs)
    w_bf = refs[pos]
    pos += 1
    side_scratch = []
    for sd in sides:
        side_scratch.append(refs[pos:pos + len(sd.scratch)])
        pos += len(sd.scratch)

    @pl.when(pl.program_id(1) == 0)
    def _():
        if shift:
            w_bf[0:tn - shift, :] = wm_ref[shift:tn, :].astype(BF16)
            w_bf[tn - shift:tn, :] = refs[2][...].astype(BF16)
        else:
            w_bf[...] = wm_ref[...].astype(BF16)

    step = pl.program_id(0) * pl.num_programs(1) + pl.program_id(1)
    for sd, scr in zip(sides, side_scratch):
        if sd.init:
            sd.init(step, scr)

    placers = [idx for idx, sd in enumerate(sides) if sd.places]
    assert len(placers) <= 1
    widths = sides[placers[0]].places if placers else (tn,)
    n_parts = len(widths)
    assert sum(widths) == tn and all(w % MXU_WIDTH == 0 for w in widths)

    def emit(k):
        lo = sum(widths[:k])
        cols = slice(lo, lo + widths[k])
        o_ref[:, cols] = _dot_nt(h_ref[...], w_bf[cols, :]).astype(o_ref.dtype)

    for idx, sd in enumerate(sides):
        if not sd.places:
            sd.body(step, side_in[idx], side_out[idx], side_scratch[idx], None, 0)
    if placers:
        idx = placers[0]
        sides[idx].body(step, side_in[idx], side_out[idx], side_scratch[idx], emit, n_parts)
    else:
        emit(0)


def _proj(h, wt, col0, ncols, out_dtype, name, tm=1024, tn=1024, sides=()):
    s, d = h.shape
    shift = col0 % tn
    base = col0 - shift
    n_m = s // tm
    n_steps = (ncols // tn) * n_m
    assert shift % SUBLANES == 0 and ncols % tn == 0
    assert all(sd.steps == n_steps or (sd.clamped and sd.steps < n_steps) for sd in sides)

    def flat(spec):
        return pl.BlockSpec(spec.block, lambda j, m: spec.index(j * n_m + m),
                            pipeline_mode=spec.mode)

    in_specs = [
        pl.BlockSpec((tm, d), lambda j, m: (m, 0)),
        pl.BlockSpec((tn, d), lambda j, m: (base // tn + j, 0)),
    ]
    args = [h, wt]
    if shift:
        in_specs.append(pl.BlockSpec((shift, d), lambda j, m: ((base + (j + 1) * tn) // shift, 0)))
        args.append(wt)
    out_specs = [pl.BlockSpec((tm, tn), lambda j, m: (m, j))]
    out_shapes = [jax.ShapeDtypeStruct((s, ncols), out_dtype)]
    scratch = [pltpu.VMEM((tn, d), BF16)]
    for sd in sides:
        in_specs += [flat(sp) for sp in sd.in_specs]
        args += list(sd.args)
    for sd in sides:
        out_specs += [flat(sp) for sp in sd.out_specs]
        out_shapes += list(sd.out_shapes)
    for sd in sides:
        scratch += list(sd.scratch)
    return pl.pallas_call(
        functools.partial(_proj_kernel, shift=shift, tn=tn, sides=tuple(sides)),
        grid=(ncols // tn, n_m),
        in_specs=in_specs,
        out_specs=out_specs,
        out_shape=out_shapes,
        scratch_shapes=scratch,
        compiler_params=_params(("arbitrary", "arbitrary")),
        name=name,
    )(*args)


def _cast_body(step, in_refs, out_refs, scratch_refs, emit, n_parts):
    del step, scratch_refs, emit, n_parts
    out_refs[0][...] = in_refs[0][...].astype(BF16)


def _cast_side(w, rows):
    n, d = w.shape
    nb = n // rows
    spec = _Spec((rows, d), lambda i: (jnp.minimum(i, nb - 1), 0))
    return _Side(body=_cast_body, init=None, args=(w,), in_specs=(spec,), out_specs=(spec,),
                 out_shapes=(jax.ShapeDtypeStruct((n, d), BF16),), scratch=(), steps=nb,
                 clamped=True)


MLSTM_BLOCK = 256


def _cumsum_lanes(v):
    n = v.shape[-1]
    pos = lax.broadcasted_iota(jnp.int32, v.shape, v.ndim - 1)
    shift = 1
    while shift < n:
        v = v + jnp.where(pos >= shift, pltpu.roll(v, shift, v.ndim - 1), 0.0)
        shift *= 2
    return v


def _mlstm_body(step, in_refs, out_refs, scratch_refs, emit, n_parts):
    pa_ref, gt_ref, cw_ref, cb_ref, gbc_ref, mn_ref = in_refs
    v_off = 2 * M_QK
    o_off = 2 * M_QK + M_V
    (out_ref,) = out_refs
    qk_buf, c_s, m_s = scratch_refs
    del step
    halo = SUBLANES
    tc = MLSTM_BLOCK

    qk_buf[halo:halo + tc, :] = pa_ref[:, 0:2 * M_QK].astype(F32)
    cw_half = cw_ref[...] * 0.5
    t = jnp.broadcast_to(cb_ref[...] * 0.5, (tc, 2 * M_QK))
    for tap in range(M_CONV):
        lo = halo - (M_CONV - 1) + tap
        t = t + qk_buf[lo:lo + tc, :] * cw_half[tap:tap + 1, :]
    qk_buf[0:halo, :] = qk_buf[tc:tc + halo, :]
    qk = t * (1.0 + jnp.tanh(t))
    q_all = qk[:, 0:M_QK].astype(BF16)
    k_all = qk[:, M_QK:2 * M_QK] * (M_QK_DIM ** -0.5)

    zt = gt_ref[...] + gbc_ref[:, 0:1]
    lf_all = _log_sigmoid(zt)
    b_all = _cumsum_lanes(lf_all)

    row = lax.broadcasted_iota(jnp.int32, (tc, tc), 0)
    col = lax.broadcasted_iota(jnp.int32, (tc, tc), 1)
    causal = col <= row
    ones_aug = jnp.ones((tc, LANES), BF16)

    def inter(h):
        c_prev = c_s[h]
        qc = _dot(q_all[:, h * M_QK_DIM:(h + 1) * M_QK_DIM], c_prev.astype(BF16))
        return c_prev, qc

    def first(h, c_prev, qc):
        qcols = slice(h * M_QK_DIM, (h + 1) * M_QK_DIM)
        f = M_HEADS + h
        q_h = q_all[:, qcols]
        kt = k_all[:, qcols].T
        m_prev = m_s[h][0:1, 0:1]
        u = zt[h:h + 1, :] - b_all[f:f + 1, :]
        b_tot = b_all[f:f + 1, tc - 1:tc]
        v_aug = jnp.concatenate([pa_ref[:, v_off + h * M_V_DIM:v_off + (h + 1) * M_V_DIM],
                                 ones_aug], axis=1)
        qk = _dot(q_h, kt.astype(BF16))
        w_log = b_tot + u
        m_loc = jnp.max(w_log, axis=1, keepdims=True)
        wkt = (kt * jnp.exp(w_log - m_loc)).astype(BF16)
        m_new = jnp.maximum(b_tot + m_prev, m_loc)
        decay = jnp.exp(b_tot + m_prev - m_new)
        gain = jnp.exp(m_loc - m_new)
        m_s[h] = jnp.broadcast_to(m_new, (SUBLANES, LANES))
        return (u, m_prev, v_aug, qk, qc), (wkt, v_aug, c_prev, decay, gain)

    def third(h, wkt, v_aug, c_prev, decay, gain):
        c_s[h] = decay * c_prev + gain * _dot(wkt, v_aug)

    def second(h, u, m_prev, v_aug, qk, qc):
        vcols = slice(h * M_V_DIM, (h + 1) * M_V_DIM)
        f = M_HEADS + h
        um = jnp.where(causal, u, NEG)
        g = jnp.maximum(jnp.max(um, axis=1, keepdims=True), m_prev)
        b_col = jnp.sum(jnp.where(causal, lf_all[f:f + 1, :], 0.0), axis=1, keepdims=True)
        p = jnp.exp(um - g) * qk
        na = _dot(p.astype(BF16), v_aug) + jnp.exp(m_prev - g) * qc
        num = na[:, 0:M_V_DIM]
        den = na[:, M_V_DIM:M_V_DIM + 1]
        hh = num / jnp.maximum(jnp.abs(den), jnp.exp(-(b_col + g)))
        o_gate = pa_ref[:, o_off + h * M_V_DIM:o_off + (h + 1) * M_V_DIM].astype(F32)
        hh = hh * _rms_scale(hh) * mn_ref[0:1, vcols] * _sigmoid(o_gate)
        out_ref[:, vcols] = hh.astype(BF16)

    emit(0)
    parts = [first(h, *inter(h)) for h in range(M_HEADS)]
    groups = n_parts - 1
    per = M_HEADS // groups
    for grp in range(groups):
        emit(1 + grp)
        for h in range(grp * per, (grp + 1) * per):
            second(h, *parts[h][0])
    for h in range(M_HEADS):
        third(h, *parts[h][1])
    assert groups * per == M_HEADS


def _mlstm_init(step, scratch_refs):
    qk_buf, c_s, m_s = scratch_refs

    @pl.when(step == 0)
    def _():
        qk_buf[0:SUBLANES, :] = jnp.zeros((SUBLANES, 2 * M_QK), F32)
        c_s[...] = jnp.zeros(c_s.shape, F32)
        m_s[...] = jnp.full(m_s.shape, NEG, F32)


def _mlstm_side(pa, gt, cw, cb, gbc, mn):
    s = pa.shape[0]
    tc = MLSTM_BLOCK
    const = lambda i: (0, 0)
    return _Side(
        body=_mlstm_body,
        init=_mlstm_init,
        args=(pa, gt, cw, cb, gbc, mn),
        in_specs=(
            _Spec((tc, pa.shape[1]), lambda i: (i, 0)),
            _Spec((SUBLANES, tc), lambda i: (0, i)),
            _Spec((M_CONV, 2 * M_QK), const),
            _Spec((1, 2 * M_QK), const),
            _Spec((SUBLANES, LANES), const),
            _Spec((1, M_V), const),
        ),
        out_specs=(_Spec((tc, M_V), lambda i: (i, 0)),),
        out_shapes=(jax.ShapeDtypeStruct((s, M_V), BF16),),
        scratch=(
            pltpu.VMEM((SUBLANES + tc, 2 * M_QK), F32),
            pltpu.VMEM((M_HEADS, M_QK_DIM, M_V_DIM + LANES), F32),
            pltpu.VMEM((M_HEADS, SUBLANES, LANES), F32),
        ),
        steps=s // tc,
        places=(MXU_WIDTH,) * 3,
    )


ATT_G = 4
ATT_ROWS = ATT_G * CHUNK
ATT_KBLOCKS = (A_PAST_CHUNKS + ATT_G) // ATT_G


def _attn_body(step, in_refs, out_refs, scratch_refs, emit, n_parts):
    del step, scratch_refs
    q_ref, k_ref, v_ref, bias_ref = in_refs
    (out_ref,) = out_refs
    scale2 = (A_HEAD_DIM ** -0.5) * LOG2E
    ones = jnp.ones((ATT_KBLOCKS * ATT_ROWS, A_HEAD_DIM), BF16)

    def scores(h):
        cols = slice(h * A_HEAD_DIM, (h + 1) * A_HEAD_DIM)
        return _dot_nt(q_ref[:, cols], k_ref[:, cols])

    def finish(h, s):
        cols = slice(h * A_HEAD_DIM, (h + 1) * A_HEAD_DIM)
        s = s * scale2 + bias_ref[h]
        p = jnp.exp2(s - jnp.max(s, axis=1, keepdims=True)).astype(BF16)
        o = _dot(p, jnp.concatenate([v_ref[:, cols], ones], axis=1))
        out_ref[:, cols] = (o[:, 0:A_HEAD_DIM] / o[:, A_HEAD_DIM:A_HEAD_DIM + 1]).astype(BF16)

    per = A_HEADS // n_parts
    s_next = [scores(h) for h in range(per)]
    for grp in range(n_parts):
        s_grp = s_next
        if grp + 1 < n_parts:
            s_next = [scores(h) for h in range((grp + 1) * per, (grp + 2) * per)]
        emit(grp)
        for h, s in zip(range(grp * per, (grp + 1) * per), s_grp):
            finish(h, s)


def _attn_side(pb, bias):
    s = pb.shape[0]
    last = ATT_KBLOCKS - 1

    def kv(group):
        return _Spec((pl.Element(ATT_KBLOCKS * ATT_ROWS), pl.Element(A_W)),
                     lambda g: (jnp.maximum(g - last, 0) * ATT_ROWS, group * A_W))

    return _Side(
        body=_attn_body,
        init=None,
        args=(pb,) * 3 + (bias,),
        in_specs=(
            _Spec((ATT_ROWS, A_W), lambda g: (g, 0)),
            kv(1), kv(2),
            _Spec((pl.Element(A_HEADS), pl.Element(ATT_ROWS), pl.Element(ATT_KBLOCKS * ATT_ROWS)),
                  lambda g: (0, 0, jnp.maximum(last - g, 0) * ATT_ROWS)),
        ),
        out_specs=(_Spec((ATT_ROWS, A_W), lambda g: (g, 0)),),
        out_shapes=(jax.ShapeDtypeStruct((s, A_W), BF16),),
        scratch=(),
        steps=s // ATT_ROWS,
        places=(MXU_WIDTH,) * 4,
    )


def _band_bias(rel_table):
    hds = rel_table.shape[0]
    wcols = ATT_KBLOCKS * ATT_ROWS
    band_w = (A_PAST_CHUNKS + 1) * CHUNK
    far = A_PAST_CHUNKS * CHUNK - A_MAX_REL + CHUNK
    tab = rel_table.astype(F32) * LOG2E
    e = jnp.concatenate([jnp.broadcast_to(tab[:, 2 * A_MAX_REL:], (hds, far)),
                         tab[:, A_MAX_REL - CHUNK + 1:2 * A_MAX_REL][:, ::-1],
                         jnp.zeros((hds, 1), F32)], axis=1)
    period = e.shape[1]
    tiled = jnp.broadcast_to(e[:, None, :], (hds, CHUNK, period)).reshape(hds, -1)
    skew = tiled[:, :CHUNK * (period - 1)].reshape(hds, CHUNK, period - 1)
    chunk_bias = skew[:, :, CHUNK - 1:CHUNK - 1 + band_w]
    total = wcols + (ATT_KBLOCKS - 1) * ATT_ROWS
    rows = [jnp.pad(chunk_bias, ((0, 0), (0, 0), (ci * CHUNK, total - band_w - ci * CHUNK)),
                    constant_values=NEG) for ci in range(ATT_G)]
    return jnp.concatenate(rows, axis=1)


def _merge_kernel(hm_ref, ha_ref, gm_ref, ga_ref, x_ref, wm_ref, wa_ref, wo_ref, gb_ref, nf_ref,
                  x1_ref, h2_ref):
    ym = _dot(hm_ref[...], wm_ref[...])
    ya = _dot(ha_ref[...], wa_ref[...])
    merged = (_sigmoid(gm_ref[...].astype(F32) + gb_ref[0:1, :]) * ym
              + _sigmoid(ga_ref[...].astype(F32) + gb_ref[1:2, :]) * ya)
    x1 = x_ref[...] + _dot(merged.astype(BF16), wo_ref[...])
    x1_ref[...] = x1
    h2_ref[...] = (x1 * _rms_scale(x1) * nf_ref[...]).astype(BF16)


def _merge(hm, ha, pc, x, wm, wa, wo, gb, nf, tm=256):
    s, d = x.shape
    const = lambda i: (0, 0)
    resident = pl.Buffered(1)
    return pl.pallas_call(
        _merge_kernel,
        grid=(s // tm,),
        in_specs=[
            pl.BlockSpec((tm, M_V), lambda i: (i, 0)),
            pl.BlockSpec((tm, A_W), lambda i: (i, 0)),
            pl.BlockSpec((tm, d), lambda i: (i, 0)),
            pl.BlockSpec((tm, d), lambda i: (i, 1)),
            pl.BlockSpec((tm, d), lambda i: (i, 0)),
            pl.BlockSpec((M_V, d), const, pipeline_mode=resident),
            pl.BlockSpec((A_W, d), const, pipeline_mode=resident),
            pl.BlockSpec((d, d), const, pipeline_mode=resident),
            pl.BlockSpec((2, d), const),
            pl.BlockSpec((1, d), const),
        ],
        out_specs=[
            pl.BlockSpec((tm, d), lambda i: (i, 0)),
            pl.BlockSpec((tm, d), lambda i: (i, 0)),
        ],
        out_shape=[
            jax.ShapeDtypeStruct((s, d), F32),
            jax.ShapeDtypeStruct((s, d), BF16),
        ],
        compiler_params=_params(("arbitrary",)),
        name="merge",
    )(hm, ha, pc, pc, x, wm, wa, wo, gb, nf)


FFN_SUB_ROWS = 512


def _ffn_up_kernel(h_ref, wg_ref, wv_ref, cw_ref, cb_ref, wd_ref, a_ref, wd_bf_ref, w_bf, ug_buf,
                   *, tm, tf):
    halo = SUBLANES
    rs = FFN_SUB_ROWS
    wd_bf_ref[...] = wd_ref[...].astype(BF16)

    @pl.when(pl.program_id(1) == 0)
    def _():
        w_bf[:, 0:tf] = wg_ref[...].astype(BF16)
        w_bf[:, tf:2 * tf] = wv_ref[...].astype(BF16)
        ug_buf[0:halo, :] = jnp.zeros((halo, tf), F32)

    for r in range(tm // rs):
        h = h_ref[r * rs:(r + 1) * rs, :]
        u = _dot(h, w_bf[...])
        ug = u[:, 0:tf]
        uv = u[:, tf:2 * tf]
        base = halo + r * rs
        ug_buf[base:base + rs, :] = ug
        conv = cb_ref[...] + ug * cw_ref[FFN_CONV - 1:FFN_CONV, :]
        for tap in range(FFN_CONV - 1):
            lo = base - (FFN_CONV - 1) + tap
            conv = conv + ug_buf[lo:lo + rs, :] * cw_ref[tap:tap + 1, :]
        a_ref[r * rs:(r + 1) * rs, :] = (conv * _sigmoid(conv) * uv).astype(BF16)
    ug_buf[0:halo, :] = ug_buf[tm:tm + halo, :]


def _ffn_up(h2, w_up, cw, cb, w_down, tm=2048, tf=512):
    s, d = h2.shape
    nf = D_FF // tf
    n_m = s // tm
    wd_rows = D_FF // (nf * n_m)
    assert wd_rows * nf * n_m == D_FF and wd_rows % (2 * SUBLANES) == 0
    wd_spec = pl.BlockSpec((wd_rows, d), lambda j, m: (j * n_m + m, 0))
    return pl.pallas_call(
        functools.partial(_ffn_up_kernel, tm=tm, tf=tf),
        grid=(nf, n_m),
        in_specs=[
            pl.BlockSpec((tm, d), lambda j, m: (m, 0)),
            pl.BlockSpec((d, tf), lambda j, m: (0, j)),
            pl.BlockSpec((d, tf), lambda j, m: (0, nf + j)),
            pl.BlockSpec((FFN_CONV, tf), lambda j, m: (0, j)),
            pl.BlockSpec((1, tf), lambda j, m: (0, j)),
            wd_spec,
        ],
        out_specs=[pl.BlockSpec((tm, tf), lambda j, m: (m, j)), wd_spec],
        out_shape=[jax.ShapeDtypeStruct((s, D_FF), BF16),
                   jax.ShapeDtypeStruct((D_FF, d), BF16)],
        scratch_shapes=[
            pltpu.VMEM((d, 2 * tf), BF16),
            pltpu.VMEM((SUBLANES + tm, tf), F32),
        ],
        compiler_params=_params(("arbitrary", "arbitrary")),
        name="ffn_up",
    )(h2, w_up, w_up, cw, cb, w_down)


FFN_DOWN_COLS = 512


def _ffn_down_kernel(a_ref, w_ref, x1_ref, nf_ref, o_ref):
    d = o_ref.shape[1]
    ssq = jnp.zeros((o_ref.shape[0], 1), F32)
    for n in range(d // FFN_DOWN_COLS):
        cols = slice(n * FFN_DOWN_COLS, (n + 1) * FFN_DOWN_COLS)
        y = x1_ref[:, cols] + _dot(a_ref[...], w_ref[:, cols])
        o_ref[:, cols] = y
        ssq = ssq + jnp.sum(y * y, axis=-1, keepdims=True)
    o_ref[...] = o_ref[...] * lax.rsqrt(ssq * (1.0 / d) + EPS) * nf_ref[...]


def _ffn_down(a, w_down, x1, nf, tm=512):
    s, d = x1.shape
    dff = a.shape[1]
    return pl.pallas_call(
        _ffn_down_kernel,
        grid=(s // tm,),
        in_specs=[
            pl.BlockSpec((tm, dff), lambda i: (i, 0)),
            pl.BlockSpec((dff, d), lambda i: (0, 0), pipeline_mode=pl.Buffered(1)),
            pl.BlockSpec((tm, d), lambda i: (i, 0)),
            pl.BlockSpec((1, d), lambda i: (0, 0)),
        ],
        out_specs=pl.BlockSpec((tm, d), lambda i: (i, 0)),
        out_shape=jax.ShapeDtypeStruct((s, d), F32),
        compiler_params=_params(("arbitrary",)),
        name="ffn_down",
    )(a, w_down, x1, nf)


def kernel(x, norm_mix, w_in, conv_qk_w, conv_qk_b, b_igate, b_fgate, m_norm, rel_bias, gate_bias,
           w_branch_m, w_branch_a, w_out, norm_ffn, w_up, conv_ffn_w, conv_ffn_b, w_down, norm_final):
    batch, seq, d = x.shape
    depth = w_in.shape[0]
    assert (batch, seq, d, depth) == (1, SEQ, D_MODEL, 1)
    xs = x[0]
    l = 0

    w_in_t = jnp.swapaxes(w_in[l], 0, 1)
    w_g = jnp.pad(w_in_t[GATE_COL0:GATE_COL0 + N_GATES, :].T, ((0, 0), (0, LANES - N_GATES)))
    gate_b = jnp.concatenate([b_igate[l], b_fgate[l]])
    gb_col = jnp.broadcast_to(gate_b[:, None], (SUBLANES, LANES))
    bias = _band_bias(rel_bias[l])

    h, gt = _prenorm(xs, norm_mix[l][None, :], w_g.astype(BF16))
    na = 2 * M_QK + 2 * M_V
    attn0 = GATE_COL0 + N_GATES
    pa, w_m_bf, w_a_bf = _proj(h, w_in_t, 0, na, BF16, "proj_mlstm",
                               sides=[_cast_side(w_branch_m[l], 128), _cast_side(w_branch_a[l], 128)])
    mlstm = _mlstm_side(pa, gt, conv_qk_w[l], conv_qk_b[l][None, :], gb_col, m_norm[l][None, :])
    pb, hm, w_out_bf = _proj(h, w_in_t, attn0, 3 * A_W, BF16, "proj_attn_mlstm", tn=768,
                             sides=[mlstm, _cast_side(w_out[l], 128)])
    pc, ha = _proj(h, w_in_t, attn0 + 3 * A_W, 2 * D_MODEL, BF16, "proj_gates_attn",
                   sides=[_attn_side(pb, bias)])

    x1, h2 = _merge(hm, ha, pc, xs, w_m_bf, w_a_bf, w_out_bf, gate_bias[l], norm_ffn[l][None, :])
    a, w_down_bf = _ffn_up(h2, w_up[l], conv_ffn_w[l], conv_ffn_b[l][None, :], w_down[l])
    out = _ffn_down(a, w_down_bf, x1, norm_final[None, :])
    return out[None]
```

```python
import functools
import math
from typing import Callable, NamedTuple

import jax
import jax.numpy as jnp
from jax import lax
from jax.experimental import pallas as pl
from jax.experimental.pallas import tpu as pltpu

D_MODEL = 2048
SEQ = 8192
CHUNK = 64
M_HEADS = 4
M_QK_DIM = 128
M_V_DIM = 256
M_CONV = 4
A_HEADS = 8
A_HEAD_DIM = 128
A_PAST_CHUNKS = 8
A_MAX_REL = 128
D_FF = 5632
FFN_CONV = 3
EPS = 1e-6
NEG = -1e30
LOG2E = math.log2(math.e)

M_QK = M_HEADS * M_QK_DIM
M_V = M_HEADS * M_V_DIM
A_W = A_HEADS * A_HEAD_DIM
GATE_COL0 = 2 * M_QK + 2 * M_V
N_GATES = 2 * M_HEADS

LANES = 128
SUBLANES = 8
MXU_WIDTH = 256
VMEM_LIMIT = 56 * 1024 * 1024

BF16 = jnp.bfloat16
F32 = jnp.float32


def _params(sem):
    return pltpu.CompilerParams(dimension_semantics=sem, vmem_limit_bytes=VMEM_LIMIT)


def _dot(a, b):
    return jnp.dot(a, b, preferred_element_type=F32)


def _dot_nt(a, b):
    return lax.dot_general(a, b, (((1,), (1,)), ((), ())), preferred_element_type=F32)


def _sigmoid(z):
    return 0.5 * jnp.tanh(0.5 * z) + 0.5


def _log_sigmoid(z):
    return jnp.minimum(z, 0.0) - jnp.log(1.0 + jnp.exp(-jnp.abs(z)))


def _rms_scale(y):
    return lax.rsqrt(jnp.mean(y * y, axis=-1, keepdims=True) + EPS)


def _prenorm_kernel(x_ref, g_ref, wg_ref, h_ref, gt_ref):
    xf = x_ref[...]
    hb = (xf * _rms_scale(xf) * g_ref[...]).astype(BF16)
    h_ref[...] = hb
    gcol = _dot(hb, wg_ref[...])
    gt_ref[...] = gcol.T[0:SUBLANES, :]


def _prenorm(x, g, wg, tm=1024):
    s, d = x.shape
    return pl.pallas_call(
        _prenorm_kernel,
        grid=(s // tm,),
        in_specs=[
            pl.BlockSpec((tm, d), lambda i: (i, 0)),
            pl.BlockSpec((1, d), lambda i: (0, 0)),
            pl.BlockSpec((d, LANES), lambda i: (0, 0)),
        ],
        out_specs=[
            pl.BlockSpec((tm, d), lambda i: (i, 0)),
            pl.BlockSpec((SUBLANES, tm), lambda i: (0, i)),
        ],
        out_shape=[
            jax.ShapeDtypeStruct((s, d), BF16),
            jax.ShapeDtypeStruct((SUBLANES, s), F32),
        ],
        compiler_params=_params(("arbitrary",)),
        name="prenorm",
    )(x, g, wg)


class _Spec(NamedTuple):
    block: tuple
    index: Callable
    mode: object = None


class _Side(NamedTuple):
    body: Callable
    init: Callable
    args: tuple
    in_specs: tuple
    out_specs: tuple
    out_shapes: tuple
    scratch: tuple
    steps: int
    clamped: bool = False
    places: tuple = ()


def _proj_kernel(*refs, shift, tn, sides):
    n_w = 3 if shift else 2
    h_ref, wm_ref = refs[0], refs[1]
    pos = n_w
    side_in = []
    for sd in sides:
        side_in.append(refs[pos:pos + len(sd.args)])
        pos += len(sd.args)
    o_ref = refs[pos]
    pos += 1
    side_out = []
    for sd in sides:
        side_out.append(refs[pos:pos + len(sd.out_specs)])
        pos += len(sd.out_specs)
    w_bf = refs[pos]
    pos += 1
    side_scratch = []
    for sd in sides:
        side_scratch.append(refs[pos:pos + len(sd.scratch)])
        pos += len(sd.scratch)

    @pl.when(pl.program_id(1) == 0)
    def _():
        if shift:
            w_bf[0:tn - shift, :] = wm_ref[shift:tn, :].astype(BF16)
            w_bf[tn - shift:tn, :] = refs[2][...].astype(BF16)
        else:
            w_bf[...] = wm_ref[...].astype(BF16)

    step = pl.program_id(0) * pl.num_programs(1) + pl.program_id(1)
    for sd, scr in zip(sides, side_scratch):
        if sd.init:
            sd.init(step, scr)

    placers = [idx for idx, sd in enumerate(sides) if sd.places]
    assert len(placers) <= 1
    widths = sides[placers[0]].places if placers else (tn,)
    n_parts = len(widths)
    assert sum(widths) == tn and all(w % MXU_WIDTH == 0 for w in widths)

    def emit(k):
        lo = sum(widths[:k])
        cols = slice(lo, lo + widths[k])
        o_ref[:, cols] = _dot_nt(h_ref[...], w_bf[cols, :]).astype(o_ref.dtype)

    for idx, sd in enumerate(sides):
        if not sd.places:
            sd.body(step, side_in[idx], side_out[idx], side_scratch[idx], None, 0)
    if placers:
        idx = placers[0]
        sides[idx].body(step, side_in[idx], side_out[idx], side_scratch[idx], emit, n_parts)
    else:
        emit(0)


def _proj(h, wt, col0, ncols, out_dtype, name, tm=1024, tn=1024, sides=()):
    s, d = h.shape
    shift = col0 % tn
    base = col0 - shift
    n_m = s // tm
    n_steps = (ncols // tn) * n_m
    assert shift % SUBLANES == 0 and ncols % tn == 0
    assert all(sd.steps == n_steps or (sd.clamped and sd.steps < n_steps) for sd in sides)

    def flat(spec):
        return pl.BlockSpec(spec.block, lambda j, m: spec.index(j * n_m + m),
                            pipeline_mode=spec.mode)

    in_specs = [
        pl.BlockSpec((tm, d), lambda j, m: (m, 0)),
        pl.BlockSpec((tn, d), lambda j, m: (base // tn + j, 0)),
    ]
    args = [h, wt]
    if shift:
        in_specs.append(pl.BlockSpec((shift, d), lambda j, m: ((base + (j + 1) * tn) // shift, 0)))
        args.append(wt)
    out_specs = [pl.BlockSpec((tm, tn), lambda j, m: (m, j))]
    out_shapes = [jax.ShapeDtypeStruct((s, ncols), out_dtype)]
    scratch = [pltpu.VMEM((tn, d), BF16)]
    for sd in sides:
        in_specs += [flat(sp) for sp in sd.in_specs]
        args += list(sd.args)
    for sd in sides:
        out_specs += [flat(sp) for sp in sd.out_specs]
        out_shapes += list(sd.out_shapes)
    for sd in sides:
        scratch += list(sd.scratch)
    return pl.pallas_call(
        functools.partial(_proj_kernel, shift=shift, tn=tn, sides=tuple(sides)),
        grid=(ncols // tn, n_m),
        in_specs=in_specs,
        out_specs=out_specs,
        out_shape=out_shapes,
        scratch_shapes=scratch,
        compiler_params=_params(("arbitrary", "arbitrary")),
        name=name,
    )(*args)


def _cast_body(step, in_refs, out_refs, scratch_refs, emit, n_parts):
    del step, scratch_refs, emit, n_parts
    out_refs[0][...] = in_refs[0][...].astype(BF16)


def _cast_side(w, rows):
    n, d = w.shape
    nb = n // rows
    spec = _Spec((rows, d), lambda i: (jnp.minimum(i, nb - 1), 0))
    return _Side(body=_cast_body, init=None, args=(w,), in_specs=(spec,), out_specs=(spec,),
                 out_shapes=(jax.ShapeDtypeStruct((n, d), BF16),), scratch=(), steps=nb,
                 clamped=True)


MLSTM_BLOCK = 256


def _cumulative_lanes(v, op, identity):
    n = v.shape[-1]
    pos = lax.broadcasted_iota(jnp.int32, v.shape, v.ndim - 1)
    shift = 1
    while shift < n:
        v = op(v, jnp.where(pos >= shift, pltpu.roll(v, shift, v.ndim - 1), identity))
        shift *= 2
    return v


def _mlstm_body(step, in_refs, out_refs, scratch_refs, emit, n_parts):
    pa_ref, gt_ref, cw_ref, cb_ref, gbc_ref, mn_ref = in_refs
    v_off = 2 * M_QK
    o_off = 2 * M_QK + M_V
    (out_ref,) = out_refs
    qk_buf, c_s, m_s = scratch_refs
    del step
    halo = SUBLANES
    tc = MLSTM_BLOCK

    qk_buf[halo:halo + tc, :] = pa_ref[:, 0:2 * M_QK].astype(F32)
    cw_half = cw_ref[...] * 0.5
    t = jnp.broadcast_to(cb_ref[...] * 0.5, (tc, 2 * M_QK))
    for tap in range(M_CONV):
        lo = halo - (M_CONV - 1) + tap
        t = t + qk_buf[lo:lo + tc, :] * cw_half[tap:tap + 1, :]
    qk_buf[0:halo, :] = qk_buf[tc:tc + halo, :]
    qk = t * (1.0 + jnp.tanh(t))
    q_all = qk[:, 0:M_QK].astype(BF16)
    k_all = qk[:, M_QK:2 * M_QK] * (M_QK_DIM ** -0.5)

    zt = gt_ref[...] + gbc_ref[:, 0:1]
    lf_all = _log_sigmoid(zt)
    b_all = _cumulative_lanes(lf_all, jnp.add, 0.0)
    u_all = zt[0:M_HEADS, :] - b_all[M_HEADS:2 * M_HEADS, :]
    umax_all = _cumulative_lanes(u_all, jnp.maximum, NEG)
    cols_t = jnp.concatenate(
        [b_all, umax_all, jnp.zeros((LANES - SUBLANES - M_HEADS, tc), F32)], axis=0).T

    row = lax.broadcasted_iota(jnp.int32, (tc, tc), 0)
    col = lax.broadcasted_iota(jnp.int32, (tc, tc), 1)
    causal = col <= row
    ones_aug = jnp.ones((tc, LANES), BF16)

    def inter(h):
        c_prev = c_s[h]
        qc = _dot(q_all[:, h * M_QK_DIM:(h + 1) * M_QK_DIM], c_prev.astype(BF16))
        return c_prev, qc

    def first(h, c_prev, qc):
        qcols = slice(h * M_QK_DIM, (h + 1) * M_QK_DIM)
        f = M_HEADS + h
        q_h = q_all[:, qcols]
        kt = k_all[:, qcols].T
        m_prev = m_s[h][0:1, 0:1]
        u = u_all[h:h + 1, :]
        b_tot = b_all[f:f + 1, tc - 1:tc]
        v_aug = jnp.concatenate([pa_ref[:, v_off + h * M_V_DIM:v_off + (h + 1) * M_V_DIM],
                                 ones_aug], axis=1)
        qk = _dot(q_h, kt.astype(BF16))
        w_log = b_tot + u
        m_loc = b_tot + umax_all[h:h + 1, tc - 1:tc]
        wkt = (kt * jnp.exp(w_log - m_loc)).astype(BF16)
        m_new = jnp.maximum(b_tot + m_prev, m_loc)
        decay = jnp.exp(b_tot + m_prev - m_new)
        gain = jnp.exp(m_loc - m_new)
        m_s[h] = jnp.broadcast_to(m_new, (SUBLANES, LANES))
        return (u, m_prev, v_aug, qk, qc), (wkt, v_aug, c_prev, decay, gain)

    def third(h, wkt, v_aug, c_prev, decay, gain):
        c_s[h] = decay * c_prev + gain * _dot(wkt, v_aug)

    def second(h, u, m_prev, v_aug, qk, qc):
        vcols = slice(h * M_V_DIM, (h + 1) * M_V_DIM)
        f = M_HEADS + h
        um = jnp.where(causal, u, NEG)
        g = jnp.maximum(cols_t[:, SUBLANES + h:SUBLANES + h + 1], m_prev)
        b_col = cols_t[:, f:f + 1]
        p = jnp.exp(um - g) * qk
        na = _dot(p.astype(BF16), v_aug) + jnp.exp(m_prev - g) * qc
        num = na[:, 0:M_V_DIM]
        den = na[:, M_V_DIM:M_V_DIM + 1]
        hh = num / jnp.maximum(jnp.abs(den), jnp.exp(-(b_col + g)))
        o_gate = pa_ref[:, o_off + h * M_V_DIM:o_off + (h + 1) * M_V_DIM].astype(F32)
        hh = hh * _rms_scale(hh) * mn_ref[0:1, vcols] * _sigmoid(o_gate)
        out_ref[:, vcols] = hh.astype(BF16)

    emit(0)
    parts = [first(h, *inter(h)) for h in range(M_HEADS)]
    groups = n_parts - 1
    per = M_HEADS // groups
    for grp in range(groups):
        emit(1 + grp)
        for h in range(grp * per, (grp + 1) * per):
            second(h, *parts[h][0])
    for h in range(M_HEADS):
        third(h, *parts[h][1])
    assert groups * per == M_HEADS


def _mlstm_init(step, scratch_refs):
    qk_buf, c_s, m_s = scratch_refs

    @pl.when(step == 0)
    def _():
        qk_buf[0:SUBLANES, :] = jnp.zeros((SUBLANES, 2 * M_QK), F32)
        c_s[...] = jnp.zeros(c_s.shape, F32)
        m_s[...] = jnp.full(m_s.shape, NEG, F32)


def _mlstm_side(pa, gt, cw, cb, gbc, mn):
    s = pa.shape[0]
    tc = MLSTM_BLOCK
    const = lambda i: (0, 0)
    return _Side(
        body=_mlstm_body,
        init=_mlstm_init,
        args=(pa, gt, cw, cb, gbc, mn),
        in_specs=(
            _Spec((tc, pa.shape[1]), lambda i: (i, 0)),
            _Spec((SUBLANES, tc), lambda i: (0, i)),
            _Spec((M_CONV, 2 * M_QK), const),
            _Spec((1, 2 * M_QK), const),
            _Spec((SUBLANES, LANES), const),
            _Spec((1, M_V), const),
        ),
        out_specs=(_Spec((tc, M_V), lambda i: (i, 0)),),
        out_shapes=(jax.ShapeDtypeStruct((s, M_V), BF16),),
        scratch=(
            pltpu.VMEM((SUBLANES + tc, 2 * M_QK), F32),
            pltpu.VMEM((M_HEADS, M_QK_DIM, M_V_DIM + LANES), F32),
            pltpu.VMEM((M_HEADS, SUBLANES, LANES), F32),
        ),
        steps=s // tc,
        places=(MXU_WIDTH,) * 3,
    )


ATT_G = 4
ATT_ROWS = ATT_G * CHUNK
ATT_KBLOCKS = (A_PAST_CHUNKS + ATT_G) // ATT_G


def _attn_body(step, in_refs, out_refs, scratch_refs, emit, n_parts):
    del step, scratch_refs
    q_ref, k_ref, v_ref, bias_ref = in_refs
    (out_ref,) = out_refs
    scale2 = (A_HEAD_DIM ** -0.5) * LOG2E
    ones = jnp.ones((ATT_KBLOCKS * ATT_ROWS, A_HEAD_DIM), BF16)

    def scores(h):
        cols = slice(h * A_HEAD_DIM, (h + 1) * A_HEAD_DIM)
        return _dot_nt(q_ref[:, cols], k_ref[:, cols])

    def finish(h, s):
        cols = slice(h * A_HEAD_DIM, (h + 1) * A_HEAD_DIM)
        s = s * scale2 + bias_ref[h]
        p = jnp.exp2(s - jnp.max(s, axis=1, keepdims=True)).astype(BF16)
        o = _dot(p, jnp.concatenate([v_ref[:, cols], ones], axis=1))
        out_ref[:, cols] = (o[:, 0:A_HEAD_DIM] / o[:, A_HEAD_DIM:A_HEAD_DIM + 1]).astype(BF16)

    per = A_HEADS // n_parts
    s_next = [scores(h) for h in range(per)]
    for grp in range(n_parts):
        s_grp = s_next
        if grp + 1 < n_parts:
            s_next = [scores(h) for h in range((grp + 1) * per, (grp + 2) * per)]
        emit(grp)
        for h, s in zip(range(grp * per, (grp + 1) * per), s_grp):
            finish(h, s)


def _attn_side(pb, bias):
    s = pb.shape[0]
    last = ATT_KBLOCKS - 1

    def kv(group):
        return _Spec((pl.Element(ATT_KBLOCKS * ATT_ROWS), pl.Element(A_W)),
                     lambda g: (jnp.maximum(g - last, 0) * ATT_ROWS, group * A_W))

    return _Side(
        body=_attn_body,
        init=None,
        args=(pb,) * 3 + (bias,),
        in_specs=(
            _Spec((ATT_ROWS, A_W), lambda g: (g, 0)),
            kv(1), kv(2),
            _Spec((pl.Element(A_HEADS), pl.Element(ATT_ROWS), pl.Element(ATT_KBLOCKS * ATT_ROWS)),
                  lambda g: (0, 0, jnp.maximum(last - g, 0) * ATT_ROWS)),
        ),
        out_specs=(_Spec((ATT_ROWS, A_W), lambda g: (g, 0)),),
        out_shapes=(jax.ShapeDtypeStruct((s, A_W), BF16),),
        scratch=(),
        steps=s // ATT_ROWS,
        places=(MXU_WIDTH,) * 4,
    )


def _band_bias(rel_table):
    hds = rel_table.shape[0]
    wcols = ATT_KBLOCKS * ATT_ROWS
    band_w = (A_PAST_CHUNKS + 1) * CHUNK
    far = A_PAST_CHUNKS * CHUNK - A_MAX_REL + CHUNK
    tab = rel_table.astype(F32) * LOG2E
    e = jnp.concatenate([jnp.broadcast_to(tab[:, 2 * A_MAX_REL:], (hds, far)),
                         tab[:, A_MAX_REL - CHUNK + 1:2 * A_MAX_REL][:, ::-1],
                         jnp.zeros((hds, 1), F32)], axis=1)
    period = e.shape[1]
    tiled = jnp.broadcast_to(e[:, None, :], (hds, CHUNK, period)).reshape(hds, -1)
    skew = tiled[:, :CHUNK * (period - 1)].reshape(hds, CHUNK, period - 1)
    chunk_bias = skew[:, :, CHUNK - 1:CHUNK - 1 + band_w]
    total = wcols + (ATT_KBLOCKS - 1) * ATT_ROWS
    rows = [jnp.pad(chunk_bias, ((0, 0), (0, 0), (ci * CHUNK, total - band_w - ci * CHUNK)),
                    constant_values=NEG) for ci in range(ATT_G)]
    return jnp.concatenate(rows, axis=1)


def _merge_kernel(hm_ref, ha_ref, gm_ref, ga_ref, x_ref, wm_ref, wa_ref, wo_ref, gb_ref, nf_ref,
                  x1_ref, h2_ref):
    ym = _dot(hm_ref[...], wm_ref[...])
    ya = _dot(ha_ref[...], wa_ref[...])
    merged = (_sigmoid(gm_ref[...].astype(F32) + gb_ref[0:1, :]) * ym
              + _sigmoid(ga_ref[...].astype(F32) + gb_ref[1:2, :]) * ya)
    x1 = x_ref[...] + _dot(merged.astype(BF16), wo_ref[...])
    x1_ref[...] = x1
    h2_ref[...] = (x1 * _rms_scale(x1) * nf_ref[...]).astype(BF16)


def _merge(hm, ha, pc, x, wm, wa, wo, gb, nf, tm=256):
    s, d = x.shape
    const = lambda i: (0, 0)
    resident = pl.Buffered(1)
    return pl.pallas_call(
        _merge_kernel,
        grid=(s // tm,),
        in_specs=[
            pl.BlockSpec((tm, M_V), lambda i: (i, 0)),
            pl.BlockSpec((tm, A_W), lambda i: (i, 0)),
            pl.BlockSpec((tm, d), lambda i: (i, 0)),
            pl.BlockSpec((tm, d), lambda i: (i, 1)),
            pl.BlockSpec((tm, d), lambda i: (i, 0)),
            pl.BlockSpec((M_V, d), const, pipeline_mode=resident),
            pl.BlockSpec((A_W, d), const, pipeline_mode=resident),
            pl.BlockSpec((d, d), const, pipeline_mode=resident),
            pl.BlockSpec((2, d), const),
            pl.BlockSpec((1, d), const),
        ],
        out_specs=[
            pl.BlockSpec((tm, d), lambda i: (i, 0)),
            pl.BlockSpec((tm, d), lambda i: (i, 0)),
        ],
        out_shape=[
            jax.ShapeDtypeStruct((s, d), F32),
            jax.ShapeDtypeStruct((s, d), BF16),
        ],
        compiler_params=_params(("arbitrary",)),
        name="merge",
    )(hm, ha, pc, pc, x, wm, wa, wo, gb, nf)


FFN_SUB_ROWS = 512


def _ffn_up_kernel(h_ref, wg_ref, wv_ref, cw_ref, cb_ref, wd_ref, a_ref, wd_bf_ref, w_bf, ug_buf,
                   *, tm, tf):
    halo = SUBLANES
    rs = FFN_SUB_ROWS
    wd_bf_ref[...] = wd_ref[...].astype(BF16)

    @pl.when(pl.program_id(1) == 0)
    def _():
        w_bf[:, 0:tf] = wg_ref[...].astype(BF16)
        w_bf[:, tf:2 * tf] = wv_ref[...].astype(BF16)
        ug_buf[0:halo, :] = jnp.zeros((halo, tf), F32)

    for r in range(tm // rs):
        h = h_ref[r * rs:(r + 1) * rs, :]
        u = _dot(h, w_bf[...])
        ug = u[:, 0:tf]
        uv = u[:, tf:2 * tf]
        base = halo + r * rs
        ug_buf[base:base + rs, :] = ug
        conv = cb_ref[...] + ug * cw_ref[FFN_CONV - 1:FFN_CONV, :]
        for tap in range(FFN_CONV - 1):
            lo = base - (FFN_CONV - 1) + tap
            conv = conv + ug_buf[lo:lo + rs, :] * cw_ref[tap:tap + 1, :]
        a_ref[r * rs:(r + 1) * rs, :] = (conv * _sigmoid(conv) * uv).astype(BF16)
    ug_buf[0:halo, :] = ug_buf[tm:tm + halo, :]


def _ffn_up(h2, w_up, cw, cb, w_down, tm=2048, tf=512):
    s, d = h2.shape
    nf = D_FF // tf
    n_m = s // tm
    wd_rows = D_FF // (nf * n_m)
    assert wd_rows * nf * n_m == D_FF and wd_rows % (2 * SUBLANES) == 0
    wd_spec = pl.BlockSpec((wd_rows, d), lambda j, m: (j * n_m + m, 0))
    return pl.pallas_call(
        functools.partial(_ffn_up_kernel, tm=tm, tf=tf),
        grid=(nf, n_m),
        in_specs=[
            pl.BlockSpec((tm, d), lambda j, m: (m, 0)),
            pl.BlockSpec((d, tf), lambda j, m: (0, j)),
            pl.BlockSpec((d, tf), lambda j, m: (0, nf + j)),
            pl.BlockSpec((FFN_CONV, tf), lambda j, m: (0, j)),
            pl.BlockSpec((1, tf), lambda j, m: (0, j)),
            wd_spec,
        ],
        out_specs=[pl.BlockSpec((tm, tf), lambda j, m: (m, j)), wd_spec],
        out_shape=[jax.ShapeDtypeStruct((s, D_FF), BF16),
                   jax.ShapeDtypeStruct((D_FF, d), BF16)],
        scratch_shapes=[
            pltpu.VMEM((d, 2 * tf), BF16),
            pltpu.VMEM((SUBLANES + tm, tf), F32),
        ],
        compiler_params=_params(("arbitrary", "arbitrary")),
        name="ffn_up",
    )(h2, w_up, w_up, cw, cb, w_down)


FFN_DOWN_COLS = 512


def _ffn_down_kernel(a_ref, w_ref, x1_ref, nf_ref, o_ref):
    d = o_ref.shape[1]
    ssq = jnp.zeros((o_ref.shape[0], 1), F32)
    for n in range(d // FFN_DOWN_COLS):
        cols = slice(n * FFN_DOWN_COLS, (n + 1) * FFN_DOWN_COLS)
        y = x1_ref[:, cols] + _dot(a_ref[...], w_ref[:, cols])
        o_ref[:, cols] = y
        ssq = ssq + jnp.sum(y * y, axis=-1, keepdims=True)
    o_ref[...] = o_ref[...] * lax.rsqrt(ssq * (1.0 / d) + EPS) * nf_ref[...]


def _ffn_down(a, w_down, x1, nf, tm=512):
    s, d = x1.shape
    dff = a.shape[1]
    return pl.pallas_call(
        _ffn_down_kernel,
        grid=(s // tm,),
        in_specs=[
            pl.BlockSpec((tm, dff), lambda i: (i, 0)),
            pl.BlockSpec((dff, d), lambda i: (0, 0), pipeline_mode=pl.Buffered(1)),
            pl.BlockSpec((tm, d), lambda i: (i, 0)),
            pl.BlockSpec((1, d), lambda i: (0, 0)),
        ],
        out_specs=pl.BlockSpec((tm, d), lambda i: (i, 0)),
        out_shape=jax.ShapeDtypeStruct((s, d), F32),
        compiler_params=_params(("arbitrary",)),
        name="ffn_down",
    )(a, w_down, x1, nf)


def kernel(x, norm_mix, w_in, conv_qk_w, conv_qk_b, b_igate, b_fgate, m_norm, rel_bias, gate_bias,
           w_branch_m, w_branch_a, w_out, norm_ffn, w_up, conv_ffn_w, conv_ffn_b, w_down, norm_final):
    batch, seq, d = x.shape
    depth = w_in.shape[0]
    assert (batch, seq, d, depth) == (1, SEQ, D_MODEL, 1)
    xs = x[0]
    l = 0

    w_in_t = jnp.swapaxes(w_in[l], 0, 1)
    w_g = jnp.pad(w_in_t[GATE_COL0:GATE_COL0 + N_GATES, :].T, ((0, 0), (0, LANES - N_GATES)))
    gate_b = jnp.concatenate([b_igate[l], b_fgate[l]])
    gb_col = jnp.broadcast_to(gate_b[:, None], (SUBLANES, LANES))
    bias = _band_bias(rel_bias[l])

    h, gt = _prenorm(xs, norm_mix[l][None, :], w_g.astype(BF16))
    na = 2 * M_QK + 2 * M_V
    attn0 = GATE_COL0 + N_GATES
    pa, w_m_bf, w_a_bf = _proj(h, w_in_t, 0, na, BF16, "proj_mlstm",
                               sides=[_cast_side(w_branch_m[l], 128), _cast_side(w_branch_a[l], 128)])
    mlstm = _mlstm_side(pa, gt, conv_qk_w[l], conv_qk_b[l][None, :], gb_col, m_norm[l][None, :])
    pb, hm, w_out_bf = _proj(h, w_in_t, attn0, 3 * A_W, BF16, "proj_attn_mlstm", tn=768,
                             sides=[mlstm, _cast_side(w_out[l], 128)])
    pc, ha = _proj(h, w_in_t, attn0 + 3 * A_W, 2 * D_MODEL, BF16, "proj_gates_attn",
                   sides=[_attn_side(pb, bias)])

    x1, h2 = _merge(hm, ha, pc, xs, w_m_bf, w_a_bf, w_out_bf, gate_bias[l], norm_ffn[l][None, :])
    a, w_down_bf = _ffn_up(h2, w_up[l], conv_ffn_w[l], conv_ffn_b[l][None, :], w_down[l])
    out = _ffn_down(a, w_down_bf, x1, norm_final[None, :])
    return out[None]
```

```python
import functools
import math
from typing import Callable, NamedTuple

import jax
import jax.numpy as jnp
from jax import lax
from jax.experimental import pallas as pl
from jax.experimental.pallas import tpu as pltpu

D_MODEL = 2048
SEQ = 8192
CHUNK = 64
M_HEADS = 4
M_QK_DIM = 128
M_V_DIM = 256
M_CONV = 4
A_HEADS = 8
A_HEAD_DIM = 128
A_PAST_CHUNKS = 8
A_MAX_REL = 128
D_FF = 5632
FFN_CONV = 3
EPS = 1e-6
NEG = -1e30
LOG2E = math.log2(math.e)

M_QK = M_HEADS * M_QK_DIM
M_V = M_HEADS * M_V_DIM
A_W = A_HEADS * A_HEAD_DIM
GATE_COL0 = 2 * M_QK + 2 * M_V
N_GATES = 2 * M_HEADS

LANES = 128
SUBLANES = 8
MXU_WIDTH = 256
VMEM_LIMIT = 56 * 1024 * 1024

BF16 = jnp.bfloat16
F32 = jnp.float32


def _params(sem):
    return pltpu.CompilerParams(dimension_semantics=sem, vmem_limit_bytes=VMEM_LIMIT)


def _dot(a, b):
    return jnp.dot(a, b, preferred_element_type=F32)


def _dot_nt(a, b):
    return lax.dot_general(a, b, (((1,), (1,)), ((), ())), preferred_element_type=F32)


def _sigmoid(z):
    return 0.5 * jnp.tanh(0.5 * z) + 0.5


def _log_sigmoid(z):
    return jnp.minimum(z, 0.0) - jnp.log(1.0 + jnp.exp(-jnp.abs(z)))


def _rms_scale(y):
    return lax.rsqrt(jnp.mean(y * y, axis=-1, keepdims=True) + EPS)


def _prenorm_kernel(x_ref, g_ref, wg_ref, h_ref, gt_ref):
    xf = x_ref[...]
    hb = (xf * _rms_scale(xf) * g_ref[...]).astype(BF16)
    h_ref[...] = hb
    gcol = _dot(hb, wg_ref[...])
    gt_ref[...] = gcol.T[0:SUBLANES, :]


def _prenorm(x, g, wg, tm=1024):
    s, d = x.shape
    return pl.pallas_call(
        _prenorm_kernel,
        grid=(s // tm,),
        in_specs=[
            pl.BlockSpec((tm, d), lambda i: (i, 0)),
            pl.BlockSpec((1, d), lambda i: (0, 0)),
            pl.BlockSpec((d, LANES), lambda i: (0, 0)),
        ],
        out_specs=[
            pl.BlockSpec((tm, d), lambda i: (i, 0)),
            pl.BlockSpec((SUBLANES, tm), lambda i: (0, i)),
        ],
        out_shape=[
            jax.ShapeDtypeStruct((s, d), BF16),
            jax.ShapeDtypeStruct((SUBLANES, s), F32),
        ],
        compiler_params=_params(("arbitrary",)),
        name="prenorm",
    )(x, g, wg)


class _Spec(NamedTuple):
    block: tuple
    index: Callable
    mode: object = None


class _Side(NamedTuple):
    body: Callable
    init: Callable
    args: tuple
    in_specs: tuple
    out_specs: tuple
    out_shapes: tuple
    scratch: tuple
    steps: int
    clamped: bool = False
    places: tuple = ()


H_RING = 3


def _proj_kernel(*refs, shift, tn, tm, ring, sides):
    n_w = 3 if shift else 2
    h_ref, wm_ref = refs[0], refs[1]
    pos = n_w
    side_in = []
    for sd in sides:
        side_in.append(refs[pos:pos + len(sd.args)])
        pos += len(sd.args)
    o_ref = refs[pos]
    pos += 1
    side_out = []
    for sd in sides:
        side_out.append(refs[pos:pos + len(sd.out_specs)])
        pos += len(sd.out_specs)
    w_bf = refs[pos]
    pos += 1
    if ring:
        h_ring, h_sem = refs[pos], refs[pos + 1]
        pos += 2
    side_scratch = []
    for sd in sides:
        side_scratch.append(refs[pos:pos + len(sd.scratch)])
        pos += len(sd.scratch)

    @pl.when(pl.program_id(1) == 0)
    def _():
        if shift:
            w_bf[0:tn - shift, :] = wm_ref[shift:tn, :].astype(BF16)
            w_bf[tn - shift:tn, :] = refs[2][...].astype(BF16)
        else:
            w_bf[...] = wm_ref[...].astype(BF16)

    n_m = pl.num_programs(1)
    step = pl.program_id(0) * n_m + pl.program_id(1)
    if ring:
        n_steps = pl.num_programs(0) * n_m

        def tile_copy(s):
            rows = pl.ds(pl.multiple_of(lax.rem(s, n_m) * tm, tm), tm)
            slot = lax.rem(s, H_RING)
            return pltpu.make_async_copy(h_ref.at[rows, :], h_ring.at[slot], h_sem.at[slot])

        @pl.when(step == 0)
        def _():
            tile_copy(0).start()
            tile_copy(1).start()

        @pl.when(step + 2 < n_steps)
        def _():
            tile_copy(step + 2).start()

        tile_copy(step).wait()
        h_ref = h_ring.at[lax.rem(step, H_RING)]
    for sd, scr in zip(sides, side_scratch):
        if sd.init:
            sd.init(step, scr)

    placers = [idx for idx, sd in enumerate(sides) if sd.places]
    assert len(placers) <= 1
    widths = sides[placers[0]].places if placers else (tn,)
    n_parts = len(widths)
    assert sum(widths) == tn and all(w % MXU_WIDTH == 0 for w in widths)

    def emit(k):
        lo = sum(widths[:k])
        cols = slice(lo, lo + widths[k])
        o_ref[:, cols] = _dot_nt(h_ref[...], w_bf[cols, :]).astype(o_ref.dtype)

    for idx, sd in enumerate(sides):
        if not sd.places:
            sd.body(step, side_in[idx], side_out[idx], side_scratch[idx], None, 0)
    if placers:
        idx = placers[0]
        sides[idx].body(step, side_in[idx], side_out[idx], side_scratch[idx], emit, n_parts)
    else:
        emit(0)


def _proj(h, wt, col0, ncols, out_dtype, name, tm=1024, tn=1024, sides=(), ring=False):
    s, d = h.shape
    shift = col0 % tn
    base = col0 - shift
    n_m = s // tm
    n_steps = (ncols // tn) * n_m
    assert shift % SUBLANES == 0 and ncols % tn == 0
    assert all(sd.steps == n_steps or (sd.clamped and sd.steps < n_steps) for sd in sides)
    assert not ring or n_steps >= 2

    def flat(spec):
        return pl.BlockSpec(spec.block, lambda j, m: spec.index(j * n_m + m),
                            pipeline_mode=spec.mode)

    in_specs = [
        pl.BlockSpec(memory_space=pl.ANY) if ring else pl.BlockSpec((tm, d), lambda j, m: (m, 0)),
        pl.BlockSpec((tn, d), lambda j, m: (base // tn + j, 0)),
    ]
    args = [h, wt]
    if shift:
        in_specs.append(pl.BlockSpec((shift, d), lambda j, m: ((base + (j + 1) * tn) // shift, 0)))
        args.append(wt)
    out_specs = [pl.BlockSpec((tm, tn), lambda j, m: (m, j))]
    out_shapes = [jax.ShapeDtypeStruct((s, ncols), out_dtype)]
    scratch = [pltpu.VMEM((tn, d), BF16)]
    if ring:
        scratch += [pltpu.VMEM((H_RING, tm, d), BF16), pltpu.SemaphoreType.DMA((H_RING,))]
    for sd in sides:
        in_specs += [flat(sp) for sp in sd.in_specs]
        args += list(sd.args)
    for sd in sides:
        out_specs += [flat(sp) for sp in sd.out_specs]
        out_shapes += list(sd.out_shapes)
    for sd in sides:
        scratch += list(sd.scratch)
    return pl.pallas_call(
        functools.partial(_proj_kernel, shift=shift, tn=tn, tm=tm, ring=ring, sides=tuple(sides)),
        grid=(ncols // tn, n_m),
        in_specs=in_specs,
        out_specs=out_specs,
        out_shape=out_shapes,
        scratch_shapes=scratch,
        compiler_params=_params(("arbitrary", "arbitrary")),
        name=name,
    )(*args)


def _cast_body(step, in_refs, out_refs, scratch_refs, emit, n_parts):
    del step, scratch_refs, emit, n_parts
    out_refs[0][...] = in_refs[0][...].astype(BF16)


def _cast_side(w, rows):
    n, d = w.shape
    nb = n // rows
    spec = _Spec((rows, d), lambda i: (jnp.minimum(i, nb - 1), 0))
    return _Side(body=_cast_body, init=None, args=(w,), in_specs=(spec,), out_specs=(spec,),
                 out_shapes=(jax.ShapeDtypeStruct((n, d), BF16),), scratch=(), steps=nb,
                 clamped=True)


MLSTM_BLOCK = 256


def _cumulative_lanes(v, op, identity):
    n = v.shape[-1]
    pos = lax.broadcasted_iota(jnp.int32, v.shape, v.ndim - 1)
    shift = 1
    while shift < n:
        v = op(v, jnp.where(pos >= shift, pltpu.roll(v, shift, v.ndim - 1), identity))
        shift *= 2
    return v


def _mlstm_body(step, in_refs, out_refs, scratch_refs, emit, n_parts):
    pa_ref, gt_ref, cw_ref, cb_ref, gbc_ref, mn_ref = in_refs
    v_off = 2 * M_QK
    o_off = 2 * M_QK + M_V
    (out_ref,) = out_refs
    qk_buf, c_s, m_s = scratch_refs
    del step
    halo = SUBLANES
    tc = MLSTM_BLOCK

    qk_buf[halo:halo + tc, :] = pa_ref[:, 0:2 * M_QK].astype(F32)
    cw_half = cw_ref[...] * 0.5
    t = jnp.broadcast_to(cb_ref[...] * 0.5, (tc, 2 * M_QK))
    for tap in range(M_CONV):
        lo = halo - (M_CONV - 1) + tap
        t = t + qk_buf[lo:lo + tc, :] * cw_half[tap:tap + 1, :]
    qk_buf[0:halo, :] = qk_buf[tc:tc + halo, :]
    qk = t * (1.0 + jnp.tanh(t))
    q_all = qk[:, 0:M_QK].astype(BF16)
    k_all = qk[:, M_QK:2 * M_QK] * (M_QK_DIM ** -0.5)

    zt = gt_ref[...] + gbc_ref[:, 0:1]
    lf_all = _log_sigmoid(zt)
    b_all = _cumulative_lanes(lf_all, jnp.add, 0.0)
    u_all = zt[0:M_HEADS, :] - b_all[M_HEADS:2 * M_HEADS, :]
    umax_all = _cumulative_lanes(u_all, jnp.maximum, NEG)
    cols_t = jnp.concatenate(
        [b_all, umax_all, jnp.zeros((LANES - SUBLANES - M_HEADS, tc), F32)], axis=0).T

    row = lax.broadcasted_iota(jnp.int32, (tc, tc), 0)
    col = lax.broadcasted_iota(jnp.int32, (tc, tc), 1)
    causal = col <= row
    ones_aug = jnp.ones((tc, LANES), BF16)

    def inter(h):
        c_prev = c_s[h]
        qc = _dot(q_all[:, h * M_QK_DIM:(h + 1) * M_QK_DIM], c_prev.astype(BF16))
        return c_prev, qc

    def first(h, c_prev, qc):
        qcols = slice(h * M_QK_DIM, (h + 1) * M_QK_DIM)
        f = M_HEADS + h
        q_h = q_all[:, qcols]
        kt = k_all[:, qcols].T
        m_prev = m_s[h][0:1, 0:1]
        u = u_all[h:h + 1, :]
        b_tot = b_all[f:f + 1, tc - 1:tc]
        v_aug = jnp.concatenate([pa_ref[:, v_off + h * M_V_DIM:v_off + (h + 1) * M_V_DIM],
                                 ones_aug], axis=1)
        qk = _dot(q_h, kt.astype(BF16))
        w_log = b_tot + u
        m_loc = b_tot + umax_all[h:h + 1, tc - 1:tc]
        wkt = (kt * jnp.exp(w_log - m_loc)).astype(BF16)
        m_new = jnp.maximum(b_tot + m_prev, m_loc)
        decay = jnp.exp(b_tot + m_prev - m_new)
        gain = jnp.exp(m_loc - m_new)
        m_s[h] = jnp.broadcast_to(m_new, (SUBLANES, LANES))
        return (u, m_prev, v_aug, qk, qc), (wkt, v_aug, c_prev, decay, gain)

    def third(h, wkt, v_aug, c_prev, decay, gain):
        c_s[h] = decay * c_prev + gain * _dot(wkt, v_aug)

    def second(h, u, m_prev, v_aug, qk, qc):
        vcols = slice(h * M_V_DIM, (h + 1) * M_V_DIM)
        f = M_HEADS + h
        um = jnp.where(causal, u, NEG)
        g = jnp.maximum(cols_t[:, SUBLANES + h:SUBLANES + h + 1], m_prev)
        b_col = cols_t[:, f:f + 1]
        p = jnp.exp(um - g) * qk
        na = _dot(p.astype(BF16), v_aug) + jnp.exp(m_prev - g) * qc
        num = na[:, 0:M_V_DIM]
        den = na[:, M_V_DIM:M_V_DIM + 1]
        hh = num / jnp.maximum(jnp.abs(den), jnp.exp(-(b_col + g)))
        o_gate = pa_ref[:, o_off + h * M_V_DIM:o_off + (h + 1) * M_V_DIM].astype(F32)
        hh = hh * _rms_scale(hh) * mn_ref[0:1, vcols] * _sigmoid(o_gate)
        out_ref[:, vcols] = hh.astype(BF16)

    emit(0)
    parts = [first(h, *inter(h)) for h in range(M_HEADS)]
    groups = n_parts - 1
    per = M_HEADS // groups
    for grp in range(groups):
        emit(1 + grp)
        for h in range(grp * per, (grp + 1) * per):
            second(h, *parts[h][0])
    for h in range(M_HEADS):
        third(h, *parts[h][1])
    assert groups * per == M_HEADS


def _mlstm_init(step, scratch_refs):
    qk_buf, c_s, m_s = scratch_refs

    @pl.when(step == 0)
    def _():
        qk_buf[0:SUBLANES, :] = jnp.zeros((SUBLANES, 2 * M_QK), F32)
        c_s[...] = jnp.zeros(c_s.shape, F32)
        m_s[...] = jnp.full(m_s.shape, NEG, F32)


def _mlstm_side(pa, gt, cw, cb, gbc, mn):
    s = pa.shape[0]
    tc = MLSTM_BLOCK
    const = lambda i: (0, 0)
    return _Side(
        body=_mlstm_body,
        init=_mlstm_init,
        args=(pa, gt, cw, cb, gbc, mn),
        in_specs=(
            _Spec((tc, pa.shape[1]), lambda i: (i, 0)),
            _Spec((SUBLANES, tc), lambda i: (0, i)),
            _Spec((M_CONV, 2 * M_QK), const),
            _Spec((1, 2 * M_QK), const),
            _Spec((SUBLANES, LANES), const),
            _Spec((1, M_V), const),
        ),
        out_specs=(_Spec((tc, M_V), lambda i: (i, 0)),),
        out_shapes=(jax.ShapeDtypeStruct((s, M_V), BF16),),
        scratch=(
            pltpu.VMEM((SUBLANES + tc, 2 * M_QK), F32),
            pltpu.VMEM((M_HEADS, M_QK_DIM, M_V_DIM + LANES), F32),
            pltpu.VMEM((M_HEADS, SUBLANES, LANES), F32),
        ),
        steps=s // tc,
        places=(MXU_WIDTH,) * 3,
    )


ATT_G = 4
ATT_ROWS = ATT_G * CHUNK
ATT_KBLOCKS = (A_PAST_CHUNKS + ATT_G) // ATT_G


def _attn_body(step, in_refs, out_refs, scratch_refs, emit, n_parts):
    del step, scratch_refs
    q_ref, k_ref, v_ref, bias_ref = in_refs
    (out_ref,) = out_refs
    scale2 = (A_HEAD_DIM ** -0.5) * LOG2E
    ones = jnp.ones((ATT_KBLOCKS * ATT_ROWS, A_HEAD_DIM), BF16)

    def scores(h):
        cols = slice(h * A_HEAD_DIM, (h + 1) * A_HEAD_DIM)
        return _dot_nt(q_ref[:, cols], k_ref[:, cols])

    def finish(h, s):
        cols = slice(h * A_HEAD_DIM, (h + 1) * A_HEAD_DIM)
        s = s * scale2 + bias_ref[h]
        p = jnp.exp2(s - jnp.max(s, axis=1, keepdims=True)).astype(BF16)
        o = _dot(p, jnp.concatenate([v_ref[:, cols], ones], axis=1))
        out_ref[:, cols] = (o[:, 0:A_HEAD_DIM] / o[:, A_HEAD_DIM:A_HEAD_DIM + 1]).astype(BF16)

    per = A_HEADS // n_parts
    s_next = [scores(h) for h in range(per)]
    for grp in range(n_parts):
        s_grp = s_next
        if grp + 1 < n_parts:
            s_next = [scores(h) for h in range((grp + 1) * per, (grp + 2) * per)]
        emit(grp)
        for h, s in zip(range(grp * per, (grp + 1) * per), s_grp):
            finish(h, s)


def _attn_side(pb, bias):
    s = pb.shape[0]
    last = ATT_KBLOCKS - 1

    def kv(group):
        return _Spec((pl.Element(ATT_KBLOCKS * ATT_ROWS), pl.Element(A_W)),
                     lambda g: (jnp.maximum(g - last, 0) * ATT_ROWS, group * A_W))

    return _Side(
        body=_attn_body,
        init=None,
        args=(pb,) * 3 + (bias,),
        in_specs=(
            _Spec((ATT_ROWS, A_W), lambda g: (g, 0)),
            kv(1), kv(2),
            _Spec((pl.Element(A_HEADS), pl.Element(ATT_ROWS), pl.Element(ATT_KBLOCKS * ATT_ROWS)),
                  lambda g: (0, 0, jnp.maximum(last - g, 0) * ATT_ROWS)),
        ),
        out_specs=(_Spec((ATT_ROWS, A_W), lambda g: (g, 0)),),
        out_shapes=(jax.ShapeDtypeStruct((s, A_W), BF16),),
        scratch=(),
        steps=s // ATT_ROWS,
        places=(MXU_WIDTH,) * 4,
    )


def _band_bias(rel_table):
    hds = rel_table.shape[0]
    wcols = ATT_KBLOCKS * ATT_ROWS
    band_w = (A_PAST_CHUNKS + 1) * CHUNK
    far = A_PAST_CHUNKS * CHUNK - A_MAX_REL + CHUNK
    tab = rel_table.astype(F32) * LOG2E
    e = jnp.concatenate([jnp.broadcast_to(tab[:, 2 * A_MAX_REL:], (hds, far)),
                         tab[:, A_MAX_REL - CHUNK + 1:2 * A_MAX_REL][:, ::-1],
                         jnp.zeros((hds, 1), F32)], axis=1)
    period = e.shape[1]
    tiled = jnp.broadcast_to(e[:, None, :], (hds, CHUNK, period)).reshape(hds, -1)
    skew = tiled[:, :CHUNK * (period - 1)].reshape(hds, CHUNK, period - 1)
    chunk_bias = skew[:, :, CHUNK - 1:CHUNK - 1 + band_w]
    total = wcols + (ATT_KBLOCKS - 1) * ATT_ROWS
    rows = [jnp.pad(chunk_bias, ((0, 0), (0, 0), (ci * CHUNK, total - band_w - ci * CHUNK)),
                    constant_values=NEG) for ci in range(ATT_G)]
    return jnp.concatenate(rows, axis=1)


def _merge_kernel(hm_ref, ha_ref, gm_ref, ga_ref, x_ref, wm_ref, wa_ref, wo_ref, gb_ref, nf_ref,
                  x1_ref, h2_ref):
    ym = _dot(hm_ref[...], wm_ref[...])
    ya = _dot(ha_ref[...], wa_ref[...])
    merged = (_sigmoid(gm_ref[...].astype(F32) + gb_ref[0:1, :]) * ym
              + _sigmoid(ga_ref[...].astype(F32) + gb_ref[1:2, :]) * ya)
    x1 = x_ref[...] + _dot(merged.astype(BF16), wo_ref[...])
    x1_ref[...] = x1
    h2_ref[...] = (x1 * _rms_scale(x1) * nf_ref[...]).astype(BF16)


def _merge(hm, ha, pc, x, wm, wa, wo, gb, nf, tm=256):
    s, d = x.shape
    const = lambda i: (0, 0)
    resident = pl.Buffered(1)
    return pl.pallas_call(
        _merge_kernel,
        grid=(s // tm,),
        in_specs=[
            pl.BlockSpec((tm, M_V), lambda i: (i, 0)),
            pl.BlockSpec((tm, A_W), lambda i: (i, 0)),
            pl.BlockSpec((tm, d), lambda i: (i, 0)),
            pl.BlockSpec((tm, d), lambda i: (i, 1)),
            pl.BlockSpec((tm, d), lambda i: (i, 0)),
            pl.BlockSpec((M_V, d), const, pipeline_mode=resident),
            pl.BlockSpec((A_W, d), const, pipeline_mode=resident),
            pl.BlockSpec((d, d), const, pipeline_mode=resident),
            pl.BlockSpec((2, d), const),
            pl.BlockSpec((1, d), const),
        ],
        out_specs=[
            pl.BlockSpec((tm, d), lambda i: (i, 0)),
            pl.BlockSpec((tm, d), lambda i: (i, 0)),
        ],
        out_shape=[
            jax.ShapeDtypeStruct((s, d), F32),
            jax.ShapeDtypeStruct((s, d), BF16),
        ],
        compiler_params=_params(("arbitrary",)),
        name="merge",
    )(hm, ha, pc, pc, x, wm, wa, wo, gb, nf)


FFN_SUB_ROWS = 512


def _ffn_up_kernel(h_ref, wg_ref, wv_ref, cw_ref, cb_ref, wd_ref, a_ref, wd_bf_ref, w_bf, ug_buf,
                   *, tm, tf):
    halo = SUBLANES
    rs = FFN_SUB_ROWS
    wd_bf_ref[...] = wd_ref[...].astype(BF16)

    @pl.when(pl.program_id(1) == 0)
    def _():
        w_bf[:, 0:tf] = wg_ref[...].astype(BF16)
        w_bf[:, tf:2 * tf] = wv_ref[...].astype(BF16)
        ug_buf[0:halo, :] = jnp.zeros((halo, tf), F32)

    for r in range(tm // rs):
        h = h_ref[r * rs:(r + 1) * rs, :]
        u = _dot(h, w_bf[...])
        ug = u[:, 0:tf]
        uv = u[:, tf:2 * tf]
        base = halo + r * rs
        ug_buf[base:base + rs, :] = ug
        conv = cb_ref[...] + ug * cw_ref[FFN_CONV - 1:FFN_CONV, :]
        for tap in range(FFN_CONV - 1):
            lo = base - (FFN_CONV - 1) + tap
            conv = conv + ug_buf[lo:lo + rs, :] * cw_ref[tap:tap + 1, :]
        a_ref[r * rs:(r + 1) * rs, :] = (conv * _sigmoid(conv) * uv).astype(BF16)
    ug_buf[0:halo, :] = ug_buf[tm:tm + halo, :]


def _ffn_up(h2, w_up, cw, cb, w_down, tm=2048, tf=512):
    s, d = h2.shape
    nf = D_FF // tf
    n_m = s // tm
    wd_rows = D_FF // (nf * n_m)
    assert wd_rows * nf * n_m == D_FF and wd_rows % (2 * SUBLANES) == 0
    wd_spec = pl.BlockSpec((wd_rows, d), lambda j, m: (j * n_m + m, 0))
    return pl.pallas_call(
        functools.partial(_ffn_up_kernel, tm=tm, tf=tf),
        grid=(nf, n_m),
        in_specs=[
            pl.BlockSpec((tm, d), lambda j, m: (m, 0)),
            pl.BlockSpec((d, tf), lambda j, m: (0, j)),
            pl.BlockSpec((d, tf), lambda j, m: (0, nf + j)),
            pl.BlockSpec((FFN_CONV, tf), lambda j, m: (0, j)),
            pl.BlockSpec((1, tf), lambda j, m: (0, j)),
            wd_spec,
        ],
        out_specs=[pl.BlockSpec((tm, tf), lambda j, m: (m, j)), wd_spec],
        out_shape=[jax.ShapeDtypeStruct((s, D_FF), BF16),
                   jax.ShapeDtypeStruct((D_FF, d), BF16)],
        scratch_shapes=[
            pltpu.VMEM((d, 2 * tf), BF16),
            pltpu.VMEM((SUBLANES + tm, tf), F32),
        ],
        compiler_params=_params(("arbitrary", "arbitrary")),
        name="ffn_up",
    )(h2, w_up, w_up, cw, cb, w_down)


FFN_DOWN_COLS = 512


def _ffn_down_kernel(a_ref, w_ref, x1_ref, nf_ref, o_ref):
    d = o_ref.shape[1]
    ssq = jnp.zeros((o_ref.shape[0], 1), F32)
    for n in range(d // FFN_DOWN_COLS):
        cols = slice(n * FFN_DOWN_COLS, (n + 1) * FFN_DOWN_COLS)
        y = x1_ref[:, cols] + _dot(a_ref[...], w_ref[:, cols])
        o_ref[:, cols] = y
        ssq = ssq + jnp.sum(y * y, axis=-1, keepdims=True)
    o_ref[...] = o_ref[...] * lax.rsqrt(ssq * (1.0 / d) + EPS) * nf_ref[...]


def _ffn_down(a, w_down, x1, nf, tm=512):
    s, d = x1.shape
    dff = a.shape[1]
    return pl.pallas_call(
        _ffn_down_kernel,
        grid=(s // tm,),
        in_specs=[
            pl.BlockSpec((tm, dff), lambda i: (i, 0)),
            pl.BlockSpec((dff, d), lambda i: (0, 0), pipeline_mode=pl.Buffered(1)),
            pl.BlockSpec((tm, d), lambda i: (i, 0)),
            pl.BlockSpec((1, d), lambda i: (0, 0)),
        ],
        out_specs=pl.BlockSpec((tm, d), lambda i: (i, 0)),
        out_shape=jax.ShapeDtypeStruct((s, d), F32),
        compiler_params=_params(("arbitrary",)),
        name="ffn_down",
    )(a, w_down, x1, nf)


def kernel(x, norm_mix, w_in, conv_qk_w, conv_qk_b, b_igate, b_fgate, m_norm, rel_bias, gate_bias,
           w_branch_m, w_branch_a, w_out, norm_ffn, w_up, conv_ffn_w, conv_ffn_b, w_down, norm_final):
    batch, seq, d = x.shape
    depth = w_in.shape[0]
    assert (batch, seq, d, depth) == (1, SEQ, D_MODEL, 1)
    xs = x[0]
    l = 0

    w_in_t = jnp.swapaxes(w_in[l], 0, 1)
    w_g = jnp.pad(w_in_t[GATE_COL0:GATE_COL0 + N_GATES, :].T, ((0, 0), (0, LANES - N_GATES)))
    gate_b = jnp.concatenate([b_igate[l], b_fgate[l]])
    gb_col = jnp.broadcast_to(gate_b[:, None], (SUBLANES, LANES))
    bias = _band_bias(rel_bias[l])

    h, gt = _prenorm(xs, norm_mix[l][None, :], w_g.astype(BF16))
    na = 2 * M_QK + 2 * M_V
    attn0 = GATE_COL0 + N_GATES
    pa, w_m_bf, w_a_bf = _proj(h, w_in_t, 0, na, BF16, "proj_mlstm", ring=True,
                               sides=[_cast_side(w_branch_m[l], 128), _cast_side(w_branch_a[l], 128)])
    mlstm = _mlstm_side(pa, gt, conv_qk_w[l], conv_qk_b[l][None, :], gb_col, m_norm[l][None, :])
    pb, hm, w_out_bf = _proj(h, w_in_t, attn0, 3 * A_W, BF16, "proj_attn_mlstm", tn=768,
                             sides=[mlstm, _cast_side(w_out[l], 128)])
    pc, ha = _proj(h, w_in_t, attn0 + 3 * A_W, 2 * D_MODEL, BF16, "proj_gates_attn",
                   sides=[_attn_side(pb, bias)])

    x1, h2 = _merge(hm, ha, pc, xs, w_m_bf, w_a_bf, w_out_bf, gate_bias[l], norm_ffn[l][None, :])
    a, w_down_bf = _ffn_up(h2, w_up[l], conv_ffn_w[l], conv_ffn_b[l][None, :], w_down[l])
    out = _ffn_down(a, w_down_bf, x1, norm_final[None, :])
    return out[None]
```

```python
import functools
import math
from typing import Callable, NamedTuple

import jax
import jax.numpy as jnp
from jax import lax
from jax.experimental import pallas as pl
from jax.experimental.pallas import tpu as pltpu

D_MODEL = 2048
SEQ = 8192
CHUNK = 64
M_HEADS = 4
M_QK_DIM = 128
M_V_DIM = 256
M_CONV = 4
A_HEADS = 8
A_HEAD_DIM = 128
A_PAST_CHUNKS = 8
A_MAX_REL = 128
D_FF = 5632
FFN_CONV = 3
EPS = 1e-6
NEG = -1e30
LOG2E = math.log2(math.e)

M_QK = M_HEADS * M_QK_DIM
M_V = M_HEADS * M_V_DIM
A_W = A_HEADS * A_HEAD_DIM
GATE_COL0 = 2 * M_QK + 2 * M_V
N_GATES = 2 * M_HEADS

LANES = 128
SUBLANES = 8
MXU_WIDTH = 256
VMEM_LIMIT = 56 * 1024 * 1024

BF16 = jnp.bfloat16
F32 = jnp.float32


def _params(sem):
    return pltpu.CompilerParams(dimension_semantics=sem, vmem_limit_bytes=VMEM_LIMIT)


def _dot(a, b):
    return jnp.dot(a, b, preferred_element_type=F32)


def _dot_nt(a, b):
    return lax.dot_general(a, b, (((1,), (1,)), ((), ())), preferred_element_type=F32)


def _sigmoid(z):
    return 0.5 * jnp.tanh(0.5 * z) + 0.5


def _log_sigmoid(z):
    return jnp.minimum(z, 0.0) - jnp.log(1.0 + jnp.exp(-jnp.abs(z)))


def _rms_scale(y):
    return lax.rsqrt(jnp.mean(y * y, axis=-1, keepdims=True) + EPS)


def _prenorm_kernel(x_ref, g_ref, wg_ref, h_ref, gt_ref):
    xf = x_ref[...]
    hb = (xf * _rms_scale(xf) * g_ref[...]).astype(BF16)
    h_ref[...] = hb
    gcol = _dot(hb, wg_ref[...])
    gt_ref[...] = gcol.T[0:SUBLANES, :]


def _prenorm(x, g, wg, tm=1024):
    s, d = x.shape
    return pl.pallas_call(
        _prenorm_kernel,
        grid=(s // tm,),
        in_specs=[
            pl.BlockSpec((tm, d), lambda i: (i, 0)),
            pl.BlockSpec((1, d), lambda i: (0, 0)),
            pl.BlockSpec((d, LANES), lambda i: (0, 0)),
        ],
        out_specs=[
            pl.BlockSpec((tm, d), lambda i: (i, 0)),
            pl.BlockSpec((SUBLANES, tm), lambda i: (0, i)),
        ],
        out_shape=[
            jax.ShapeDtypeStruct((s, d), BF16),
            jax.ShapeDtypeStruct((SUBLANES, s), F32),
        ],
        compiler_params=_params(("arbitrary",)),
        name="prenorm",
    )(x, g, wg)


class _Spec(NamedTuple):
    block: tuple
    index: Callable
    mode: object = None


class _Side(NamedTuple):
    body: Callable
    init: Callable
    args: tuple
    in_specs: tuple
    out_specs: tuple
    out_shapes: tuple
    scratch: tuple
    steps: int
    clamped: bool = False
    places: tuple = ()


def _proj_kernel(*refs, shift, tn, sides):
    n_w = 3 if shift else 2
    h_ref, wm_ref = refs[0], refs[1]
    pos = n_w
    side_in = []
    for sd in sides:
        side_in.append(refs[pos:pos + len(sd.args)])
        pos += len(sd.args)
    o_ref = refs[pos]
    pos += 1
    side_out = []
    for sd in sides:
        side_out.append(refs[pos:pos + len(sd.out_specs)])
        pos += len(sd.out_specs)
    w_bf = refs[pos]
    pos += 1
    side_scratch = []
    for sd in sides:
        side_scratch.append(refs[pos:pos + len(sd.scratch)])
        pos += len(sd.scratch)

    @pl.when(pl.program_id(1) == 0)
    def _():
        if shift:
            w_bf[0:tn - shift, :] = wm_ref[shift:tn, :].astype(BF16)
            w_bf[tn - shift:tn, :] = refs[2][...].astype(BF16)
        else:
            w_bf[...] = wm_ref[...].astype(BF16)

    step = pl.program_id(0) * pl.num_programs(1) + pl.program_id(1)
    for sd, scr in zip(sides, side_scratch):
        if sd.init:
            sd.init(step, scr)

    placers = [idx for idx, sd in enumerate(sides) if sd.places]
    assert len(placers) <= 1
    widths = sides[placers[0]].places if placers else (tn,)
    n_parts = len(widths)
    assert sum(widths) == tn and all(w % MXU_WIDTH == 0 for w in widths)

    def emit(k):
        lo = sum(widths[:k])
        cols = slice(lo, lo + widths[k])
        o_ref[:, cols] = _dot_nt(h_ref[...], w_bf[cols, :]).astype(o_ref.dtype)

    for idx, sd in enumerate(sides):
        if not sd.places:
            sd.body(step, side_in[idx], side_out[idx], side_scratch[idx], None, 0)
    if placers:
        idx = placers[0]
        sides[idx].body(step, side_in[idx], side_out[idx], side_scratch[idx], emit, n_parts)
    else:
        emit(0)


def _proj(h, wt, col0, ncols, out_dtype, name, tm=1024, tn=1024, sides=()):
    s, d = h.shape
    shift = col0 % tn
    base = col0 - shift
    n_m = s // tm
    n_steps = (ncols // tn) * n_m
    assert shift % SUBLANES == 0 and ncols % tn == 0
    assert all(sd.steps == n_steps or (sd.clamped and sd.steps < n_steps) for sd in sides)

    def flat(spec):
        return pl.BlockSpec(spec.block, lambda j, m: spec.index(j * n_m + m),
                            pipeline_mode=spec.mode)

    in_specs = [
        pl.BlockSpec((tm, d), lambda j, m: (m, 0)),
        pl.BlockSpec((tn, d), lambda j, m: (base // tn + j, 0)),
    ]
    args = [h, wt]
    if shift:
        in_specs.append(pl.BlockSpec((shift, d), lambda j, m: ((base + (j + 1) * tn) // shift, 0)))
        args.append(wt)
    out_specs = [pl.BlockSpec((tm, tn), lambda j, m: (m, j))]
    out_shapes = [jax.ShapeDtypeStruct((s, ncols), out_dtype)]
    scratch = [pltpu.VMEM((tn, d), BF16)]
    for sd in sides:
        in_specs += [flat(sp) for sp in sd.in_specs]
        args += list(sd.args)
    for sd in sides:
        out_specs += [flat(sp) for sp in sd.out_specs]
        out_shapes += list(sd.out_shapes)
    for sd in sides:
        scratch += list(sd.scratch)
    return pl.pallas_call(
        functools.partial(_proj_kernel, shift=shift, tn=tn, sides=tuple(sides)),
        grid=(ncols // tn, n_m),
        in_specs=in_specs,
        out_specs=out_specs,
        out_shape=out_shapes,
        scratch_shapes=scratch,
        compiler_params=_params(("arbitrary", "arbitrary")),
        name=name,
    )(*args)


def _cast_body(step, in_refs, out_refs, scratch_refs, emit, n_parts):
    del step, scratch_refs, emit, n_parts
    out_refs[0][...] = in_refs[0][...].astype(BF16)


def _cast_side(w, rows):
    n, d = w.shape
    nb = n // rows
    spec = _Spec((rows, d), lambda i: (jnp.minimum(i, nb - 1), 0))
    return _Side(body=_cast_body, init=None, args=(w,), in_specs=(spec,), out_specs=(spec,),
                 out_shapes=(jax.ShapeDtypeStruct((n, d), BF16),), scratch=(), steps=nb,
                 clamped=True)


MLSTM_BLOCK = 256


def _cumsum_lanes(v):
    n = v.shape[-1]
    pos = lax.broadcasted_iota(jnp.int32, v.shape, v.ndim - 1)
    shift = 1
    while shift < n:
        v = v + jnp.where(pos >= shift, pltpu.roll(v, shift, v.ndim - 1), 0.0)
        shift *= 2
    return v


def _mlstm_body(step, in_refs, out_refs, scratch_refs, emit, n_parts):
    pa_ref, gt_ref, cw_ref, cb_ref, gbc_ref, mn_ref = in_refs
    v_off = 2 * M_QK
    o_off = 2 * M_QK + M_V
    (out_ref,) = out_refs
    qk_buf, c_s, m_s = scratch_refs
    del step
    halo = SUBLANES
    tc = MLSTM_BLOCK

    qk_buf[halo:halo + tc, :] = pa_ref[:, 0:2 * M_QK].astype(F32)
    cw_half = cw_ref[...] * 0.5
    t = jnp.broadcast_to(cb_ref[...] * 0.5, (tc, 2 * M_QK))
    for tap in range(M_CONV):
        lo = halo - (M_CONV - 1) + tap
        t = t + qk_buf[lo:lo + tc, :] * cw_half[tap:tap + 1, :]
    qk_buf[0:halo, :] = qk_buf[tc:tc + halo, :]
    qk = t * (1.0 + jnp.tanh(t))
    q_all = qk[:, 0:M_QK].astype(BF16)
    k_all = qk[:, M_QK:2 * M_QK] * (M_QK_DIM ** -0.5)

    zt = gt_ref[...] + gbc_ref[:, 0:1]
    lf_all = _log_sigmoid(zt)
    b_all = _cumsum_lanes(lf_all)

    row = lax.broadcasted_iota(jnp.int32, (tc, tc), 0)
    col = lax.broadcasted_iota(jnp.int32, (tc, tc), 1)
    causal = col <= row
    ones_aug = jnp.ones((tc, LANES), BF16)

    def inter(h):
        c_prev = c_s[h]
        qc = _dot(q_all[:, h * M_QK_DIM:(h + 1) * M_QK_DIM], c_prev.astype(BF16))
        return c_prev, qc

    def first(h, c_prev, qc):
        qcols = slice(h * M_QK_DIM, (h + 1) * M_QK_DIM)
        f = M_HEADS + h
        q_h = q_all[:, qcols]
        kt = k_all[:, qcols].T
        m_prev = m_s[h][0:1, 0:1]
        u = zt[h:h + 1, :] - b_all[f:f + 1, :]
        b_tot = b_all[f:f + 1, tc - 1:tc]
        v_aug = jnp.concatenate([pa_ref[:, v_off + h * M_V_DIM:v_off + (h + 1) * M_V_DIM],
                                 ones_aug], axis=1)
        qk = _dot(q_h, kt.astype(BF16))
        w_log = b_tot + u
        m_loc = jnp.max(w_log, axis=1, keepdims=True)
        wkt = (kt * jnp.exp(w_log - m_loc)).astype(BF16)
        m_new = jnp.maximum(b_tot + m_prev, m_loc)
        decay = jnp.exp(b_tot + m_prev - m_new)
        gain = jnp.exp(m_loc - m_new)
        m_s[h] = jnp.broadcast_to(m_new, (SUBLANES, LANES))
        return (u, m_prev, v_aug, qk, qc), (wkt, v_aug, c_prev, decay, gain)

    def third(h, wkt, v_aug, c_prev, decay, gain):
        c_s[h] = decay * c_prev + gain * _dot(wkt, v_aug)

    def second(h, u, m_prev, v_aug, qk, qc):
        vcols = slice(h * M_V_DIM, (h + 1) * M_V_DIM)
        f = M_HEADS + h
        um = jnp.where(causal, u, NEG)
        g = jnp.maximum(jnp.max(um, axis=1, keepdims=True), m_prev)
        b_col = jnp.sum(jnp.where(causal, lf_all[f:f + 1, :], 0.0), axis=1, keepdims=True)
        p = jnp.exp(um - g) * qk
        na = _dot(p.astype(BF16), v_aug) + jnp.exp(m_prev - g) * qc
        num = na[:, 0:M_V_DIM]
        den = na[:, M_V_DIM:M_V_DIM + 1]
        hh = num / jnp.maximum(jnp.abs(den), jnp.exp(-(b_col + g)))
        o_gate = pa_ref[:, o_off + h * M_V_DIM:o_off + (h + 1) * M_V_DIM].astype(F32)
        hh = hh * _rms_scale(hh) * mn_ref[0:1, vcols] * _sigmoid(o_gate)
        out_ref[:, vcols] = hh.astype(BF16)

    emit(0)
    parts = [first(h, *inter(h)) for h in range(M_HEADS)]
    groups = n_parts - 1
    per = M_HEADS // groups
    for grp in range(groups):
        emit(1 + grp)
        for h in range(grp * per, (grp + 1) * per):
            second(h, *parts[h][0])
    for h in range(M_HEADS):
        third(h, *parts[h][1])
    assert groups * per == M_HEADS


def _mlstm_init(step, scratch_refs):
    qk_buf, c_s, m_s = scratch_refs

    @pl.when(step == 0)
    def _():
        qk_buf[0:SUBLANES, :] = jnp.zeros((SUBLANES, 2 * M_QK), F32)
        c_s[...] = jnp.zeros(c_s.shape, F32)
        m_s[...] = jnp.full(m_s.shape, NEG, F32)


def _mlstm_side(pa, gt, cw, cb, gbc, mn):
    s = pa.shape[0]
    tc = MLSTM_BLOCK
    const = lambda i: (0, 0)
    return _Side(
        body=_mlstm_body,
        init=_mlstm_init,
        args=(pa, gt, cw, cb, gbc, mn),
        in_specs=(
            _Spec((tc, pa.shape[1]), lambda i: (i, 0)),
            _Spec((SUBLANES, tc), lambda i: (0, i)),
            _Spec((M_CONV, 2 * M_QK), const),
            _Spec((1, 2 * M_QK), const),
            _Spec((SUBLANES, LANES), const),
            _Spec((1, M_V), const),
        ),
        out_specs=(_Spec((tc, M_V), lambda i: (i, 0)),),
        out_shapes=(jax.ShapeDtypeStruct((s, M_V), BF16),),
        scratch=(
            pltpu.VMEM((SUBLANES + tc, 2 * M_QK), F32),
            pltpu.VMEM((M_HEADS, M_QK_DIM, M_V_DIM + LANES), F32),
            pltpu.VMEM((M_HEADS, SUBLANES, LANES), F32),
        ),
        steps=s // tc,
        places=(MXU_WIDTH,) * 3,
    )


ATT_G = 4
ATT_ROWS = ATT_G * CHUNK
ATT_KBLOCKS = (A_PAST_CHUNKS + ATT_G) // ATT_G


def _attn_body(step, in_refs, out_refs, scratch_refs, emit, n_parts):
    del step, scratch_refs
    q_ref, k_ref, v_ref, bias_ref = in_refs
    (out_ref,) = out_refs
    scale2 = (A_HEAD_DIM ** -0.5) * LOG2E
    ones = jnp.ones((ATT_KBLOCKS * ATT_ROWS, A_HEAD_DIM), BF16)

    def scores(h):
        cols = slice(h * A_HEAD_DIM, (h + 1) * A_HEAD_DIM)
        return _dot_nt(q_ref[:, cols], k_ref[:, cols])

    def finish(h, s):
        cols = slice(h * A_HEAD_DIM, (h + 1) * A_HEAD_DIM)
        s = s * scale2 + bias_ref[h]
        p = jnp.exp2(s - jnp.max(s, axis=1, keepdims=True)).astype(BF16)
        o = _dot(p, jnp.concatenate([v_ref[:, cols], ones], axis=1))
        out_ref[:, cols] = (o[:, 0:A_HEAD_DIM] / o[:, A_HEAD_DIM:A_HEAD_DIM + 1]).astype(BF16)

    per = A_HEADS // n_parts
    s_next = [scores(h) for h in range(per)]
    for grp in range(n_parts):
        s_grp = s_next
        if grp + 1 < n_parts:
            s_next = [scores(h) for h in range((grp + 1) * per, (grp + 2) * per)]
        emit(grp)
        for h, s in zip(range(grp * per, (grp + 1) * per), s_grp):
            finish(h, s)


def _attn_side(pb, bias):
    s = pb.shape[0]
    last = ATT_KBLOCKS - 1

    def kv(group):
        return _Spec((pl.Element(ATT_KBLOCKS * ATT_ROWS), pl.Element(A_W)),
                     lambda g: (jnp.maximum(g - last, 0) * ATT_ROWS, group * A_W))

    return _Side(
        body=_attn_body,
        init=None,
        args=(pb,) * 3 + (bias,),
        in_specs=(
            _Spec((ATT_ROWS, A_W), lambda g: (g, 0)),
            kv(1), kv(2),
            _Spec((pl.Element(A_HEADS), pl.Element(ATT_ROWS), pl.Element(ATT_KBLOCKS * ATT_ROWS)),
                  lambda g: (0, 0, jnp.maximum(last - g, 0) * ATT_ROWS)),
        ),
        out_specs=(_Spec((ATT_ROWS, A_W), lambda g: (g, 0)),),
        out_shapes=(jax.ShapeDtypeStruct((s, A_W), BF16),),
        scratch=(),
        steps=s // ATT_ROWS,
        places=(MXU_WIDTH,) * 4,
    )


def _band_bias(rel_table):
    hds = rel_table.shape[0]
    wcols = ATT_KBLOCKS * ATT_ROWS
    band_w = (A_PAST_CHUNKS + 1) * CHUNK
    far = A_PAST_CHUNKS * CHUNK - A_MAX_REL + CHUNK
    tab = rel_table.astype(F32) * LOG2E
    e = jnp.concatenate([jnp.broadcast_to(tab[:, 2 * A_MAX_REL:], (hds, far)),
                         tab[:, A_MAX_REL - CHUNK + 1:2 * A_MAX_REL][:, ::-1],
                         jnp.zeros((hds, 1), F32)], axis=1)
    period = e.shape[1]
    tiled = jnp.broadcast_to(e[:, None, :], (hds, CHUNK, period)).reshape(hds, -1)
    skew = tiled[:, :CHUNK * (period - 1)].reshape(hds, CHUNK, period - 1)
    chunk_bias = skew[:, :, CHUNK - 1:CHUNK - 1 + band_w]
    total = wcols + (ATT_KBLOCKS - 1) * ATT_ROWS
    rows = [jnp.pad(chunk_bias, ((0, 0), (0, 0), (ci * CHUNK, total - band_w - ci * CHUNK)),
                    constant_values=NEG) for ci in range(ATT_G)]
    return jnp.concatenate(rows, axis=1)


def _merge_kernel(hm_ref, ha_ref, gm_ref, ga_ref, x_ref, wm_ref, wa_ref, wo_ref, gb_ref, nf_ref,
                  x1_ref, h2_ref):
    ym = _dot(hm_ref[...], wm_ref[...])
    ya = _dot(ha_ref[...], wa_ref[...])
    merged = (_sigmoid(gm_ref[...].astype(F32) + gb_ref[0:1, :]) * ym
              + _sigmoid(ga_ref[...].astype(F32) + gb_ref[1:2, :]) * ya)
    x1 = x_ref[...] + _dot(merged.astype(BF16), wo_ref[...])
    x1_ref[...] = x1
    h2_ref[...] = (x1 * _rms_scale(x1) * nf_ref[...]).astype(BF16)


def _merge(hm, ha, pc, x, wm, wa, wo, gb, nf, tm=256):
    s, d = x.shape
    const = lambda i: (0, 0)
    resident = pl.Buffered(1)
    return pl.pallas_call(
        _merge_kernel,
        grid=(s // tm,),
        in_specs=[
            pl.BlockSpec((tm, M_V), lambda i: (i, 0)),
            pl.BlockSpec((tm, A_W), lambda i: (i, 0)),
            pl.BlockSpec((tm, d), lambda i: (i, 0)),
            pl.BlockSpec((tm, d), lambda i: (i, 1)),
            pl.BlockSpec((tm, d), lambda i: (i, 0)),
            pl.BlockSpec((M_V, d), const, pipeline_mode=resident),
            pl.BlockSpec((A_W, d), const, pipeline_mode=resident),
            pl.BlockSpec((d, d), const, pipeline_mode=resident),
            pl.BlockSpec((2, d), const),
            pl.BlockSpec((1, d), const),
        ],
        out_specs=[
            pl.BlockSpec((tm, d), lambda i: (i, 0)),
            pl.BlockSpec((tm, d), lambda i: (i, 0)),
        ],
        out_shape=[
            jax.ShapeDtypeStruct((s, d), F32),
            jax.ShapeDtypeStruct((s, d), BF16),
        ],
        compiler_params=_params(("arbitrary",)),
        name="merge",
    )(hm, ha, pc, pc, x, wm, wa, wo, gb, nf)


FFN_SUB_ROWS = 512


def _ffn_up_kernel(h_ref, wg_ref, wv_ref, cw_ref, cb_ref, wd_ref, a_ref, wd_bf_ref, w_bf, ug_buf,
                   *, tm, tf):
    halo = SUBLANES
    rs = FFN_SUB_ROWS
    wd_bf_ref[...] = wd_ref[...].astype(BF16)

    @pl.when(pl.program_id(1) == 0)
    def _():
        w_bf[:, 0:tf] = wg_ref[...].astype(BF16)
        w_bf[:, tf:2 * tf] = wv_ref[...].astype(BF16)
        ug_buf[0:halo, :] = jnp.zeros((halo, tf), F32)

    for r in range(tm // rs):
        h = h_ref[r * rs:(r + 1) * rs, :]
        u = _dot(h, w_bf[...])
        ug = u[:, 0:tf]
        uv = u[:, tf:2 * tf]
        base = halo + r * rs
        ug_buf[base:base + rs, :] = ug
        conv = cb_ref[...] + ug * cw_ref[FFN_CONV - 1:FFN_CONV, :]
        for tap in range(FFN_CONV - 1):
            lo = base - (FFN_CONV - 1) + tap
            conv = conv + ug_buf[lo:lo + rs, :] * cw_ref[tap:tap + 1, :]
        a_ref[r * rs:(r + 1) * rs, :] = (conv * _sigmoid(conv) * uv).astype(BF16)
    ug_buf[0:halo, :] = ug_buf[tm:tm + halo, :]


def _ffn_up(h2, w_up, cw, cb, w_down, tm=2048, tf=512):
    s, d = h2.shape
    nf = D_FF // tf
    n_m = s // tm
    wd_rows = D_FF // (nf * n_m)
    assert wd_rows * nf * n_m == D_FF and wd_rows % (2 * SUBLANES) == 0
    wd_spec = pl.BlockSpec((wd_rows, d), lambda j, m: (j * n_m + m, 0))
    return pl.pallas_call(
        functools.partial(_ffn_up_kernel, tm=tm, tf=tf),
        grid=(nf, n_m),
        in_specs=[
            pl.BlockSpec((tm, d), lambda j, m: (m, 0)),
            pl.BlockSpec((d, tf), lambda j, m: (0, j)),
            pl.BlockSpec((d, tf), lambda j, m: (0, nf + j)),
            pl.BlockSpec((FFN_CONV, tf), lambda j, m: (0, j)),
            pl.BlockSpec((1, tf), lambda j, m: (0, j)),
            wd_spec,
        ],
        out_specs=[pl.BlockSpec((tm, tf), lambda j, m: (m, j)), wd_spec],
        out_shape=[jax.ShapeDtypeStruct((s, D_FF), BF16),
                   jax.ShapeDtypeStruct((D_FF, d), BF16)],
        scratch_shapes=[
            pltpu.VMEM((d, 2 * tf), BF16),
            pltpu.VMEM((SUBLANES + tm, tf), F32),
        ],
        compiler_params=_params(("arbitrary", "arbitrary")),
        name="ffn_up",
    )(h2, w_up, w_up, cw, cb, w_down)


FFN_DOWN_COLS = 512


def _ffn_down_kernel(a_ref, w_ref, x1_ref, nf_ref, o_ref):
    d = o_ref.shape[1]
    ssq = jnp.zeros((o_ref.shape[0], 1), F32)
    for n in range(d // FFN_DOWN_COLS):
        cols = slice(n * FFN_DOWN_COLS, (n + 1) * FFN_DOWN_COLS)
        y = x1_ref[:, cols] + _dot(a_ref[...], w_ref[:, cols])
        o_ref[:, cols] = y
        ssq = ssq + jnp.sum(y * y, axis=-1, keepdims=True)
    o_ref[...] = o_ref[...] * lax.rsqrt(ssq * (1.0 / d) + EPS) * nf_ref[...]


def _ffn_down(a, w_down, x1, nf, tm=512):
    s, d = x1.shape
    dff = a.shape[1]
    return pl.pallas_call(
        _ffn_down_kernel,
        grid=(s // tm,),
        in_specs=[
            pl.BlockSpec((tm, dff), lambda i: (i, 0)),
            pl.BlockSpec((dff, d), lambda i: (0, 0), pipeline_mode=pl.Buffered(1)),
            pl.BlockSpec((tm, d), lambda i: (i, 0)),
            pl.BlockSpec((1, d), lambda i: (0, 0)),
        ],
        out_specs=pl.BlockSpec((tm, d), lambda i: (i, 0)),
        out_shape=jax.ShapeDtypeStruct((s, d), F32),
        compiler_params=_params(("arbitrary",)),
        name="ffn_down",
    )(a, w_down, x1, nf)


def kernel(x, norm_mix, w_in, conv_qk_w, conv_qk_b, b_igate, b_fgate, m_norm, rel_bias, gate_bias,
           w_branch_m, w_branch_a, w_out, norm_ffn, w_up, conv_ffn_w, conv_ffn_b, w_down, norm_final):
    batch, seq, d = x.shape
    depth = w_in.shape[0]
    assert (batch, seq, d, depth) == (1, SEQ, D_MODEL, 1)
    xs = x[0]
    l = 0

    w_in_t = jnp.swapaxes(w_in[l], 0, 1)
    w_g = jnp.pad(w_in_t[GATE_COL0:GATE_COL0 + N_GATES, :].T, ((0, 0), (0, LANES - N_GATES)))
    gate_b = jnp.concatenate([b_igate[l], b_fgate[l]])
    gb_col = jnp.broadcast_to(gate_b[:, None], (SUBLANES, LANES))
    bias = _band_bias(rel_bias[l])

    h, gt = _prenorm(xs, norm_mix[l][None, :], w_g.astype(BF16))
    na = 2 * M_QK + 2 * M_V
    attn0 = GATE_COL0 + N_GATES
    pa, w_m_bf, w_a_bf, w_out_bf = _proj(
        h, w_in_t, 0, na, BF16, "proj_mlstm",
        sides=[_cast_side(w_branch_m[l], 128), _cast_side(w_branch_a[l], 128),
               _cast_side(w_out[l], 128)])
    mlstm = _mlstm_side(pa, gt, conv_qk_w[l], conv_qk_b[l][None, :], gb_col, m_norm[l][None, :])
    pb, hm = _proj(h, w_in_t, attn0, 3 * A_W, BF16, "proj_attn_mlstm", tn=768, sides=[mlstm])
    pc, ha = _proj(h, w_in_t, attn0 + 3 * A_W, 2 * D_MODEL, BF16, "proj_gates_attn",
                   sides=[_attn_side(pb, bias)])

    x1, h2 = _merge(hm, ha, pc, xs, w_m_bf, w_a_bf, w_out_bf, gate_bias[l], norm_ffn[l][None, :])
    a, w_down_bf = _ffn_up(h2, w_up[l], conv_ffn_w[l], conv_ffn_b[l][None, :], w_down[l])
    out = _ffn_down(a, w_down_bf, x1, norm_final[None, :])
    return out[None]
```

```python
import functools
import math
from typing import Callable, NamedTuple

import jax
import jax.numpy as jnp
from jax import lax
from jax.experimental import pallas as pl
from jax.experimental.pallas import tpu as pltpu

D_MODEL = 2048
SEQ = 8192
CHUNK = 64
M_HEADS = 4
M_QK_DIM = 128
M_V_DIM = 256
M_CONV = 4
A_HEADS = 8
A_HEAD_DIM = 128
A_PAST_CHUNKS = 8
A_MAX_REL = 128
D_FF = 5632
FFN_CONV = 3
EPS = 1e-6
NEG = -1e30
LOG2E = math.log2(math.e)

M_QK = M_HEADS * M_QK_DIM
M_V = M_HEADS * M_V_DIM
A_W = A_HEADS * A_HEAD_DIM
GATE_COL0 = 2 * M_QK + 2 * M_V
N_GATES = 2 * M_HEADS

LANES = 128
SUBLANES = 8
MXU_WIDTH = 256
VMEM_LIMIT = 56 * 1024 * 1024

BF16 = jnp.bfloat16
F32 = jnp.float32


def _params(sem):
    return pltpu.CompilerParams(dimension_semantics=sem, vmem_limit_bytes=VMEM_LIMIT)


def _dot(a, b):
    return jnp.dot(a, b, preferred_element_type=F32)


def _dot_nt(a, b):
    return lax.dot_general(a, b, (((1,), (1,)), ((), ())), preferred_element_type=F32)


def _sigmoid(z):
    return 0.5 * jnp.tanh(0.5 * z) + 0.5


def _log_sigmoid(z):
    return jnp.minimum(z, 0.0) - jnp.log(1.0 + jnp.exp(-jnp.abs(z)))


def _rms_scale(y):
    return lax.rsqrt(jnp.mean(y * y, axis=-1, keepdims=True) + EPS)


def _prenorm_kernel(x_ref, g_ref, wg_ref, h_ref, gt_ref):
    xf = x_ref[...]
    hb = (xf * _rms_scale(xf) * g_ref[...]).astype(BF16)
    h_ref[...] = hb
    gcol = _dot(hb, wg_ref[...])
    gt_ref[...] = gcol.T[0:SUBLANES, :]


def _prenorm(x, g, wg, tm=1024):
    s, d = x.shape
    return pl.pallas_call(
        _prenorm_kernel,
        grid=(s // tm,),
        in_specs=[
            pl.BlockSpec((tm, d), lambda i: (i, 0)),
            pl.BlockSpec((1, d), lambda i: (0, 0)),
            pl.BlockSpec((d, LANES), lambda i: (0, 0)),
        ],
        out_specs=[
            pl.BlockSpec((tm, d), lambda i: (i, 0)),
            pl.BlockSpec((SUBLANES, tm), lambda i: (0, i)),
        ],
        out_shape=[
            jax.ShapeDtypeStruct((s, d), BF16),
            jax.ShapeDtypeStruct((SUBLANES, s), F32),
        ],
        compiler_params=_params(("arbitrary",)),
        name="prenorm",
    )(x, g, wg)


class _Spec(NamedTuple):
    block: tuple
    index: Callable
    mode: object = None


class _Side(NamedTuple):
    body: Callable
    init: Callable
    args: tuple
    in_specs: tuple
    out_specs: tuple
    out_shapes: tuple
    scratch: tuple
    steps: int
    clamped: bool = False
    places: tuple = ()


def _proj_kernel(*refs, shift, tn, sides):
    n_w = 3 if shift else 2
    h_ref, wm_ref = refs[0], refs[1]
    pos = n_w
    side_in = []
    for sd in sides:
        side_in.append(refs[pos:pos + len(sd.args)])
        pos += len(sd.args)
    o_ref = refs[pos]
    pos += 1
    side_out = []
    for sd in sides:
        side_out.append(refs[pos:pos + len(sd.out_specs)])
        pos += len(sd.out_specs)
    w_bf = refs[pos]
    pos += 1
    side_scratch = []
    for sd in sides:
        side_scratch.append(refs[pos:pos + len(sd.scratch)])
        pos += len(sd.scratch)

    @pl.when(pl.program_id(1) == 0)
    def _():
        if shift:
            w_bf[0:tn - shift, :] = wm_ref[shift:tn, :].astype(BF16)
            w_bf[tn - shift:tn, :] = refs[2][...].astype(BF16)
        else:
            w_bf[...] = wm_ref[...].astype(BF16)

    step = pl.program_id(0) * pl.num_programs(1) + pl.program_id(1)
    for sd, scr in zip(sides, side_scratch):
        if sd.init:
            sd.init(step, scr)

    placers = [idx for idx, sd in enumerate(sides) if sd.places]
    assert len(placers) <= 1
    widths = sides[placers[0]].places if placers else (tn // 2, tn // 2)
    n_parts = len(widths)
    assert sum(widths) == tn and all(w % MXU_WIDTH == 0 for w in widths)

    def emit(k):
        lo = sum(widths[:k])
        cols = slice(lo, lo + widths[k])
        o_ref[:, cols] = _dot_nt(h_ref[...], w_bf[cols, :]).astype(o_ref.dtype)

    for idx, sd in enumerate(sides):
        if not sd.places:
            sd.body(step, side_in[idx], side_out[idx], side_scratch[idx], None, 0)
    if placers:
        idx = placers[0]
        sides[idx].body(step, side_in[idx], side_out[idx], side_scratch[idx], emit, n_parts)
    else:
        for k in range(n_parts):
            emit(k)


def _proj(h, wt, col0, ncols, out_dtype, name, tm=1024, tn=1024, sides=()):
    s, d = h.shape
    shift = col0 % tn
    base = col0 - shift
    n_m = s // tm
    n_steps = (ncols // tn) * n_m
    assert shift % SUBLANES == 0 and ncols % tn == 0
    assert all(sd.steps == n_steps or (sd.clamped and sd.steps < n_steps) for sd in sides)

    def flat(spec):
        return pl.BlockSpec(spec.block, lambda j, m: spec.index(j * n_m + m),
                            pipeline_mode=spec.mode)

    in_specs = [
        pl.BlockSpec((tm, d), lambda j, m: (m, 0)),
        pl.BlockSpec((tn, d), lambda j, m: (base // tn + j, 0)),
    ]
    args = [h, wt]
    if shift:
        in_specs.append(pl.BlockSpec((shift, d), lambda j, m: ((base + (j + 1) * tn) // shift, 0)))
        args.append(wt)
    out_specs = [pl.BlockSpec((tm, tn), lambda j, m: (m, j))]
    out_shapes = [jax.ShapeDtypeStruct((s, ncols), out_dtype)]
    scratch = [pltpu.VMEM((tn, d), BF16)]
    for sd in sides:
        in_specs += [flat(sp) for sp in sd.in_specs]
        args += list(sd.args)
    for sd in sides:
        out_specs += [flat(sp) for sp in sd.out_specs]
        out_shapes += list(sd.out_shapes)
    for sd in sides:
        scratch += list(sd.scratch)
    return pl.pallas_call(
        functools.partial(_proj_kernel, shift=shift, tn=tn, sides=tuple(sides)),
        grid=(ncols // tn, n_m),
        in_specs=in_specs,
        out_specs=out_specs,
        out_shape=out_shapes,
        scratch_shapes=scratch,
        compiler_params=_params(("arbitrary", "arbitrary")),
        name=name,
    )(*args)


def _cast_body(step, in_refs, out_refs, scratch_refs, emit, n_parts):
    del step, scratch_refs, emit, n_parts
    out_refs[0][...] = in_refs[0][...].astype(BF16)


def _cast_side(w, rows):
    n, d = w.shape
    nb = n // rows
    spec = _Spec((rows, d), lambda i: (jnp.minimum(i, nb - 1), 0))
    return _Side(body=_cast_body, init=None, args=(w,), in_specs=(spec,), out_specs=(spec,),
                 out_shapes=(jax.ShapeDtypeStruct((n, d), BF16),), scratch=(), steps=nb,
                 clamped=True)


MLSTM_BLOCK = 256


def _cumsum_lanes(v):
    n = v.shape[-1]
    pos = lax.broadcasted_iota(jnp.int32, v.shape, v.ndim - 1)
    shift = 1
    while shift < n:
        v = v + jnp.where(pos >= shift, pltpu.roll(v, shift, v.ndim - 1), 0.0)
        shift *= 2
    return v


def _mlstm_body(step, in_refs, out_refs, scratch_refs, emit, n_parts):
    pa_ref, gt_ref, cw_ref, cb_ref, gbc_ref, mn_ref = in_refs
    v_off = 2 * M_QK
    o_off = 2 * M_QK + M_V
    (out_ref,) = out_refs
    qk_buf, c_s, m_s = scratch_refs
    del step
    halo = SUBLANES
    tc = MLSTM_BLOCK

    qk_buf[halo:halo + tc, :] = pa_ref[:, 0:2 * M_QK].astype(F32)
    cw_half = cw_ref[...] * 0.5
    t = jnp.broadcast_to(cb_ref[...] * 0.5, (tc, 2 * M_QK))
    for tap in range(M_CONV):
        lo = halo - (M_CONV - 1) + tap
        t = t + qk_buf[lo:lo + tc, :] * cw_half[tap:tap + 1, :]
    qk_buf[0:halo, :] = qk_buf[tc:tc + halo, :]
    qk = t * (1.0 + jnp.tanh(t))
    q_all = qk[:, 0:M_QK].astype(BF16)
    k_all = qk[:, M_QK:2 * M_QK] * (M_QK_DIM ** -0.5)

    zt = gt_ref[...] + gbc_ref[:, 0:1]
    lf_all = _log_sigmoid(zt)
    b_all = _cumsum_lanes(lf_all)

    row = lax.broadcasted_iota(jnp.int32, (tc, tc), 0)
    col = lax.broadcasted_iota(jnp.int32, (tc, tc), 1)
    causal = col <= row
    ones_aug = jnp.ones((tc, LANES), BF16)

    def inter(h):
        c_prev = c_s[h]
        qc = _dot(q_all[:, h * M_QK_DIM:(h + 1) * M_QK_DIM], c_prev.astype(BF16))
        return c_prev, qc

    def first(h, c_prev, qc):
        qcols = slice(h * M_QK_DIM, (h + 1) * M_QK_DIM)
        f = M_HEADS + h
        q_h = q_all[:, qcols]
        kt = k_all[:, qcols].T
        m_prev = m_s[h][0:1, 0:1]
        u = zt[h:h + 1, :] - b_all[f:f + 1, :]
        b_tot = b_all[f:f + 1, tc - 1:tc]
        v_aug = jnp.concatenate([pa_ref[:, v_off + h * M_V_DIM:v_off + (h + 1) * M_V_DIM],
                                 ones_aug], axis=1)
        qk = _dot(q_h, kt.astype(BF16))
        w_log = b_tot + u
        m_loc = jnp.max(w_log, axis=1, keepdims=True)
        wkt = (kt * jnp.exp(w_log - m_loc)).astype(BF16)
        m_new = jnp.maximum(b_tot + m_prev, m_loc)
        decay = jnp.exp(b_tot + m_prev - m_new)
        gain = jnp.exp(m_loc - m_new)
        m_s[h] = jnp.broadcast_to(m_new, (SUBLANES, LANES))
        return (u, m_prev, v_aug, qk, qc), (wkt, v_aug, c_prev, decay, gain)

    def third(h, wkt, v_aug, c_prev, decay, gain):
        c_s[h] = decay * c_prev + gain * _dot(wkt, v_aug)

    def second(h, u, m_prev, v_aug, qk, qc):
        vcols = slice(h * M_V_DIM, (h + 1) * M_V_DIM)
        f = M_HEADS + h
        um = jnp.where(causal, u, NEG)
        g = jnp.maximum(jnp.max(um, axis=1, keepdims=True), m_prev)
        b_col = jnp.sum(jnp.where(causal, lf_all[f:f + 1, :], 0.0), axis=1, keepdims=True)
        p = jnp.exp(um - g) * qk
        na = _dot(p.astype(BF16), v_aug) + jnp.exp(m_prev - g) * qc
        num = na[:, 0:M_V_DIM]
        den = na[:, M_V_DIM:M_V_DIM + 1]
        hh = num / jnp.maximum(jnp.abs(den), jnp.exp(-(b_col + g)))
        o_gate = pa_ref[:, o_off + h * M_V_DIM:o_off + (h + 1) * M_V_DIM].astype(F32)
        hh = hh * _rms_scale(hh) * mn_ref[0:1, vcols] * _sigmoid(o_gate)
        out_ref[:, vcols] = hh.astype(BF16)

    emit(0)
    parts = [first(h, *inter(h)) for h in range(M_HEADS)]
    groups = n_parts - 1
    per = M_HEADS // groups
    for grp in range(groups):
        emit(1 + grp)
        for h in range(grp * per, (grp + 1) * per):
            second(h, *parts[h][0])
    for h in range(M_HEADS):
        third(h, *parts[h][1])
    assert groups * per == M_HEADS


def _mlstm_init(step, scratch_refs):
    qk_buf, c_s, m_s = scratch_refs

    @pl.when(step == 0)
    def _():
        qk_buf[0:SUBLANES, :] = jnp.zeros((SUBLANES, 2 * M_QK), F32)
        c_s[...] = jnp.zeros(c_s.shape, F32)
        m_s[...] = jnp.full(m_s.shape, NEG, F32)


def _mlstm_side(pa, gt, cw, cb, gbc, mn):
    s = pa.shape[0]
    tc = MLSTM_BLOCK
    const = lambda i: (0, 0)
    return _Side(
        body=_mlstm_body,
        init=_mlstm_init,
        args=(pa, gt, cw, cb, gbc, mn),
        in_specs=(
            _Spec((tc, pa.shape[1]), lambda i: (i, 0)),
            _Spec((SUBLANES, tc), lambda i: (0, i)),
            _Spec((M_CONV, 2 * M_QK), const),
            _Spec((1, 2 * M_QK), const),
            _Spec((SUBLANES, LANES), const),
            _Spec((1, M_V), const),
        ),
        out_specs=(_Spec((tc, M_V), lambda i: (i, 0)),),
        out_shapes=(jax.ShapeDtypeStruct((s, M_V), BF16),),
        scratch=(
            pltpu.VMEM((SUBLANES + tc, 2 * M_QK), F32),
            pltpu.VMEM((M_HEADS, M_QK_DIM, M_V_DIM + LANES), F32),
            pltpu.VMEM((M_HEADS, SUBLANES, LANES), F32),
        ),
        steps=s // tc,
        places=(MXU_WIDTH,) * 3,
    )


ATT_G = 4
ATT_ROWS = ATT_G * CHUNK
ATT_KBLOCKS = (A_PAST_CHUNKS + ATT_G) // ATT_G


def _attn_body(step, in_refs, out_refs, scratch_refs, emit, n_parts):
    del step, scratch_refs
    q_ref, k_ref, v_ref, bias_ref = in_refs
    (out_ref,) = out_refs
    scale2 = (A_HEAD_DIM ** -0.5) * LOG2E
    ones = jnp.ones((ATT_KBLOCKS * ATT_ROWS, A_HEAD_DIM), BF16)

    def scores(h):
        cols = slice(h * A_HEAD_DIM, (h + 1) * A_HEAD_DIM)
        return _dot_nt(q_ref[:, cols], k_ref[:, cols])

    def finish(h, s):
        cols = slice(h * A_HEAD_DIM, (h + 1) * A_HEAD_DIM)
        s = s * scale2 + bias_ref[h]
        p = jnp.exp2(s - jnp.max(s, axis=1, keepdims=True)).astype(BF16)
        o = _dot(p, jnp.concatenate([v_ref[:, cols], ones], axis=1))
        out_ref[:, cols] = (o[:, 0:A_HEAD_DIM] / o[:, A_HEAD_DIM:A_HEAD_DIM + 1]).astype(BF16)

    per = A_HEADS // n_parts
    s_next = [scores(h) for h in range(per)]
    for grp in range(n_parts):
        s_grp = s_next
        if grp + 1 < n_parts:
            s_next = [scores(h) for h in range((grp + 1) * per, (grp + 2) * per)]
        emit(grp)
        for h, s in zip(range(grp * per, (grp + 1) * per), s_grp):
            finish(h, s)


def _attn_side(pb, bias):
    s = pb.shape[0]
    last = ATT_KBLOCKS - 1

    def kv(group):
        return _Spec((pl.Element(ATT_KBLOCKS * ATT_ROWS), pl.Element(A_W)),
                     lambda g: (jnp.maximum(g - last, 0) * ATT_ROWS, group * A_W))

    return _Side(
        body=_attn_body,
        init=None,
        args=(pb,) * 3 + (bias,),
        in_specs=(
            _Spec((ATT_ROWS, A_W), lambda g: (g, 0)),
            kv(1), kv(2),
            _Spec((pl.Element(A_HEADS), pl.Element(ATT_ROWS), pl.Element(ATT_KBLOCKS * ATT_ROWS)),
                  lambda g: (0, 0, jnp.maximum(last - g, 0) * ATT_ROWS)),
        ),
        out_specs=(_Spec((ATT_ROWS, A_W), lambda g: (g, 0)),),
        out_shapes=(jax.ShapeDtypeStruct((s, A_W), BF16),),
        scratch=(),
        steps=s // ATT_ROWS,
        places=(MXU_WIDTH,) * 4,
    )


def _band_bias(rel_table):
    hds = rel_table.shape[0]
    wcols = ATT_KBLOCKS * ATT_ROWS
    band_w = (A_PAST_CHUNKS + 1) * CHUNK
    far = A_PAST_CHUNKS * CHUNK - A_MAX_REL + CHUNK
    tab = rel_table.astype(F32) * LOG2E
    e = jnp.concatenate([jnp.broadcast_to(tab[:, 2 * A_MAX_REL:], (hds, far)),
                         tab[:, A_MAX_REL - CHUNK + 1:2 * A_MAX_REL][:, ::-1],
                         jnp.zeros((hds, 1), F32)], axis=1)
    period = e.shape[1]
    tiled = jnp.broadcast_to(e[:, None, :], (hds, CHUNK, period)).reshape(hds, -1)
    skew = tiled[:, :CHUNK * (period - 1)].reshape(hds, CHUNK, period - 1)
    chunk_bias = skew[:, :, CHUNK - 1:CHUNK - 1 + band_w]
    total = wcols + (ATT_KBLOCKS - 1) * ATT_ROWS
    rows = [jnp.pad(chunk_bias, ((0, 0), (0, 0), (ci * CHUNK, total - band_w - ci * CHUNK)),
                    constant_values=NEG) for ci in range(ATT_G)]
    return jnp.concatenate(rows, axis=1)


def _merge_kernel(hm_ref, ha_ref, gm_ref, ga_ref, x_ref, wm_ref, wa_ref, wo_ref, gb_ref, nf_ref,
                  x1_ref, h2_ref):
    ym = _dot(hm_ref[...], wm_ref[...])
    ya = _dot(ha_ref[...], wa_ref[...])
    merged = (_sigmoid(gm_ref[...].astype(F32) + gb_ref[0:1, :]) * ym
              + _sigmoid(ga_ref[...].astype(F32) + gb_ref[1:2, :]) * ya)
    x1 = x_ref[...] + _dot(merged.astype(BF16), wo_ref[...])
    x1_ref[...] = x1
    h2_ref[...] = (x1 * _rms_scale(x1) * nf_ref[...]).astype(BF16)


def _merge(hm, ha, pc, x, wm, wa, wo, gb, nf, tm=256):
    s, d = x.shape
    const = lambda i: (0, 0)
    resident = pl.Buffered(1)
    return pl.pallas_call(
        _merge_kernel,
        grid=(s // tm,),
        in_specs=[
            pl.BlockSpec((tm, M_V), lambda i: (i, 0)),
            pl.BlockSpec((tm, A_W), lambda i: (i, 0)),
            pl.BlockSpec((tm, d), lambda i: (i, 0)),
            pl.BlockSpec((tm, d), lambda i: (i, 1)),
            pl.BlockSpec((tm, d), lambda i: (i, 0)),
            pl.BlockSpec((M_V, d), const, pipeline_mode=resident),
            pl.BlockSpec((A_W, d), const, pipeline_mode=resident),
            pl.BlockSpec((d, d), const, pipeline_mode=resident),
            pl.BlockSpec((2, d), const),
            pl.BlockSpec((1, d), const),
        ],
        out_specs=[
            pl.BlockSpec((tm, d), lambda i: (i, 0)),
            pl.BlockSpec((tm, d), lambda i: (i, 0)),
        ],
        out_shape=[
            jax.ShapeDtypeStruct((s, d), F32),
            jax.ShapeDtypeStruct((s, d), BF16),
        ],
        compiler_params=_params(("arbitrary",)),
        name="merge",
    )(hm, ha, pc, pc, x, wm, wa, wo, gb, nf)


FFN_SUB_ROWS = 512


def _ffn_up_kernel(h_ref, wg_ref, wv_ref, cw_ref, cb_ref, wd_ref, a_ref, wd_bf_ref, w_bf, ug_buf,
                   *, tm, tf):
    halo = SUBLANES
    rs = FFN_SUB_ROWS
    wd_bf_ref[...] = wd_ref[...].astype(BF16)

    @pl.when(pl.program_id(1) == 0)
    def _():
        w_bf[:, 0:tf] = wg_ref[...].astype(BF16)
        w_bf[:, tf:2 * tf] = wv_ref[...].astype(BF16)
        ug_buf[0:halo, :] = jnp.zeros((halo, tf), F32)

    for r in range(tm // rs):
        h = h_ref[r * rs:(r + 1) * rs, :]
        u = _dot(h, w_bf[...])
        ug = u[:, 0:tf]
        uv = u[:, tf:2 * tf]
        base = halo + r * rs
        ug_buf[base:base + rs, :] = ug
        conv = cb_ref[...] + ug * cw_ref[FFN_CONV - 1:FFN_CONV, :]
        for tap in range(FFN_CONV - 1):
            lo = base - (FFN_CONV - 1) + tap
            conv = conv + ug_buf[lo:lo + rs, :] * cw_ref[tap:tap + 1, :]
        a_ref[r * rs:(r + 1) * rs, :] = (conv * _sigmoid(conv) * uv).astype(BF16)
    ug_buf[0:halo, :] = ug_buf[tm:tm + halo, :]


def _ffn_up(h2, w_up, cw, cb, w_down, tm=2048, tf=512):
    s, d = h2.shape
    nf = D_FF // tf
    n_m = s // tm
    wd_rows = D_FF // (nf * n_m)
    assert wd_rows * nf * n_m == D_FF and wd_rows % (2 * SUBLANES) == 0
    wd_spec = pl.BlockSpec((wd_rows, d), lambda j, m: (j * n_m + m, 0))
    return pl.pallas_call(
        functools.partial(_ffn_up_kernel, tm=tm, tf=tf),
        grid=(nf, n_m),
        in_specs=[
            pl.BlockSpec((tm, d), lambda j, m: (m, 0)),
            pl.BlockSpec((d, tf), lambda j, m: (0, j)),
            pl.BlockSpec((d, tf), lambda j, m: (0, nf + j)),
            pl.BlockSpec((FFN_CONV, tf), lambda j, m: (0, j)),
            pl.BlockSpec((1, tf), lambda j, m: (0, j)),
            wd_spec,
        ],
        out_specs=[pl.BlockSpec((tm, tf), lambda j, m: (m, j)), wd_spec],
        out_shape=[jax.ShapeDtypeStruct((s, D_FF), BF16),
                   jax.ShapeDtypeStruct((D_FF, d), BF16)],
        scratch_shapes=[
            pltpu.VMEM((d, 2 * tf), BF16),
            pltpu.VMEM((SUBLANES + tm, tf), F32),
        ],
        compiler_params=_params(("arbitrary", "arbitrary")),
        name="ffn_up",
    )(h2, w_up, w_up, cw, cb, w_down)


FFN_DOWN_COLS = 512


def _ffn_down_kernel(a_ref, w_ref, x1_ref, nf_ref, o_ref):
    d = o_ref.shape[1]
    ssq = jnp.zeros((o_ref.shape[0], 1), F32)
    for n in range(d // FFN_DOWN_COLS):
        cols = slice(n * FFN_DOWN_COLS, (n + 1) * FFN_DOWN_COLS)
        y = x1_ref[:, cols] + _dot(a_ref[...], w_ref[:, cols])
        o_ref[:, cols] = y
        ssq = ssq + jnp.sum(y * y, axis=-1, keepdims=True)
    o_ref[...] = o_ref[...] * lax.rsqrt(ssq * (1.0 / d) + EPS) * nf_ref[...]


def _ffn_down(a, w_down, x1, nf, tm=512):
    s, d = x1.shape
    dff = a.shape[1]
    return pl.pallas_call(
        _ffn_down_kernel,
        grid=(s // tm,),
        in_specs=[
            pl.BlockSpec((tm, dff), lambda i: (i, 0)),
            pl.BlockSpec((dff, d), lambda i: (0, 0), pipeline_mode=pl.Buffered(1)),
            pl.BlockSpec((tm, d), lambda i: (i, 0)),
            pl.BlockSpec((1, d), lambda i: (0, 0)),
        ],
        out_specs=pl.BlockSpec((tm, d), lambda i: (i, 0)),
        out_shape=jax.ShapeDtypeStruct((s, d), F32),
        compiler_params=_params(("arbitrary",)),
        name="ffn_down",
    )(a, w_down, x1, nf)


def kernel(x, norm_mix, w_in, conv_qk_w, conv_qk_b, b_igate, b_fgate, m_norm, rel_bias, gate_bias,
           w_branch_m, w_branch_a, w_out, norm_ffn, w_up, conv_ffn_w, conv_ffn_b, w_down, norm_final):
    batch, seq, d = x.shape
    depth = w_in.shape[0]
    assert (batch, seq, d, depth) == (1, SEQ, D_MODEL, 1)
    xs = x[0]
    l = 0

    w_in_t = jnp.swapaxes(w_in[l], 0, 1)
    w_g = jnp.pad(w_in_t[GATE_COL0:GATE_COL0 + N_GATES, :].T, ((0, 0), (0, LANES - N_GATES)))
    gate_b = jnp.concatenate([b_igate[l], b_fgate[l]])
    gb_col = jnp.broadcast_to(gate_b[:, None], (SUBLANES, LANES))
    bias = _band_bias(rel_bias[l])

    h, gt = _prenorm(xs, norm_mix[l][None, :], w_g.astype(BF16))
    na = 2 * M_QK + 2 * M_V
    attn0 = GATE_COL0 + N_GATES
    pa, w_m_bf, w_a_bf = _proj(h, w_in_t, 0, na, BF16, "proj_mlstm", tm=2048,
                               sides=[_cast_side(w_branch_m[l], 128), _cast_side(w_branch_a[l], 128)])
    mlstm = _mlstm_side(pa, gt, conv_qk_w[l], conv_qk_b[l][None, :], gb_col, m_norm[l][None, :])
    pb, hm, w_out_bf = _proj(h, w_in_t, attn0, 3 * A_W, BF16, "proj_attn_mlstm", tn=768,
                             sides=[mlstm, _cast_side(w_out[l], 128)])
    pc, ha = _proj(h, w_in_t, attn0 + 3 * A_W, 2 * D_MODEL, BF16, "proj_gates_attn",
                   sides=[_attn_side(pb, bias)])

    x1, h2 = _merge(hm, ha, pc, xs, w_m_bf, w_a_bf, w_out_bf, gate_bias[l], norm_ffn[l][None, :])
    a, w_down_bf = _ffn_up(h2, w_up[l], conv_ffn_w[l], conv_ffn_b[l][None, :], w_down[l])
    out = _ffn_down(a, w_down_bf, x1, norm_final[None, :])
    return out[None]
```

```python
import functools
import math
from typing import Callable, NamedTuple

import jax
import jax.numpy as jnp
from jax import lax
from jax.experimental import pallas as pl
from jax.experimental.pallas import tpu as pltpu

D_MODEL = 2048
SEQ = 8192
CHUNK = 64
M_HEADS = 4
M_QK_DIM = 128
M_V_DIM = 256
M_CONV = 4
A_HEADS = 8
A_HEAD_DIM = 128
A_PAST_CHUNKS = 8
A_MAX_REL = 128
D_FF = 5632
FFN_CONV = 3
EPS = 1e-6
NEG = -1e30
LOG2E = math.log2(math.e)

M_QK = M_HEADS * M_QK_DIM
M_V = M_HEADS * M_V_DIM
A_W = A_HEADS * A_HEAD_DIM
GATE_COL0 = 2 * M_QK + 2 * M_V
N_GATES = 2 * M_HEADS

LANES = 128
SUBLANES = 8
MXU_WIDTH = 256
VMEM_LIMIT = 56 * 1024 * 1024

BF16 = jnp.bfloat16
F32 = jnp.float32


def _params(sem):
    return pltpu.CompilerParams(dimension_semantics=sem, vmem_limit_bytes=VMEM_LIMIT)


def _dot(a, b):
    return jnp.dot(a, b, preferred_element_type=F32)


def _dot_nt(a, b):
    return lax.dot_general(a, b, (((1,), (1,)), ((), ())), preferred_element_type=F32)


def _sigmoid(z):
    return 0.5 * jnp.tanh(0.5 * z) + 0.5


def _log_sigmoid(z):
    return jnp.minimum(z, 0.0) - jnp.log(1.0 + jnp.exp(-jnp.abs(z)))


def _rms_scale(y):
    return lax.rsqrt(jnp.mean(y * y, axis=-1, keepdims=True) + EPS)


def _prenorm_kernel(x_ref, g_ref, wg_ref, h_ref, gt_ref):
    xf = x_ref[...]
    hb = (xf * _rms_scale(xf) * g_ref[...]).astype(BF16)
    h_ref[...] = hb
    gcol = _dot(hb, wg_ref[...])
    gt_ref[...] = gcol.T[0:SUBLANES, :]


def _prenorm(x, g, wg, tm=1024):
    s, d = x.shape
    return pl.pallas_call(
        _prenorm_kernel,
        grid=(s // tm,),
        in_specs=[
            pl.BlockSpec((tm, d), lambda i: (i, 0)),
            pl.BlockSpec((1, d), lambda i: (0, 0)),
            pl.BlockSpec((d, LANES), lambda i: (0, 0)),
        ],
        out_specs=[
            pl.BlockSpec((tm, d), lambda i: (i, 0)),
            pl.BlockSpec((SUBLANES, tm), lambda i: (0, i)),
        ],
        out_shape=[
            jax.ShapeDtypeStruct((s, d), BF16),
            jax.ShapeDtypeStruct((SUBLANES, s), F32),
        ],
        compiler_params=_params(("arbitrary",)),
        name="prenorm",
    )(x, g, wg)


class _Spec(NamedTuple):
    block: tuple
    index: Callable
    mode: object = None


class _Side(NamedTuple):
    body: Callable
    init: Callable
    args: tuple
    in_specs: tuple
    out_specs: tuple
    out_shapes: tuple
    scratch: tuple
    steps: int
    clamped: bool = False
    places: tuple = ()


def _proj_kernel(*refs, shift, tn, sides):
    n_w = 3 if shift else 2
    h_ref, wm_ref = refs[0], refs[1]
    pos = n_w
    side_in = []
    for sd in sides:
        side_in.append(refs[pos:pos + len(sd.args)])
        pos += len(sd.args)
    o_ref = refs[pos]
    pos += 1
    side_out = []
    for sd in sides:
        side_out.append(refs[pos:pos + len(sd.out_specs)])
        pos += len(sd.out_specs)
    w_bf = refs[pos]
    pos += 1
    side_scratch = []
    for sd in sides:
        side_scratch.append(refs[pos:pos + len(sd.scratch)])
        pos += len(sd.scratch)

    @pl.when(pl.program_id(1) == 0)
    def _():
        if shift:
            w_bf[0:tn - shift, :] = wm_ref[shift:tn, :].astype(BF16)
            w_bf[tn - shift:tn, :] = refs[2][...].astype(BF16)
        else:
            w_bf[...] = wm_ref[...].astype(BF16)

    step = pl.program_id(0) * pl.num_programs(1) + pl.program_id(1)
    for sd, scr in zip(sides, side_scratch):
        if sd.init:
            sd.init(step, scr)

    placers = [idx for idx, sd in enumerate(sides) if sd.places]
    assert len(placers) <= 1
    widths = sides[placers[0]].places if placers else (tn // 2, tn // 2)
    n_parts = len(widths)
    assert sum(widths) == tn and all(w % MXU_WIDTH == 0 for w in widths)

    def emit(k):
        lo = sum(widths[:k])
        cols = slice(lo, lo + widths[k])
        o_ref[:, cols] = _dot_nt(h_ref[...], w_bf[cols, :]).astype(o_ref.dtype)

    for idx, sd in enumerate(sides):
        if not sd.places:
            sd.body(step, side_in[idx], side_out[idx], side_scratch[idx], None, 0)
    if placers:
        idx = placers[0]
        sides[idx].body(step, side_in[idx], side_out[idx], side_scratch[idx], emit, n_parts)
    else:
        for k in range(n_parts):
            emit(k)


def _proj(h, wt, col0, ncols, out_dtype, name, tm=1024, tn=1024, sides=()):
    s, d = h.shape
    shift = col0 % tn
    base = col0 - shift
    n_m = s // tm
    n_steps = (ncols // tn) * n_m
    assert shift % SUBLANES == 0 and ncols % tn == 0
    assert all(sd.steps == n_steps or (sd.clamped and sd.steps < n_steps) for sd in sides)

    def flat(spec):
        return pl.BlockSpec(spec.block, lambda j, m: spec.index(j * n_m + m),
                            pipeline_mode=spec.mode)

    in_specs = [
        pl.BlockSpec((tm, d), lambda j, m: (m, 0)),
        pl.BlockSpec((tn, d), lambda j, m: (base // tn + j, 0)),
    ]
    args = [h, wt]
    if shift:
        in_specs.append(pl.BlockSpec((shift, d), lambda j, m: ((base + (j + 1) * tn) // shift, 0)))
        args.append(wt)
    out_specs = [pl.BlockSpec((tm, tn), lambda j, m: (m, j))]
    out_shapes = [jax.ShapeDtypeStruct((s, ncols), out_dtype)]
    scratch = [pltpu.VMEM((tn, d), BF16)]
    for sd in sides:
        in_specs += [flat(sp) for sp in sd.in_specs]
        args += list(sd.args)
    for sd in sides:
        out_specs += [flat(sp) for sp in sd.out_specs]
        out_shapes += list(sd.out_shapes)
    for sd in sides:
        scratch += list(sd.scratch)
    return pl.pallas_call(
        functools.partial(_proj_kernel, shift=shift, tn=tn, sides=tuple(sides)),
        grid=(ncols // tn, n_m),
        in_specs=in_specs,
        out_specs=out_specs,
        out_shape=out_shapes,
        scratch_shapes=scratch,
        compiler_params=_params(("arbitrary", "arbitrary")),
        name=name,
    )(*args)


def _cast_body(step, in_refs, out_refs, scratch_refs, emit, n_parts):
    del step, scratch_refs, emit, n_parts
    out_refs[0][...] = in_refs[0][...].astype(BF16)


CAST_ROWS = 128


def _cast_side(w, rows=CAST_ROWS):
    n, d = w.shape
    nb = n // rows
    spec = _Spec((rows, d), lambda i: (jnp.minimum(i, nb - 1), 0))
    return _Side(body=_cast_body, init=None, args=(w,), in_specs=(spec,), out_specs=(spec,),
                 out_shapes=(jax.ShapeDtypeStruct((n, d), BF16),), scratch=(), steps=nb,
                 clamped=True)


MLSTM_BLOCK = 256


def _cumsum_lanes(v):
    n = v.shape[-1]
    pos = lax.broadcasted_iota(jnp.int32, v.shape, v.ndim - 1)
    shift = 1
    while shift < n:
        v = v + jnp.where(pos >= shift, pltpu.roll(v, shift, v.ndim - 1), 0.0)
        shift *= 2
    return v


def _mlstm_body(step, in_refs, out_refs, scratch_refs, emit, n_parts):
    pa_ref, gt_ref, cw_ref, cb_ref, gbc_ref, mn_ref = in_refs
    v_off = 2 * M_QK
    o_off = 2 * M_QK + M_V
    (out_ref,) = out_refs
    qk_buf, c_s, m_s = scratch_refs
    del step
    halo = SUBLANES
    tc = MLSTM_BLOCK

    qk_buf[halo:halo + tc, :] = pa_ref[:, 0:2 * M_QK].astype(F32)
    cw_half = cw_ref[...] * 0.5
    t = jnp.broadcast_to(cb_ref[...] * 0.5, (tc, 2 * M_QK))
    for tap in range(M_CONV):
        lo = halo - (M_CONV - 1) + tap
        t = t + qk_buf[lo:lo + tc, :] * cw_half[tap:tap + 1, :]
    qk_buf[0:halo, :] = qk_buf[tc:tc + halo, :]
    qk = t * (1.0 + jnp.tanh(t))
    q_all = qk[:, 0:M_QK].astype(BF16)
    k_all = qk[:, M_QK:2 * M_QK] * (M_QK_DIM ** -0.5)

    zt = gt_ref[...] + gbc_ref[:, 0:1]
    lf_all = _log_sigmoid(zt)
    b_all = _cumsum_lanes(lf_all)

    row = lax.broadcasted_iota(jnp.int32, (tc, tc), 0)
    col = lax.broadcasted_iota(jnp.int32, (tc, tc), 1)
    causal = col <= row
    ones_aug = jnp.ones((tc, LANES), BF16)

    def inter(h):
        c_prev = c_s[h]
        qc = _dot(q_all[:, h * M_QK_DIM:(h + 1) * M_QK_DIM], c_prev.astype(BF16))
        return c_prev, qc

    def first(h, c_prev, qc):
        qcols = slice(h * M_QK_DIM, (h + 1) * M_QK_DIM)
        f = M_HEADS + h
        q_h = q_all[:, qcols]
        kt = k_all[:, qcols].T
        m_prev = m_s[h][0:1, 0:1]
        u = zt[h:h + 1, :] - b_all[f:f + 1, :]
        b_tot = b_all[f:f + 1, tc - 1:tc]
        v_aug = jnp.concatenate([pa_ref[:, v_off + h * M_V_DIM:v_off + (h + 1) * M_V_DIM],
                                 ones_aug], axis=1)
        qk = _dot(q_h, kt.astype(BF16))
        w_log = b_tot + u
        m_loc = jnp.max(w_log, axis=1, keepdims=True)
        wkt = (kt * jnp.exp(w_log - m_loc)).astype(BF16)
        m_new = jnp.maximum(b_tot + m_prev, m_loc)
        decay = jnp.exp(b_tot + m_prev - m_new)
        gain = jnp.exp(m_loc - m_new)
        m_s[h] = jnp.broadcast_to(m_new, (SUBLANES, LANES))
        return (u, m_prev, v_aug, qk, qc), (wkt, v_aug, c_prev, decay, gain)

    def third(h, wkt, v_aug, c_prev, decay, gain):
        c_s[h] = decay * c_prev + gain * _dot(wkt, v_aug)

    def second(h, u, m_prev, v_aug, qk, qc):
        vcols = slice(h * M_V_DIM, (h + 1) * M_V_DIM)
        f = M_HEADS + h
        um = jnp.where(causal, u, NEG)
        g = jnp.maximum(jnp.max(um, axis=1, keepdims=True), m_prev)
        b_col = jnp.sum(jnp.where(causal, lf_all[f:f + 1, :], 0.0), axis=1, keepdims=True)
        p = jnp.exp(um - g) * qk
        na = _dot(p.astype(BF16), v_aug) + jnp.exp(m_prev - g) * qc
        num = na[:, 0:M_V_DIM]
        den = na[:, M_V_DIM:M_V_DIM + 1]
        hh = num / jnp.maximum(jnp.abs(den), jnp.exp(-(b_col + g)))
        o_gate = pa_ref[:, o_off + h * M_V_DIM:o_off + (h + 1) * M_V_DIM].astype(F32)
        hh = hh * _rms_scale(hh) * mn_ref[0:1, vcols] * _sigmoid(o_gate)
        out_ref[:, vcols] = hh.astype(BF16)

    emit(0)
    parts = [first(h, *inter(h)) for h in range(M_HEADS)]
    groups = n_parts - 1
    per = M_HEADS // groups
    for grp in range(groups):
        emit(1 + grp)
        for h in range(grp * per, (grp + 1) * per):
            second(h, *parts[h][0])
    for h in range(M_HEADS):
        third(h, *parts[h][1])
    assert groups * per == M_HEADS


def _mlstm_init(step, scratch_refs):
    qk_buf, c_s, m_s = scratch_refs

    @pl.when(step == 0)
    def _():
        qk_buf[0:SUBLANES, :] = jnp.zeros((SUBLANES, 2 * M_QK), F32)
        c_s[...] = jnp.zeros(c_s.shape, F32)
        m_s[...] = jnp.full(m_s.shape, NEG, F32)


def _mlstm_side(pa, gt, cw, cb, gbc, mn):
    s = pa.shape[0]
    tc = MLSTM_BLOCK
    const = lambda i: (0, 0)
    return _Side(
        body=_mlstm_body,
        init=_mlstm_init,
        args=(pa, gt, cw, cb, gbc, mn),
        in_specs=(
            _Spec((tc, pa.shape[1]), lambda i: (i, 0)),
            _Spec((SUBLANES, tc), lambda i: (0, i)),
            _Spec((M_CONV, 2 * M_QK), const),
            _Spec((1, 2 * M_QK), const),
            _Spec((SUBLANES, LANES), const),
            _Spec((1, M_V), const),
        ),
        out_specs=(_Spec((tc, M_V), lambda i: (i, 0)),),
        out_shapes=(jax.ShapeDtypeStruct((s, M_V), BF16),),
        scratch=(
            pltpu.VMEM((SUBLANES + tc, 2 * M_QK), F32),
            pltpu.VMEM((M_HEADS, M_QK_DIM, M_V_DIM + LANES), F32),
            pltpu.VMEM((M_HEADS, SUBLANES, LANES), F32),
        ),
        steps=s // tc,
        places=(MXU_WIDTH,) * 3,
    )


ATT_G = 4
ATT_ROWS = ATT_G * CHUNK
ATT_KBLOCKS = (A_PAST_CHUNKS + ATT_G) // ATT_G


def _attn_body(step, in_refs, out_refs, scratch_refs, emit, n_parts):
    del step, scratch_refs
    q_ref, k_ref, v_ref, bias_ref = in_refs
    (out_ref,) = out_refs
    scale2 = (A_HEAD_DIM ** -0.5) * LOG2E
    ones = jnp.ones((ATT_KBLOCKS * ATT_ROWS, A_HEAD_DIM), BF16)

    def scores(h):
        cols = slice(h * A_HEAD_DIM, (h + 1) * A_HEAD_DIM)
        return _dot_nt(q_ref[:, cols], k_ref[:, cols])

    def finish(h, s):
        cols = slice(h * A_HEAD_DIM, (h + 1) * A_HEAD_DIM)
        s = s * scale2 + bias_ref[h]
        p = jnp.exp2(s - jnp.max(s, axis=1, keepdims=True)).astype(BF16)
        o = _dot(p, jnp.concatenate([v_ref[:, cols], ones], axis=1))
        out_ref[:, cols] = (o[:, 0:A_HEAD_DIM] / o[:, A_HEAD_DIM:A_HEAD_DIM + 1]).astype(BF16)

    per = A_HEADS // n_parts
    s_next = [scores(h) for h in range(per)]
    for grp in range(n_parts):
        s_grp = s_next
        if grp + 1 < n_parts:
            s_next = [scores(h) for h in range((grp + 1) * per, (grp + 2) * per)]
        emit(grp)
        for h, s in zip(range(grp * per, (grp + 1) * per), s_grp):
            finish(h, s)


def _attn_side(pb, bias):
    s = pb.shape[0]
    last = ATT_KBLOCKS - 1

    def kv(group):
        return _Spec((pl.Element(ATT_KBLOCKS * ATT_ROWS), pl.Element(A_W)),
                     lambda g: (jnp.maximum(g - last, 0) * ATT_ROWS, group * A_W))

    return _Side(
        body=_attn_body,
        init=None,
        args=(pb,) * 3 + (bias,),
        in_specs=(
            _Spec((ATT_ROWS, A_W), lambda g: (g, 0)),
            kv(1), kv(2),
            _Spec((pl.Element(A_HEADS), pl.Element(ATT_ROWS), pl.Element(ATT_KBLOCKS * ATT_ROWS)),
                  lambda g: (0, 0, jnp.maximum(last - g, 0) * ATT_ROWS)),
        ),
        out_specs=(_Spec((ATT_ROWS, A_W), lambda g: (g, 0)),),
        out_shapes=(jax.ShapeDtypeStruct((s, A_W), BF16),),
        scratch=(),
        steps=s // ATT_ROWS,
        places=(MXU_WIDTH,) * 4,
    )


def _band_bias(rel_table):
    hds = rel_table.shape[0]
    wcols = ATT_KBLOCKS * ATT_ROWS
    band_w = (A_PAST_CHUNKS + 1) * CHUNK
    far = A_PAST_CHUNKS * CHUNK - A_MAX_REL + CHUNK
    tab = rel_table.astype(F32) * LOG2E
    e = jnp.concatenate([jnp.broadcast_to(tab[:, 2 * A_MAX_REL:], (hds, far)),
                         tab[:, A_MAX_REL - CHUNK + 1:2 * A_MAX_REL][:, ::-1],
                         jnp.zeros((hds, 1), F32)], axis=1)
    period = e.shape[1]
    tiled = jnp.broadcast_to(e[:, None, :], (hds, CHUNK, period)).reshape(hds, -1)
    skew = tiled[:, :CHUNK * (period - 1)].reshape(hds, CHUNK, period - 1)
    chunk_bias = skew[:, :, CHUNK - 1:CHUNK - 1 + band_w]
    total = wcols + (ATT_KBLOCKS - 1) * ATT_ROWS
    rows = [jnp.pad(chunk_bias, ((0, 0), (0, 0), (ci * CHUNK, total - band_w - ci * CHUNK)),
                    constant_values=NEG) for ci in range(ATT_G)]
    return jnp.concatenate(rows, axis=1)


def _merge_kernel(hm_ref, ha_ref, gm_ref, ga_ref, x_ref, wm_ref, wa_ref, wo_ref, gb_ref, nf_ref,
                  x1_ref, h2_ref):
    ym = _dot(hm_ref[...], wm_ref[...])
    ya = _dot(ha_ref[...], wa_ref[...])
    merged = (_sigmoid(gm_ref[...].astype(F32) + gb_ref[0:1, :]) * ym
              + _sigmoid(ga_ref[...].astype(F32) + gb_ref[1:2, :]) * ya)
    x1 = x_ref[...] + _dot(merged.astype(BF16), wo_ref[...])
    x1_ref[...] = x1
    h2_ref[...] = (x1 * _rms_scale(x1) * nf_ref[...]).astype(BF16)


def _merge(hm, ha, pc, x, wm, wa, wo, gb, nf, tm=256):
    s, d = x.shape
    const = lambda i: (0, 0)
    resident = pl.Buffered(1)
    return pl.pallas_call(
        _merge_kernel,
        grid=(s // tm,),
        in_specs=[
            pl.BlockSpec((tm, M_V), lambda i: (i, 0)),
            pl.BlockSpec((tm, A_W), lambda i: (i, 0)),
            pl.BlockSpec((tm, d), lambda i: (i, 0)),
            pl.BlockSpec((tm, d), lambda i: (i, 1)),
            pl.BlockSpec((tm, d), lambda i: (i, 0)),
            pl.BlockSpec((M_V, d), const, pipeline_mode=resident),
            pl.BlockSpec((A_W, d), const, pipeline_mode=resident),
            pl.BlockSpec((d, d), const, pipeline_mode=resident),
            pl.BlockSpec((2, d), const),
            pl.BlockSpec((1, d), const),
        ],
        out_specs=[
            pl.BlockSpec((tm, d), lambda i: (i, 0)),
            pl.BlockSpec((tm, d), lambda i: (i, 0)),
        ],
        out_shape=[
            jax.ShapeDtypeStruct((s, d), F32),
            jax.ShapeDtypeStruct((s, d), BF16),
        ],
        compiler_params=_params(("arbitrary",)),
        name="merge",
    )(hm, ha, pc, pc, x, wm, wa, wo, gb, nf)


FFN_SUB_ROWS = 512


def _ffn_up_kernel(h_ref, wg_ref, wv_ref, cw_ref, cb_ref, wd_ref, a_ref, wd_bf_ref, w_bf, ug_buf,
                   *, tm, tf):
    halo = SUBLANES
    rs = FFN_SUB_ROWS
    wd_bf_ref[...] = wd_ref[...].astype(BF16)

    @pl.when(pl.program_id(1) == 0)
    def _():
        w_bf[:, 0:tf] = wg_ref[...].astype(BF16)
        w_bf[:, tf:2 * tf] = wv_ref[...].astype(BF16)
        ug_buf[0:halo, :] = jnp.zeros((halo, tf), F32)

    for r in range(tm // rs):
        h = h_ref[r * rs:(r + 1) * rs, :]
        u = _dot(h, w_bf[...])
        ug = u[:, 0:tf]
        uv = u[:, tf:2 * tf]
        base = halo + r * rs
        ug_buf[base:base + rs, :] = ug
        conv = cb_ref[...] + ug * cw_ref[FFN_CONV - 1:FFN_CONV, :]
        for tap in range(FFN_CONV - 1):
            lo = base - (FFN_CONV - 1) + tap
            conv = conv + ug_buf[lo:lo + rs, :] * cw_ref[tap:tap + 1, :]
        a_ref[r * rs:(r + 1) * rs, :] = (conv * _sigmoid(conv) * uv).astype(BF16)
    ug_buf[0:halo, :] = ug_buf[tm:tm + halo, :]


def _ffn_up(h2, w_up, cw, cb, w_down, tm=2048, tf=512):
    s, d = h2.shape
    nf = D_FF // tf
    n_m = s // tm
    wd_rows = D_FF // (nf * n_m)
    assert wd_rows * nf * n_m == D_FF and wd_rows % (2 * SUBLANES) == 0
    wd_spec = pl.BlockSpec((wd_rows, d), lambda j, m: (j * n_m + m, 0))
    return pl.pallas_call(
        functools.partial(_ffn_up_kernel, tm=tm, tf=tf),
        grid=(nf, n_m),
        in_specs=[
            pl.BlockSpec((tm, d), lambda j, m: (m, 0)),
            pl.BlockSpec((d, tf), lambda j, m: (0, j)),
            pl.BlockSpec((d, tf), lambda j, m: (0, nf + j)),
            pl.BlockSpec((FFN_CONV, tf), lambda j, m: (0, j)),
            pl.BlockSpec((1, tf), lambda j, m: (0, j)),
            wd_spec,
        ],
        out_specs=[pl.BlockSpec((tm, tf), lambda j, m: (m, j)), wd_spec],
        out_shape=[jax.ShapeDtypeStruct((s, D_FF), BF16),
                   jax.ShapeDtypeStruct((D_FF, d), BF16)],
        scratch_shapes=[
            pltpu.VMEM((d, 2 * tf), BF16),
            pltpu.VMEM((SUBLANES + tm, tf), F32),
        ],
        compiler_params=_params(("arbitrary", "arbitrary")),
        name="ffn_up",
    )(h2, w_up, w_up, cw, cb, w_down)


FFN_DOWN_COLS = 512


def _ffn_down_kernel(a_ref, w_ref, x1_ref, nf_ref, o_ref):
    d = o_ref.shape[1]
    ssq = jnp.zeros((o_ref.shape[0], 1), F32)
    for n in range(d // FFN_DOWN_COLS):
        cols = slice(n * FFN_DOWN_COLS, (n + 1) * FFN_DOWN_COLS)
        y = x1_ref[:, cols] + _dot(a_ref[...], w_ref[:, cols])
        o_ref[:, cols] = y
        ssq = ssq + jnp.sum(y * y, axis=-1, keepdims=True)
    o_ref[...] = o_ref[...] * lax.rsqrt(ssq * (1.0 / d) + EPS) * nf_ref[...]


def _ffn_down(a, w_down, x1, nf, tm=512):
    s, d = x1.shape
    dff = a.shape[1]
    return pl.pallas_call(
        _ffn_down_kernel,
        grid=(s // tm,),
        in_specs=[
            pl.BlockSpec((tm, dff), lambda i: (i, 0)),
            pl.BlockSpec((dff, d), lambda i: (0, 0), pipeline_mode=pl.Buffered(1)),
            pl.BlockSpec((tm, d), lambda i: (i, 0)),
            pl.BlockSpec((1, d), lambda i: (0, 0)),
        ],
        out_specs=pl.BlockSpec((tm, d), lambda i: (i, 0)),
        out_shape=jax.ShapeDtypeStruct((s, d), F32),
        compiler_params=_params(("arbitrary",)),
        name="ffn_down",
    )(a, w_down, x1, nf)


def kernel(x, norm_mix, w_in, conv_qk_w, conv_qk_b, b_igate, b_fgate, m_norm, rel_bias, gate_bias,
           w_branch_m, w_branch_a, w_out, norm_ffn, w_up, conv_ffn_w, conv_ffn_b, w_down, norm_final):
    batch, seq, d = x.shape
    depth = w_in.shape[0]
    assert (batch, seq, d, depth) == (1, SEQ, D_MODEL, 1)
    xs = x[0]
    l = 0

    w_in_t = jnp.swapaxes(w_in[l], 0, 1)
    w_g = jnp.pad(w_in_t[GATE_COL0:GATE_COL0 + N_GATES, :].T, ((0, 0), (0, LANES - N_GATES)))
    gate_b = jnp.concatenate([b_igate[l], b_fgate[l]])
    gb_col = jnp.broadcast_to(gate_b[:, None], (SUBLANES, LANES))
    bias = _band_bias(rel_bias[l])

    h, gt = _prenorm(xs, norm_mix[l][None, :], w_g.astype(BF16))
    na = 2 * M_QK + 2 * M_V
    attn0 = GATE_COL0 + N_GATES
    pa, w_m_bf, w_a_bf = _proj(h, w_in_t, 0, na, BF16, "proj_mlstm", tm=2048,
                               sides=[_cast_side(w_branch_m[l]), _cast_side(w_branch_a[l])])
    mlstm = _mlstm_side(pa, gt, conv_qk_w[l], conv_qk_b[l][None, :], gb_col, m_norm[l][None, :])
    pb, hm, w_out_bf = _proj(h, w_in_t, attn0, 3 * A_W, BF16, "proj_attn_mlstm", tn=768,
                             sides=[mlstm, _cast_side(w_out[l])])
    pc, ha = _proj(h, w_in_t, attn0 + 3 * A_W, 2 * D_MODEL, BF16, "proj_gates_attn",
                   sides=[_attn_side(pb, bias)])

    x1, h2 = _merge(hm, ha, pc, xs, w_m_bf, w_a_bf, w_out_bf, gate_bias[l], norm_ffn[l][None, :])
    a, w_down_bf = _ffn_up(h2, w_up[l], conv_ffn_w[l], conv_ffn_b[l][None, :], w_down[l])
    out = _ffn_down(a, w_down_bf, x1, norm_final[None, :])
    return out[None]
```

```python
import functools
import math
from typing import Callable, NamedTuple

import jax
import jax.numpy as jnp
from jax import lax
from jax.experimental import pallas as pl
from jax.experimental.pallas import tpu as pltpu

D_MODEL = 2048
SEQ = 8192
CHUNK = 64
M_HEADS = 4
M_QK_DIM = 128
M_V_DIM = 256
M_CONV = 4
A_HEADS = 8
A_HEAD_DIM = 128
A_PAST_CHUNKS = 8
A_MAX_REL = 128
D_FF = 5632
FFN_CONV = 3
EPS = 1e-6
NEG = -1e30
LOG2E = math.log2(math.e)

M_QK = M_HEADS * M_QK_DIM
M_V = M_HEADS * M_V_DIM
A_W = A_HEADS * A_HEAD_DIM
GATE_COL0 = 2 * M_QK + 2 * M_V
N_GATES = 2 * M_HEADS

LANES = 128
SUBLANES = 8
MXU_WIDTH = 256
VMEM_LIMIT = 56 * 1024 * 1024

BF16 = jnp.bfloat16
F32 = jnp.float32


def _params(sem):
    return pltpu.CompilerParams(dimension_semantics=sem, vmem_limit_bytes=VMEM_LIMIT)


def _dot(a, b):
    return jnp.dot(a, b, preferred_element_type=F32)


def _dot_nt(a, b):
    return lax.dot_general(a, b, (((1,), (1,)), ((), ())), preferred_element_type=F32)


def _sigmoid(z):
    return 0.5 * jnp.tanh(0.5 * z) + 0.5


def _log_sigmoid(z):
    return jnp.minimum(z, 0.0) - jnp.log(1.0 + jnp.exp(-jnp.abs(z)))


def _rms_scale(y):
    return lax.rsqrt(jnp.mean(y * y, axis=-1, keepdims=True) + EPS)


def _prenorm_kernel(x_ref, g_ref, wg_ref, h_ref, gt_ref):
    xf = x_ref[...]
    hb = (xf * _rms_scale(xf) * g_ref[...]).astype(BF16)
    h_ref[...] = hb
    gcol = _dot(hb, wg_ref[...])
    gt_ref[...] = gcol.T[0:SUBLANES, :]


def _prenorm(x, g, wg, tm=1024):
    s, d = x.shape
    return pl.pallas_call(
        _prenorm_kernel,
        grid=(s // tm,),
        in_specs=[
            pl.BlockSpec((tm, d), lambda i: (i, 0)),
            pl.BlockSpec((1, d), lambda i: (0, 0)),
            pl.BlockSpec((d, LANES), lambda i: (0, 0)),
        ],
        out_specs=[
            pl.BlockSpec((tm, d), lambda i: (i, 0)),
            pl.BlockSpec((SUBLANES, tm), lambda i: (0, i)),
        ],
        out_shape=[
            jax.ShapeDtypeStruct((s, d), BF16),
            jax.ShapeDtypeStruct((SUBLANES, s), F32),
        ],
        compiler_params=_params(("arbitrary",)),
        name="prenorm",
    )(x, g, wg)


class _Spec(NamedTuple):
    block: tuple
    index: Callable
    mode: object = None


class _Side(NamedTuple):
    body: Callable
    init: Callable
    args: tuple
    in_specs: tuple
    out_specs: tuple
    out_shapes: tuple
    scratch: tuple
    steps: int
    clamped: bool = False
    places: tuple = ()


def _proj_kernel(*refs, shift, tn, sides):
    n_w = 3 if shift else 2
    h_ref, wm_ref = refs[0], refs[1]
    pos = n_w
    side_in = []
    for sd in sides:
        side_in.append(refs[pos:pos + len(sd.args)])
        pos += len(sd.args)
    o_ref = refs[pos]
    pos += 1
    side_out = []
    for sd in sides:
        side_out.append(refs[pos:pos + len(sd.out_specs)])
        pos += len(sd.out_specs)
    w_bf = refs[pos]
    pos += 1
    side_scratch = []
    for sd in sides:
        side_scratch.append(refs[pos:pos + len(sd.scratch)])
        pos += len(sd.scratch)

    @pl.when(pl.program_id(1) == 0)
    def _():
        if shift:
            w_bf[0:tn - shift, :] = wm_ref[shift:tn, :].astype(BF16)
            w_bf[tn - shift:tn, :] = refs[2][...].astype(BF16)
        else:
            w_bf[...] = wm_ref[...].astype(BF16)

    step = pl.program_id(0) * pl.num_programs(1) + pl.program_id(1)
    for sd, scr in zip(sides, side_scratch):
        if sd.init:
            sd.init(step, scr)

    placers = [idx for idx, sd in enumerate(sides) if sd.places]
    assert len(placers) <= 1
    widths = sides[placers[0]].places if placers else (tn // 2, tn // 2)
    n_parts = len(widths)
    assert sum(widths) == tn and all(w % MXU_WIDTH == 0 for w in widths)

    def emit(k):
        lo = sum(widths[:k])
        cols = slice(lo, lo + widths[k])
        o_ref[:, cols] = _dot_nt(h_ref[...], w_bf[cols, :]).astype(o_ref.dtype)

    for idx, sd in enumerate(sides):
        if not sd.places:
            sd.body(step, side_in[idx], side_out[idx], side_scratch[idx], None, 0)
    if placers:
        idx = placers[0]
        sides[idx].body(step, side_in[idx], side_out[idx], side_scratch[idx], emit, n_parts)
    else:
        for k in range(n_parts):
            emit(k)


def _proj(h, wt, col0, ncols, out_dtype, name, tm=1024, tn=1024, sides=()):
    s, d = h.shape
    shift = col0 % tn
    base = col0 - shift
    n_m = s // tm
    n_steps = (ncols // tn) * n_m
    assert shift % SUBLANES == 0 and ncols % tn == 0
    assert all(sd.steps == n_steps or (sd.clamped and sd.steps < n_steps) for sd in sides)

    def flat(spec):
        return pl.BlockSpec(spec.block, lambda j, m: spec.index(j * n_m + m),
                            pipeline_mode=spec.mode)

    in_specs = [
        pl.BlockSpec((tm, d), lambda j, m: (m, 0)),
        pl.BlockSpec((tn, d), lambda j, m: (base // tn + j, 0)),
    ]
    args = [h, wt]
    if shift:
        in_specs.append(pl.BlockSpec((shift, d), lambda j, m: ((base + (j + 1) * tn) // shift, 0)))
        args.append(wt)
    out_specs = [pl.BlockSpec((tm, tn), lambda j, m: (m, j))]
    out_shapes = [jax.ShapeDtypeStruct((s, ncols), out_dtype)]
    scratch = [pltpu.VMEM((tn, d), BF16)]
    for sd in sides:
        in_specs += [flat(sp) for sp in sd.in_specs]
        args += list(sd.args)
    for sd in sides:
        out_specs += [flat(sp) for sp in sd.out_specs]
        out_shapes += list(sd.out_shapes)
    for sd in sides:
        scratch += list(sd.scratch)
    return pl.pallas_call(
        functools.partial(_proj_kernel, shift=shift, tn=tn, sides=tuple(sides)),
        grid=(ncols // tn, n_m),
        in_specs=in_specs,
        out_specs=out_specs,
        out_shape=out_shapes,
        scratch_shapes=scratch,
        compiler_params=_params(("arbitrary", "arbitrary")),
        name=name,
    )(*args)


def _cast_body(step, in_refs, out_refs, scratch_refs, emit, n_parts):
    del step, scratch_refs, emit, n_parts
    out_refs[0][...] = in_refs[0][...].astype(BF16)


CAST_ROWS = 128


def _cast_side(w, rows=CAST_ROWS):
    n, d = w.shape
    nb = n // rows
    spec = _Spec((rows, d), lambda i: (jnp.minimum(i, nb - 1), 0))
    return _Side(body=_cast_body, init=None, args=(w,), in_specs=(spec,), out_specs=(spec,),
                 out_shapes=(jax.ShapeDtypeStruct((n, d), BF16),), scratch=(), steps=nb,
                 clamped=True)


MLSTM_BLOCK = 256


def _cumsum_lanes(v):
    n = v.shape[-1]
    pos = lax.broadcasted_iota(jnp.int32, v.shape, v.ndim - 1)
    shift = 1
    while shift < n:
        v = v + jnp.where(pos >= shift, pltpu.roll(v, shift, v.ndim - 1), 0.0)
        shift *= 2
    return v


def _mlstm_body(step, in_refs, out_refs, scratch_refs, emit, n_parts):
    pa_ref, gt_ref, cw_ref, cb_ref, gbc_ref, mn_ref = in_refs
    v_off = 2 * M_QK
    o_off = 2 * M_QK + M_V
    (out_ref,) = out_refs
    qk_buf, c_s, m_s = scratch_refs
    del step
    halo = SUBLANES
    tc = MLSTM_BLOCK

    qk_buf[halo:halo + tc, :] = pa_ref[:, 0:2 * M_QK].astype(F32)
    cw_half = cw_ref[...] * 0.5
    t = jnp.broadcast_to(cb_ref[...] * 0.5, (tc, 2 * M_QK))
    for tap in range(M_CONV):
        lo = halo - (M_CONV - 1) + tap
        t = t + qk_buf[lo:lo + tc, :] * cw_half[tap:tap + 1, :]
    qk_buf[0:halo, :] = qk_buf[tc:tc + halo, :]
    qk = t * (1.0 + jnp.tanh(t))
    q_all = qk[:, 0:M_QK].astype(BF16)
    k_all = qk[:, M_QK:2 * M_QK] * (M_QK_DIM ** -0.5)

    zt = gt_ref[...] + gbc_ref[:, 0:1]
    lf_all = _log_sigmoid(zt)
    b_all = _cumsum_lanes(lf_all)

    row = lax.broadcasted_iota(jnp.int32, (tc, tc), 0)
    col = lax.broadcasted_iota(jnp.int32, (tc, tc), 1)
    causal = col <= row
    ones_aug = jnp.ones((tc, LANES), BF16)

    def inter(h):
        c_prev = c_s[h]
        qc = _dot(q_all[:, h * M_QK_DIM:(h + 1) * M_QK_DIM], c_prev.astype(BF16))
        return c_prev, qc

    def first(h, c_prev, qc):
        qcols = slice(h * M_QK_DIM, (h + 1) * M_QK_DIM)
        f = M_HEADS + h
        q_h = q_all[:, qcols]
        kt = k_all[:, qcols].T
        m_prev = m_s[h][0:1, 0:1]
        u = zt[h:h + 1, :] - b_all[f:f + 1, :]
        b_tot = b_all[f:f + 1, tc - 1:tc]
        v_aug = jnp.concatenate([pa_ref[:, v_off + h * M_V_DIM:v_off + (h + 1) * M_V_DIM],
                                 ones_aug], axis=1)
        qk = _dot(q_h, kt.astype(BF16))
        w_log = b_tot + u
        m_loc = jnp.max(w_log, axis=1, keepdims=True)
        wkt = (kt * jnp.exp(w_log - m_loc)).astype(BF16)
        m_new = jnp.maximum(b_tot + m_prev, m_loc)
        decay = jnp.exp(b_tot + m_prev - m_new)
        gain = jnp.exp(m_loc - m_new)
        m_s[h] = jnp.broadcast_to(m_new, (SUBLANES, LANES))
        return (u, m_prev, v_aug, qk, qc), (wkt, v_aug, c_prev, decay, gain)

    def third(h, wkt, v_aug, c_prev, decay, gain):
        c_s[h] = decay * c_prev + gain * _dot(wkt, v_aug)

    def second(h, u, m_prev, v_aug, qk, qc):
        vcols = slice(h * M_V_DIM, (h + 1) * M_V_DIM)
        f = M_HEADS + h
        um = jnp.where(causal, u, NEG)
        g = jnp.maximum(jnp.max(um, axis=1, keepdims=True), m_prev)
        b_col = jnp.sum(jnp.where(causal, lf_all[f:f + 1, :], 0.0), axis=1, keepdims=True)
        p = jnp.exp(um - g) * qk
        na = _dot(p.astype(BF16), v_aug) + jnp.exp(m_prev - g) * qc
        num = na[:, 0:M_V_DIM]
        den = na[:, M_V_DIM:M_V_DIM + 1]
        hh = num / jnp.maximum(jnp.abs(den), jnp.exp(-(b_col + g)))
        o_gate = pa_ref[:, o_off + h * M_V_DIM:o_off + (h + 1) * M_V_DIM].astype(F32)
        hh = hh * _rms_scale(hh) * mn_ref[0:1, vcols] * _sigmoid(o_gate)
        out_ref[:, vcols] = hh.astype(BF16)

    emit(0)
    parts = [first(h, *inter(h)) for h in range(M_HEADS)]
    groups = n_parts - 1
    per = M_HEADS // groups
    for grp in range(groups):
        emit(1 + grp)
        for h in range(grp * per, (grp + 1) * per):
            second(h, *parts[h][0])
    for h in range(M_HEADS):
        third(h, *parts[h][1])
    assert groups * per == M_HEADS


def _mlstm_init(step, scratch_refs):
    qk_buf, c_s, m_s = scratch_refs

    @pl.when(step == 0)
    def _():
        qk_buf[0:SUBLANES, :] = jnp.zeros((SUBLANES, 2 * M_QK), F32)
        c_s[...] = jnp.zeros(c_s.shape, F32)
        m_s[...] = jnp.full(m_s.shape, NEG, F32)


def _mlstm_side(pa, gt, cw, cb, gbc, mn):
    s = pa.shape[0]
    tc = MLSTM_BLOCK
    const = lambda i: (0, 0)
    return _Side(
        body=_mlstm_body,
        init=_mlstm_init,
        args=(pa, gt, cw, cb, gbc, mn),
        in_specs=(
            _Spec((tc, pa.shape[1]), lambda i: (i, 0)),
            _Spec((SUBLANES, tc), lambda i: (0, i)),
            _Spec((M_CONV, 2 * M_QK), const),
            _Spec((1, 2 * M_QK), const),
            _Spec((SUBLANES, LANES), const),
            _Spec((1, M_V), const),
        ),
        out_specs=(_Spec((tc, M_V), lambda i: (i, 0)),),
        out_shapes=(jax.ShapeDtypeStruct((s, M_V), BF16),),
        scratch=(
            pltpu.VMEM((SUBLANES + tc, 2 * M_QK), F32),
            pltpu.VMEM((M_HEADS, M_QK_DIM, M_V_DIM + LANES), F32),
            pltpu.VMEM((M_HEADS, SUBLANES, LANES), F32),
        ),
        steps=s // tc,
        places=(MXU_WIDTH,) * 3,
    )


ATT_G = 4
ATT_ROWS = ATT_G * CHUNK
ATT_KBLOCKS = (A_PAST_CHUNKS + ATT_G) // ATT_G


def _attn_body(step, in_refs, out_refs, scratch_refs, emit, n_parts):
    del step, scratch_refs
    q_ref, k_ref, v_ref, bias_ref = in_refs
    (out_ref,) = out_refs
    scale2 = (A_HEAD_DIM ** -0.5) * LOG2E
    ones = jnp.ones((ATT_KBLOCKS * ATT_ROWS, A_HEAD_DIM), BF16)

    def scores(h):
        cols = slice(h * A_HEAD_DIM, (h + 1) * A_HEAD_DIM)
        return _dot_nt(q_ref[:, cols], k_ref[:, cols])

    def finish(h, s):
        cols = slice(h * A_HEAD_DIM, (h + 1) * A_HEAD_DIM)
        s = s * scale2 + bias_ref[h]
        p = jnp.exp2(s - jnp.max(s, axis=1, keepdims=True)).astype(BF16)
        o = _dot(p, jnp.concatenate([v_ref[:, cols], ones], axis=1))
        out_ref[:, cols] = (o[:, 0:A_HEAD_DIM] / o[:, A_HEAD_DIM:A_HEAD_DIM + 1]).astype(BF16)

    per = A_HEADS // n_parts
    s_next = [scores(h) for h in range(per)]
    for grp in range(n_parts):
        s_grp = s_next
        if grp + 1 < n_parts:
            s_next = [scores(h) for h in range((grp + 1) * per, (grp + 2) * per)]
        emit(grp)
        for h, s in zip(range(grp * per, (grp + 1) * per), s_grp):
            finish(h, s)


def _attn_side(pb, bias):
    s = pb.shape[0]
    last = ATT_KBLOCKS - 1

    def kv(group):
        return _Spec((pl.Element(ATT_KBLOCKS * ATT_ROWS), pl.Element(A_W)),
                     lambda g: (jnp.maximum(g - last, 0) * ATT_ROWS, group * A_W))

    return _Side(
        body=_attn_body,
        init=None,
        args=(pb,) * 3 + (bias,),
        in_specs=(
            _Spec((ATT_ROWS, A_W), lambda g: (g, 0)),
            kv(1), kv(2),
            _Spec((pl.Element(A_HEADS), pl.Element(ATT_ROWS), pl.Element(ATT_KBLOCKS * ATT_ROWS)),
                  lambda g: (0, 0, jnp.maximum(last - g, 0) * ATT_ROWS)),
        ),
        out_specs=(_Spec((ATT_ROWS, A_W), lambda g: (g, 0)),),
        out_shapes=(jax.ShapeDtypeStruct((s, A_W), BF16),),
        scratch=(),
        steps=s // ATT_ROWS,
        places=(MXU_WIDTH,) * 4,
    )


def _band_bias(rel_table):
    hds = rel_table.shape[0]
    wcols = ATT_KBLOCKS * ATT_ROWS
    band_w = (A_PAST_CHUNKS + 1) * CHUNK
    far = A_PAST_CHUNKS * CHUNK - A_MAX_REL + CHUNK
    tab = rel_table.astype(F32) * LOG2E
    e = jnp.concatenate([jnp.broadcast_to(tab[:, 2 * A_MAX_REL:], (hds, far)),
                         tab[:, A_MAX_REL - CHUNK + 1:2 * A_MAX_REL][:, ::-1],
                         jnp.zeros((hds, 1), F32)], axis=1)
    period = e.shape[1]
    tiled = jnp.broadcast_to(e[:, None, :], (hds, CHUNK, period)).reshape(hds, -1)
    skew = tiled[:, :CHUNK * (period - 1)].reshape(hds, CHUNK, period - 1)
    chunk_bias = skew[:, :, CHUNK - 1:CHUNK - 1 + band_w]
    total = wcols + (ATT_KBLOCKS - 1) * ATT_ROWS
    rows = [jnp.pad(chunk_bias, ((0, 0), (0, 0), (ci * CHUNK, total - band_w - ci * CHUNK)),
                    constant_values=NEG) for ci in range(ATT_G)]
    return jnp.concatenate(rows, axis=1)


def _merge_kernel(hm_ref, ha_ref, gm_ref, ga_ref, x_ref, wm_ref, wa_ref, wo_ref, gb_ref, nf_ref,
                  x1_ref, h2_ref):
    ym = _dot(hm_ref[...], wm_ref[...])
    ya = _dot(ha_ref[...], wa_ref[...])
    merged = (_sigmoid(gm_ref[...].astype(F32) + gb_ref[0:1, :]) * ym
              + _sigmoid(ga_ref[...].astype(F32) + gb_ref[1:2, :]) * ya)
    x1 = x_ref[...] + _dot(merged.astype(BF16), wo_ref[...])
    x1_ref[...] = x1
    h2_ref[...] = (x1 * _rms_scale(x1) * nf_ref[...]).astype(BF16)


def _merge(hm, ha, pc, x, wm, wa, wo, gb, nf, tm=256):
    s, d = x.shape
    const = lambda i: (0, 0)
    resident = pl.Buffered(1)
    return pl.pallas_call(
        _merge_kernel,
        grid=(s // tm,),
        in_specs=[
            pl.BlockSpec((tm, M_V), lambda i: (i, 0)),
            pl.BlockSpec((tm, A_W), lambda i: (i, 0)),
            pl.BlockSpec((tm, d), lambda i: (i, 0)),
            pl.BlockSpec((tm, d), lambda i: (i, 1)),
            pl.BlockSpec((tm, d), lambda i: (i, 0)),
            pl.BlockSpec((M_V, d), const, pipeline_mode=resident),
            pl.BlockSpec((A_W, d), const, pipeline_mode=resident),
            pl.BlockSpec((d, d), const, pipeline_mode=resident),
            pl.BlockSpec((2, d), const),
            pl.BlockSpec((1, d), const),
        ],
        out_specs=[
            pl.BlockSpec((tm, d), lambda i: (i, 0)),
            pl.BlockSpec((tm, d), lambda i: (i, 0)),
        ],
        out_shape=[
            jax.ShapeDtypeStruct((s, d), F32),
            jax.ShapeDtypeStruct((s, d), BF16),
        ],
        compiler_params=_params(("arbitrary",)),
        name="merge",
    )(hm, ha, pc, pc, x, wm, wa, wo, gb, nf)


FFN_SUB_ROWS = 512


def _ffn_up_kernel(h_ref, wg_ref, wv_ref, cw_ref, cb_ref, wd_ref, a_ref, wd_bf_ref, w_bf, ug_buf,
                   *, tm, tf):
    halo = SUBLANES
    rs = FFN_SUB_ROWS
    wd_bf_ref[...] = wd_ref[...].astype(BF16)

    @pl.when(pl.program_id(1) == 0)
    def _():
        w_bf[:, 0:tf] = wg_ref[...].astype(BF16)
        w_bf[:, tf:2 * tf] = wv_ref[...].astype(BF16)
        ug_buf[0:halo, :] = jnp.zeros((halo, tf), F32)

    for r in range(tm // rs):
        h = h_ref[r * rs:(r + 1) * rs, :]
        u = _dot(h, w_bf[...])
        ug = u[:, 0:tf]
        uv = u[:, tf:2 * tf]
        base = halo + r * rs
        ug_buf[base:base + rs, :] = ug
        conv = cb_ref[...] + ug * cw_ref[FFN_CONV - 1:FFN_CONV, :]
        for tap in range(FFN_CONV - 1):
            lo = base - (FFN_CONV - 1) + tap
            conv = conv + ug_buf[lo:lo + rs, :] * cw_ref[tap:tap + 1, :]
        a_ref[r * rs:(r + 1) * rs, :] = (conv * _sigmoid(conv) * uv).astype(BF16)
    ug_buf[0:halo, :] = ug_buf[tm:tm + halo, :]


def _ffn_up(h2, w_up, cw, cb, w_down, tm=2048, tf=512):
    s, d = h2.shape
    nf = D_FF // tf
    n_m = s // tm
    wd_rows = D_FF // (nf * n_m)
    assert wd_rows * nf * n_m == D_FF and wd_rows % (2 * SUBLANES) == 0
    wd_spec = pl.BlockSpec((wd_rows, d), lambda j, m: (j * n_m + m, 0))
    return pl.pallas_call(
        functools.partial(_ffn_up_kernel, tm=tm, tf=tf),
        grid=(nf, n_m),
        in_specs=[
            pl.BlockSpec((tm, d), lambda j, m: (m, 0)),
            pl.BlockSpec((d, tf), lambda j, m: (0, j)),
            pl.BlockSpec((d, tf), lambda j, m: (0, nf + j)),
            pl.BlockSpec((FFN_CONV, tf), lambda j, m: (0, j)),
            pl.BlockSpec((1, tf), lambda j, m: (0, j)),
            wd_spec,
        ],
        out_specs=[pl.BlockSpec((tm, tf), lambda j, m: (m, j)), wd_spec],
        out_shape=[jax.ShapeDtypeStruct((s, D_FF), BF16),
                   jax.ShapeDtypeStruct((D_FF, d), BF16)],
        scratch_shapes=[
            pltpu.VMEM((d, 2 * tf), BF16),
            pltpu.VMEM((SUBLANES + tm, tf), F32),
        ],
        compiler_params=_params(("arbitrary", "arbitrary")),
        name="ffn_up",
    )(h2, w_up, w_up, cw, cb, w_down)


FFN_DOWN_COLS = 512


def _ffn_down_kernel(a_ref, w_hbm, x1_ref, nf_ref, o_ref, w_s, w_sem):
    d = o_ref.shape[1]
    n_blk = d // FFN_DOWN_COLS

    def w_copy(n):
        cols = slice(n * FFN_DOWN_COLS, (n + 1) * FFN_DOWN_COLS)
        return pltpu.make_async_copy(w_hbm.at[:, cols], w_s.at[n], w_sem.at[n])

    def body(first_step):
        if first_step:
            for n in range(n_blk):
                w_copy(n).start()
        ssq = jnp.zeros((o_ref.shape[0], 1), F32)
        for n in range(n_blk):
            cols = slice(n * FFN_DOWN_COLS, (n + 1) * FFN_DOWN_COLS)
            if first_step:
                w_copy(n).wait()
            y = x1_ref[:, cols] + _dot(a_ref[...], w_s[n])
            o_ref[:, cols] = y
            ssq = ssq + jnp.sum(y * y, axis=-1, keepdims=True)
        o_ref[...] = o_ref[...] * lax.rsqrt(ssq * (1.0 / d) + EPS) * nf_ref[...]

    @pl.when(pl.program_id(0) == 0)
    def _():
        body(True)

    @pl.when(pl.program_id(0) > 0)
    def _():
        body(False)


def _ffn_down(a, w_down, x1, nf, tm=512):
    s, d = x1.shape
    dff = a.shape[1]
    n_blk = d // FFN_DOWN_COLS
    return pl.pallas_call(
        _ffn_down_kernel,
        grid=(s // tm,),
        in_specs=[
            pl.BlockSpec((tm, dff), lambda i: (i, 0)),
            pl.BlockSpec(memory_space=pl.ANY),
            pl.BlockSpec((tm, d), lambda i: (i, 0)),
            pl.BlockSpec((1, d), lambda i: (0, 0)),
        ],
        out_specs=pl.BlockSpec((tm, d), lambda i: (i, 0)),
        out_shape=jax.ShapeDtypeStruct((s, d), F32),
        scratch_shapes=[
            pltpu.VMEM((n_blk, dff, FFN_DOWN_COLS), BF16),
            pltpu.SemaphoreType.DMA((n_blk,)),
        ],
        compiler_params=_params(("arbitrary",)),
        name="ffn_down",
    )(a, w_down, x1, nf)


def kernel(x, norm_mix, w_in, conv_qk_w, conv_qk_b, b_igate, b_fgate, m_norm, rel_bias, gate_bias,
           w_branch_m, w_branch_a, w_out, norm_ffn, w_up, conv_ffn_w, conv_ffn_b, w_down, norm_final):
    batch, seq, d = x.shape
    depth = w_in.shape[0]
    assert (batch, seq, d, depth) == (1, SEQ, D_MODEL, 1)
    xs = x[0]
    l = 0

    w_in_t = jnp.swapaxes(w_in[l], 0, 1)
    w_g = jnp.pad(w_in_t[GATE_COL0:GATE_COL0 + N_GATES, :].T, ((0, 0), (0, LANES - N_GATES)))
    gate_b = jnp.concatenate([b_igate[l], b_fgate[l]])
    gb_col = jnp.broadcast_to(gate_b[:, None], (SUBLANES, LANES))
    bias = _band_bias(rel_bias[l])

    h, gt = _prenorm(xs, norm_mix[l][None, :], w_g.astype(BF16))
    na = 2 * M_QK + 2 * M_V
    attn0 = GATE_COL0 + N_GATES
    pa, w_m_bf, w_a_bf = _proj(h, w_in_t, 0, na, BF16, "proj_mlstm", tm=2048,
                               sides=[_cast_side(w_branch_m[l]), _cast_side(w_branch_a[l])])
    mlstm = _mlstm_side(pa, gt, conv_qk_w[l], conv_qk_b[l][None, :], gb_col, m_norm[l][None, :])
    pb, hm, w_out_bf = _proj(h, w_in_t, attn0, 3 * A_W, BF16, "proj_attn_mlstm", tn=768,
                             sides=[mlstm, _cast_side(w_out[l])])
    pc, ha = _proj(h, w_in_t, attn0 + 3 * A_W, 2 * D_MODEL, BF16, "proj_gates_attn",
                   sides=[_attn_side(pb, bias)])

    x1, h2 = _merge(hm, ha, pc, xs, w_m_bf, w_a_bf, w_out_bf, gate_bias[l], norm_ffn[l][None, :])
    a, w_down_bf = _ffn_up(h2, w_up[l], conv_ffn_w[l], conv_ffn_b[l][None, :], w_down[l])
    out = _ffn_down(a, w_down_bf, x1, norm_final[None, :])
    return out[None]
```

```python
import functools
import math
from typing import Callable, NamedTuple

import jax
import jax.numpy as jnp
from jax import lax
from jax.experimental import pallas as pl
from jax.experimental.pallas import tpu as pltpu

D_MODEL = 2048
SEQ = 8192
CHUNK = 64
M_HEADS = 4
M_QK_DIM = 128
M_V_DIM = 256
M_CONV = 4
A_HEADS = 8
A_HEAD_DIM = 128
A_PAST_CHUNKS = 8
A_MAX_REL = 128
D_FF = 5632
FFN_CONV = 3
EPS = 1e-6
NEG = -1e30
LOG2E = math.log2(math.e)

M_QK = M_HEADS * M_QK_DIM
M_V = M_HEADS * M_V_DIM
A_W = A_HEADS * A_HEAD_DIM
GATE_COL0 = 2 * M_QK + 2 * M_V
N_GATES = 2 * M_HEADS

LANES = 128
SUBLANES = 8
MXU_WIDTH = 256
VMEM_LIMIT = 56 * 1024 * 1024

BF16 = jnp.bfloat16
F32 = jnp.float32


def _params(sem):
    return pltpu.CompilerParams(dimension_semantics=sem, vmem_limit_bytes=VMEM_LIMIT)


def _dot(a, b):
    return jnp.dot(a, b, preferred_element_type=F32)


def _dot_nt(a, b):
    return lax.dot_general(a, b, (((1,), (1,)), ((), ())), preferred_element_type=F32)


def _sigmoid(z):
    return 0.5 * jnp.tanh(0.5 * z) + 0.5


def _log_sigmoid(z):
    return jnp.minimum(z, 0.0) - jnp.log(1.0 + jnp.exp(-jnp.abs(z)))


def _rms_scale(y):
    return lax.rsqrt(jnp.mean(y * y, axis=-1, keepdims=True) + EPS)


def _prenorm_kernel(x_ref, g_ref, wg_ref, h_ref, gt_ref):
    xf = x_ref[...]
    hb = (xf * _rms_scale(xf) * g_ref[...]).astype(BF16)
    h_ref[...] = hb
    gcol = _dot(hb, wg_ref[...])
    gt_ref[...] = gcol.T[0:SUBLANES, :]


def _prenorm(x, g, wg, tm=1024):
    s, d = x.shape
    return pl.pallas_call(
        _prenorm_kernel,
        grid=(s // tm,),
        in_specs=[
            pl.BlockSpec((tm, d), lambda i: (i, 0)),
            pl.BlockSpec((1, d), lambda i: (0, 0)),
            pl.BlockSpec((d, LANES), lambda i: (0, 0)),
        ],
        out_specs=[
            pl.BlockSpec((tm, d), lambda i: (i, 0)),
            pl.BlockSpec((SUBLANES, tm), lambda i: (0, i)),
        ],
        out_shape=[
            jax.ShapeDtypeStruct((s, d), BF16),
            jax.ShapeDtypeStruct((SUBLANES, s), F32),
        ],
        compiler_params=_params(("arbitrary",)),
        name="prenorm",
    )(x, g, wg)


class _Spec(NamedTuple):
    block: tuple
    index: Callable
    mode: object = None


class _Side(NamedTuple):
    body: Callable
    init: Callable
    args: tuple
    in_specs: tuple
    out_specs: tuple
    out_shapes: tuple
    scratch: tuple
    steps: int
    clamped: bool = False
    places: tuple = ()


def _proj_kernel(*refs, shift, tn, sides):
    n_w = 3 if shift else 2
    h_ref, wm_ref = refs[0], refs[1]
    pos = n_w
    side_in = []
    for sd in sides:
        side_in.append(refs[pos:pos + len(sd.args)])
        pos += len(sd.args)
    o_ref = refs[pos]
    pos += 1
    side_out = []
    for sd in sides:
        side_out.append(refs[pos:pos + len(sd.out_specs)])
        pos += len(sd.out_specs)
    w_bf = refs[pos]
    pos += 1
    side_scratch = []
    for sd in sides:
        side_scratch.append(refs[pos:pos + len(sd.scratch)])
        pos += len(sd.scratch)

    @pl.when(pl.program_id(1) == 0)
    def _():
        if shift:
            w_bf[0:tn - shift, :] = wm_ref[shift:tn, :].astype(BF16)
            w_bf[tn - shift:tn, :] = refs[2][...].astype(BF16)
        else:
            w_bf[...] = wm_ref[...].astype(BF16)

    step = pl.program_id(0) * pl.num_programs(1) + pl.program_id(1)
    for sd, scr in zip(sides, side_scratch):
        if sd.init:
            sd.init(step, scr)

    placers = [idx for idx, sd in enumerate(sides) if sd.places]
    assert len(placers) <= 1
    widths = sides[placers[0]].places if placers else (tn // 2, tn // 2)
    n_parts = len(widths)
    assert sum(widths) == tn and all(w % MXU_WIDTH == 0 for w in widths)

    def emit(k):
        lo = sum(widths[:k])
        cols = slice(lo, lo + widths[k])
        o_ref[:, cols] = _dot_nt(h_ref[...], w_bf[cols, :]).astype(o_ref.dtype)

    for idx, sd in enumerate(sides):
        if not sd.places:
            sd.body(step, side_in[idx], side_out[idx], side_scratch[idx], None, 0)
    if placers:
        idx = placers[0]
        sides[idx].body(step, side_in[idx], side_out[idx], side_scratch[idx], emit, n_parts)
    else:
        for k in range(n_parts):
            emit(k)


def _proj(h, wt, col0, ncols, out_dtype, name, tm=1024, tn=1024, sides=()):
    s, d = h.shape
    shift = col0 % tn
    base = col0 - shift
    n_m = s // tm
    n_steps = (ncols // tn) * n_m
    assert shift % SUBLANES == 0 and ncols % tn == 0
    assert all(sd.steps == n_steps or (sd.clamped and sd.steps < n_steps) for sd in sides)

    def flat(spec):
        return pl.BlockSpec(spec.block, lambda j, m: spec.index(j * n_m + m),
                            pipeline_mode=spec.mode)

    in_specs = [
        pl.BlockSpec((tm, d), lambda j, m: (m, 0)),
        pl.BlockSpec((tn, d), lambda j, m: (base // tn + j, 0)),
    ]
    args = [h, wt]
    if shift:
        in_specs.append(pl.BlockSpec((shift, d), lambda j, m: ((base + (j + 1) * tn) // shift, 0)))
        args.append(wt)
    out_specs = [pl.BlockSpec((tm, tn), lambda j, m: (m, j))]
    out_shapes = [jax.ShapeDtypeStruct((s, ncols), out_dtype)]
    scratch = [pltpu.VMEM((tn, d), BF16)]
    for sd in sides:
        in_specs += [flat(sp) for sp in sd.in_specs]
        args += list(sd.args)
    for sd in sides:
        out_specs += [flat(sp) for sp in sd.out_specs]
        out_shapes += list(sd.out_shapes)
    for sd in sides:
        scratch += list(sd.scratch)
    return pl.pallas_call(
        functools.partial(_proj_kernel, shift=shift, tn=tn, sides=tuple(sides)),
        grid=(ncols // tn, n_m),
        in_specs=in_specs,
        out_specs=out_specs,
        out_shape=out_shapes,
        scratch_shapes=scratch,
        compiler_params=_params(("arbitrary", "arbitrary")),
        name=name,
    )(*args)


def _cast_body(step, in_refs, out_refs, scratch_refs, emit, n_parts):
    del step, scratch_refs, emit, n_parts
    out_refs[0][...] = in_refs[0][...].astype(BF16)


CAST_ROWS = 128


def _cast_side(w, rows=CAST_ROWS):
    n, d = w.shape
    nb = n // rows
    spec = _Spec((rows, d), lambda i: (jnp.minimum(i, nb - 1), 0))
    return _Side(body=_cast_body, init=None, args=(w,), in_specs=(spec,), out_specs=(spec,),
                 out_shapes=(jax.ShapeDtypeStruct((n, d), BF16),), scratch=(), steps=nb,
                 clamped=True)


MLSTM_BLOCK = 256


def _cumsum_lanes(v):
    n = v.shape[-1]
    pos = lax.broadcasted_iota(jnp.int32, v.shape, v.ndim - 1)
    shift = 1
    while shift < n:
        v = v + jnp.where(pos >= shift, pltpu.roll(v, shift, v.ndim - 1), 0.0)
        shift *= 2
    return v


def _mlstm_body(step, in_refs, out_refs, scratch_refs, emit, n_parts):
    pa_ref, gt_ref, cw_ref, cb_ref, gbc_ref, mn_ref = in_refs
    v_off = 2 * M_QK
    o_off = 2 * M_QK + M_V
    (out_ref,) = out_refs
    qk_buf, c_s, m_s = scratch_refs
    del step
    halo = SUBLANES
    tc = MLSTM_BLOCK

    qk_buf[halo:halo + tc, :] = pa_ref[:, 0:2 * M_QK].astype(F32)
    cw_half = cw_ref[...] * 0.5
    t = jnp.broadcast_to(cb_ref[...] * 0.5, (tc, 2 * M_QK))
    for tap in range(M_CONV):
        lo = halo - (M_CONV - 1) + tap
        t = t + qk_buf[lo:lo + tc, :] * cw_half[tap:tap + 1, :]
    qk_buf[0:halo, :] = qk_buf[tc:tc + halo, :]
    qk = t * (1.0 + jnp.tanh(t))
    q_all = qk[:, 0:M_QK].astype(BF16)
    k_all = qk[:, M_QK:2 * M_QK] * (M_QK_DIM ** -0.5)

    zt = gt_ref[...] + gbc_ref[:, 0:1]
    lf_all = _log_sigmoid(zt)
    b_all = _cumsum_lanes(lf_all)

    row = lax.broadcasted_iota(jnp.int32, (tc, tc), 0)
    col = lax.broadcasted_iota(jnp.int32, (tc, tc), 1)
    causal = col <= row
    ones_aug = jnp.ones((tc, LANES), BF16)

    def inter(h):
        c_prev = c_s[h]
        qc = _dot(q_all[:, h * M_QK_DIM:(h + 1) * M_QK_DIM], c_prev.astype(BF16))
        return c_prev, qc

    def first(h, c_prev, qc):
        qcols = slice(h * M_QK_DIM, (h + 1) * M_QK_DIM)
        f = M_HEADS + h
        q_h = q_all[:, qcols]
        kt = k_all[:, qcols].T
        m_prev = m_s[h][0:1, 0:1]
        u = zt[h:h + 1, :] - b_all[f:f + 1, :]
        b_tot = b_all[f:f + 1, tc - 1:tc]
        v_aug = jnp.concatenate([pa_ref[:, v_off + h * M_V_DIM:v_off + (h + 1) * M_V_DIM],
                                 ones_aug], axis=1)
        qk = _dot(q_h, kt.astype(BF16))
        w_log = b_tot + u
        m_loc = jnp.max(w_log, axis=1, keepdims=True)
        wkt = (kt * jnp.exp(w_log - m_loc)).astype(BF16)
        m_new = jnp.maximum(b_tot + m_prev, m_loc)
        decay = jnp.exp(b_tot + m_prev - m_new)
        gain = jnp.exp(m_loc - m_new)
        m_s[h] = jnp.broadcast_to(m_new, (SUBLANES, LANES))
        return (u, m_prev, v_aug, qk, qc), (wkt, v_aug, c_prev, decay, gain)

    def third(h, wkt, v_aug, c_prev, decay, gain):
        c_s[h] = decay * c_prev + gain * _dot(wkt, v_aug)

    def second(h, u, m_prev, v_aug, qk, qc):
        vcols = slice(h * M_V_DIM, (h + 1) * M_V_DIM)
        f = M_HEADS + h
        um = jnp.where(causal, u, NEG)
        g = jnp.maximum(jnp.max(um, axis=1, keepdims=True), m_prev)
        b_col = jnp.sum(jnp.where(causal, lf_all[f:f + 1, :], 0.0), axis=1, keepdims=True)
        p = jnp.exp(um - g) * qk
        na = _dot(p.astype(BF16), v_aug) + jnp.exp(m_prev - g) * qc
        num = na[:, 0:M_V_DIM]
        den = na[:, M_V_DIM:M_V_DIM + 1]
        hh = num / jnp.maximum(jnp.abs(den), jnp.exp(-(b_col + g)))
        o_gate = pa_ref[:, o_off + h * M_V_DIM:o_off + (h + 1) * M_V_DIM].astype(F32)
        hh = hh * _rms_scale(hh) * mn_ref[0:1, vcols] * _sigmoid(o_gate)
        out_ref[:, vcols] = hh.astype(BF16)

    emit(0)
    parts = [first(h, *inter(h)) for h in range(M_HEADS)]
    groups = n_parts - 1
    per = M_HEADS // groups
    for grp in range(groups):
        emit(1 + grp)
        for h in range(grp * per, (grp + 1) * per):
            second(h, *parts[h][0])
    for h in range(M_HEADS):
        third(h, *parts[h][1])
    assert groups * per == M_HEADS


def _mlstm_init(step, scratch_refs):
    qk_buf, c_s, m_s = scratch_refs

    @pl.when(step == 0)
    def _():
        qk_buf[0:SUBLANES, :] = jnp.zeros((SUBLANES, 2 * M_QK), F32)
        c_s[...] = jnp.zeros(c_s.shape, F32)
        m_s[...] = jnp.full(m_s.shape, NEG, F32)


def _mlstm_side(pa, gt, cw, cb, gbc, mn):
    s = pa.shape[0]
    tc = MLSTM_BLOCK
    const = lambda i: (0, 0)
    return _Side(
        body=_mlstm_body,
        init=_mlstm_init,
        args=(pa, gt, cw, cb, gbc, mn),
        in_specs=(
            _Spec((tc, pa.shape[1]), lambda i: (i, 0)),
            _Spec((SUBLANES, tc), lambda i: (0, i)),
            _Spec((M_CONV, 2 * M_QK), const),
            _Spec((1, 2 * M_QK), const),
            _Spec((SUBLANES, LANES), const),
            _Spec((1, M_V), const),
        ),
        out_specs=(_Spec((tc, M_V), lambda i: (i, 0)),),
        out_shapes=(jax.ShapeDtypeStruct((s, M_V), BF16),),
        scratch=(
            pltpu.VMEM((SUBLANES + tc, 2 * M_QK), F32),
            pltpu.VMEM((M_HEADS, M_QK_DIM, M_V_DIM + LANES), F32),
            pltpu.VMEM((M_HEADS, SUBLANES, LANES), F32),
        ),
        steps=s // tc,
        places=(MXU_WIDTH,) * 3,
    )


ATT_G = 4
ATT_ROWS = ATT_G * CHUNK
ATT_KBLOCKS = (A_PAST_CHUNKS + ATT_G) // ATT_G


def _attn_body(step, in_refs, out_refs, scratch_refs, emit, n_parts):
    del step, scratch_refs
    q_ref, k_ref, v_ref, bias_ref = in_refs
    (out_ref,) = out_refs
    scale2 = (A_HEAD_DIM ** -0.5) * LOG2E
    ones = jnp.ones((ATT_KBLOCKS * ATT_ROWS, A_HEAD_DIM), BF16)

    def scores(h):
        cols = slice(h * A_HEAD_DIM, (h + 1) * A_HEAD_DIM)
        return _dot_nt(q_ref[:, cols], k_ref[:, cols])

    def finish(h, s):
        cols = slice(h * A_HEAD_DIM, (h + 1) * A_HEAD_DIM)
        s = s * scale2 + bias_ref[h]
        p = jnp.exp2(s - jnp.max(s, axis=1, keepdims=True)).astype(BF16)
        o = _dot(p, jnp.concatenate([v_ref[:, cols], ones], axis=1))
        out_ref[:, cols] = (o[:, 0:A_HEAD_DIM] / o[:, A_HEAD_DIM:A_HEAD_DIM + 1]).astype(BF16)

    per = A_HEADS // n_parts
    s_next = [scores(h) for h in range(per)]
    for grp in range(n_parts):
        s_grp = s_next
        if grp + 1 < n_parts:
            s_next = [scores(h) for h in range((grp + 1) * per, (grp + 2) * per)]
        emit(grp)
        for h, s in zip(range(grp * per, (grp + 1) * per), s_grp):
            finish(h, s)


def _attn_side(pb, bias):
    s = pb.shape[0]
    last = ATT_KBLOCKS - 1

    def kv(group):
        return _Spec((pl.Element(ATT_KBLOCKS * ATT_ROWS), pl.Element(A_W)),
                     lambda g: (jnp.maximum(g - last, 0) * ATT_ROWS, group * A_W))

    return _Side(
        body=_attn_body,
        init=None,
        args=(pb,) * 3 + (bias,),
        in_specs=(
            _Spec((ATT_ROWS, A_W), lambda g: (g, 0)),
            kv(1), kv(2),
            _Spec((pl.Element(A_HEADS), pl.Element(ATT_ROWS), pl.Element(ATT_KBLOCKS * ATT_ROWS)),
                  lambda g: (0, 0, jnp.maximum(last - g, 0) * ATT_ROWS)),
        ),
        out_specs=(_Spec((ATT_ROWS, A_W), lambda g: (g, 0)),),
        out_shapes=(jax.ShapeDtypeStruct((s, A_W), BF16),),
        scratch=(),
        steps=s // ATT_ROWS,
        places=(MXU_WIDTH,) * 4,
    )


def _band_bias(rel_table):
    hds = rel_table.shape[0]
    wcols = ATT_KBLOCKS * ATT_ROWS
    band_w = (A_PAST_CHUNKS + 1) * CHUNK
    far = A_PAST_CHUNKS * CHUNK - A_MAX_REL + CHUNK
    tab = rel_table.astype(F32) * LOG2E
    e = jnp.concatenate([jnp.broadcast_to(tab[:, 2 * A_MAX_REL:], (hds, far)),
                         tab[:, A_MAX_REL - CHUNK + 1:2 * A_MAX_REL][:, ::-1],
                         jnp.zeros((hds, 1), F32)], axis=1)
    period = e.shape[1]
    tiled = jnp.broadcast_to(e[:, None, :], (hds, CHUNK, period)).reshape(hds, -1)
    skew = tiled[:, :CHUNK * (period - 1)].reshape(hds, CHUNK, period - 1)
    chunk_bias = skew[:, :, CHUNK - 1:CHUNK - 1 + band_w]
    total = wcols + (ATT_KBLOCKS - 1) * ATT_ROWS
    rows = [jnp.pad(chunk_bias, ((0, 0), (0, 0), (ci * CHUNK, total - band_w - ci * CHUNK)),
                    constant_values=NEG) for ci in range(ATT_G)]
    return jnp.concatenate(rows, axis=1)


def _merge_kernel(hm_ref, ha_ref, gm_ref, ga_ref, x_ref, wm_hbm, wa_hbm, wo_hbm, gb_ref, nf_ref,
                  x1_ref, h2_ref, wm_s, wa_s, wo_s, w_sem):
    copies = (pltpu.make_async_copy(wm_hbm, wm_s, w_sem.at[0]),
              pltpu.make_async_copy(wa_hbm, wa_s, w_sem.at[1]),
              pltpu.make_async_copy(wo_hbm, wo_s, w_sem.at[2]))

    def body(first_step):
        if first_step:
            for cp in copies:
                cp.start()
            copies[0].wait()
        ym = _dot(hm_ref[...], wm_s[...])
        if first_step:
            copies[1].wait()
        ya = _dot(ha_ref[...], wa_s[...])
        merged = (_sigmoid(gm_ref[...].astype(F32) + gb_ref[0:1, :]) * ym
                  + _sigmoid(ga_ref[...].astype(F32) + gb_ref[1:2, :]) * ya)
        if first_step:
            copies[2].wait()
        x1 = x_ref[...] + _dot(merged.astype(BF16), wo_s[...])
        x1_ref[...] = x1
        h2_ref[...] = (x1 * _rms_scale(x1) * nf_ref[...]).astype(BF16)

    @pl.when(pl.program_id(0) == 0)
    def _():
        body(True)

    @pl.when(pl.program_id(0) > 0)
    def _():
        body(False)


def _merge(hm, ha, pc, x, wm, wa, wo, gb, nf, tm=256):
    s, d = x.shape
    const = lambda i: (0, 0)
    in_hbm = pl.BlockSpec(memory_space=pl.ANY)
    return pl.pallas_call(
        _merge_kernel,
        grid=(s // tm,),
        in_specs=[
            pl.BlockSpec((tm, M_V), lambda i: (i, 0)),
            pl.BlockSpec((tm, A_W), lambda i: (i, 0)),
            pl.BlockSpec((tm, d), lambda i: (i, 0)),
            pl.BlockSpec((tm, d), lambda i: (i, 1)),
            pl.BlockSpec((tm, d), lambda i: (i, 0)),
            in_hbm, in_hbm, in_hbm,
            pl.BlockSpec((2, d), const),
            pl.BlockSpec((1, d), const),
        ],
        out_specs=[
            pl.BlockSpec((tm, d), lambda i: (i, 0)),
            pl.BlockSpec((tm, d), lambda i: (i, 0)),
        ],
        out_shape=[
            jax.ShapeDtypeStruct((s, d), F32),
            jax.ShapeDtypeStruct((s, d), BF16),
        ],
        scratch_shapes=[
            pltpu.VMEM((M_V, d), BF16),
            pltpu.VMEM((A_W, d), BF16),
            pltpu.VMEM((d, d), BF16),
            pltpu.SemaphoreType.DMA((3,)),
        ],
        compiler_params=_params(("arbitrary",)),
        name="merge",
    )(hm, ha, pc, pc, x, wm, wa, wo, gb, nf)


FFN_SUB_ROWS = 512


def _ffn_up_kernel(h_ref, wg_ref, wv_ref, cw_ref, cb_ref, wd_ref, a_ref, wd_bf_ref, w_bf, ug_buf,
                   *, tm, tf):
    halo = SUBLANES
    rs = FFN_SUB_ROWS
    wd_bf_ref[...] = wd_ref[...].astype(BF16)

    @pl.when(pl.program_id(1) == 0)
    def _():
        w_bf[:, 0:tf] = wg_ref[...].astype(BF16)
        w_bf[:, tf:2 * tf] = wv_ref[...].astype(BF16)
        ug_buf[0:halo, :] = jnp.zeros((halo, tf), F32)

    for r in range(tm // rs):
        h = h_ref[r * rs:(r + 1) * rs, :]
        u = _dot(h, w_bf[...])
        ug = u[:, 0:tf]
        uv = u[:, tf:2 * tf]
        base = halo + r * rs
        ug_buf[base:base + rs, :] = ug
        conv = cb_ref[...] + ug * cw_ref[FFN_CONV - 1:FFN_CONV, :]
        for tap in range(FFN_CONV - 1):
            lo = base - (FFN_CONV - 1) + tap
            conv = conv + ug_buf[lo:lo + rs, :] * cw_ref[tap:tap + 1, :]
        a_ref[r * rs:(r + 1) * rs, :] = (conv * _sigmoid(conv) * uv).astype(BF16)
    ug_buf[0:halo, :] = ug_buf[tm:tm + halo, :]


def _ffn_up(h2, w_up, cw, cb, w_down, tm=2048, tf=512):
    s, d = h2.shape
    nf = D_FF // tf
    n_m = s // tm
    wd_rows = D_FF // (nf * n_m)
    assert wd_rows * nf * n_m == D_FF and wd_rows % (2 * SUBLANES) == 0
    wd_spec = pl.BlockSpec((wd_rows, d), lambda j, m: (j * n_m + m, 0))
    return pl.pallas_call(
        functools.partial(_ffn_up_kernel, tm=tm, tf=tf),
        grid=(nf, n_m),
        in_specs=[
            pl.BlockSpec((tm, d), lambda j, m: (m, 0)),
            pl.BlockSpec((d, tf), lambda j, m: (0, j)),
            pl.BlockSpec((d, tf), lambda j, m: (0, nf + j)),
            pl.BlockSpec((FFN_CONV, tf), lambda j, m: (0, j)),
            pl.BlockSpec((1, tf), lambda j, m: (0, j)),
            wd_spec,
        ],
        out_specs=[pl.BlockSpec((tm, tf), lambda j, m: (m, j)), wd_spec],
        out_shape=[jax.ShapeDtypeStruct((s, D_FF), BF16),
                   jax.ShapeDtypeStruct((D_FF, d), BF16)],
        scratch_shapes=[
            pltpu.VMEM((d, 2 * tf), BF16),
            pltpu.VMEM((SUBLANES + tm, tf), F32),
        ],
        compiler_params=_params(("arbitrary", "arbitrary")),
        name="ffn_up",
    )(h2, w_up, w_up, cw, cb, w_down)


FFN_DOWN_COLS = 512


def _ffn_down_kernel(a_ref, w_hbm, x1_ref, nf_ref, o_ref, w_s, w_sem):
    d = o_ref.shape[1]
    n_blk = d // FFN_DOWN_COLS

    def w_copy(n):
        cols = slice(n * FFN_DOWN_COLS, (n + 1) * FFN_DOWN_COLS)
        return pltpu.make_async_copy(w_hbm.at[:, cols], w_s.at[n], w_sem.at[n])

    def body(first_step):
        if first_step:
            for n in range(n_blk):
                w_copy(n).start()
        ssq = jnp.zeros((o_ref.shape[0], 1), F32)
        for n in range(n_blk):
            cols = slice(n * FFN_DOWN_COLS, (n + 1) * FFN_DOWN_COLS)
            if first_step:
                w_copy(n).wait()
            y = x1_ref[:, cols] + _dot(a_ref[...], w_s[n])
            o_ref[:, cols] = y
            ssq = ssq + jnp.sum(y * y, axis=-1, keepdims=True)
        o_ref[...] = o_ref[...] * lax.rsqrt(ssq * (1.0 / d) + EPS) * nf_ref[...]

    @pl.when(pl.program_id(0) == 0)
    def _():
        body(True)

    @pl.when(pl.program_id(0) > 0)
    def _():
        body(False)


def _ffn_down(a, w_down, x1, nf, tm=512):
    s, d = x1.shape
    dff = a.shape[1]
    n_blk = d // FFN_DOWN_COLS
    return pl.pallas_call(
        _ffn_down_kernel,
        grid=(s // tm,),
        in_specs=[
            pl.BlockSpec((tm, dff), lambda i: (i, 0)),
            pl.BlockSpec(memory_space=pl.ANY),
            pl.BlockSpec((tm, d), lambda i: (i, 0)),
            pl.BlockSpec((1, d), lambda i: (0, 0)),
        ],
        out_specs=pl.BlockSpec((tm, d), lambda i: (i, 0)),
        out_shape=jax.ShapeDtypeStruct((s, d), F32),
        scratch_shapes=[
            pltpu.VMEM((n_blk, dff, FFN_DOWN_COLS), BF16),
            pltpu.SemaphoreType.DMA((n_blk,)),
        ],
        compiler_params=_params(("arbitrary",)),
        name="ffn_down",
    )(a, w_down, x1, nf)


def kernel(x, norm_mix, w_in, conv_qk_w, conv_qk_b, b_igate, b_fgate, m_norm, rel_bias, gate_bias,
           w_branch_m, w_branch_a, w_out, norm_ffn, w_up, conv_ffn_w, conv_ffn_b, w_down, norm_final):
    batch, seq, d = x.shape
    depth = w_in.shape[0]
    assert (batch, seq, d, depth) == (1, SEQ, D_MODEL, 1)
    xs = x[0]
    l = 0

    w_in_t = jnp.swapaxes(w_in[l], 0, 1)
    w_g = jnp.pad(w_in_t[GATE_COL0:GATE_COL0 + N_GATES, :].T, ((0, 0), (0, LANES - N_GATES)))
    gate_b = jnp.concatenate([b_igate[l], b_fgate[l]])
    gb_col = jnp.broadcast_to(gate_b[:, None], (SUBLANES, LANES))
    bias = _band_bias(rel_bias[l])

    h, gt = _prenorm(xs, norm_mix[l][None, :], w_g.astype(BF16))
    na = 2 * M_QK + 2 * M_V
    attn0 = GATE_COL0 + N_GATES
    pa, w_m_bf, w_a_bf = _proj(h, w_in_t, 0, na, BF16, "proj_mlstm", tm=2048,
                               sides=[_cast_side(w_branch_m[l]), _cast_side(w_branch_a[l])])
    mlstm = _mlstm_side(pa, gt, conv_qk_w[l], conv_qk_b[l][None, :], gb_col, m_norm[l][None, :])
    pb, hm, w_out_bf = _proj(h, w_in_t, attn0, 3 * A_W, BF16, "proj_attn_mlstm", tn=768,
                             sides=[mlstm, _cast_side(w_out[l])])
    pc, ha = _proj(h, w_in_t, attn0 + 3 * A_W, 2 * D_MODEL, BF16, "proj_gates_attn",
                   sides=[_attn_side(pb, bias)])

    x1, h2 = _merge(hm, ha, pc, xs, w_m_bf, w_a_bf, w_out_bf, gate_bias[l], norm_ffn[l][None, :])
    a, w_down_bf = _ffn_up(h2, w_up[l], conv_ffn_w[l], conv_ffn_b[l][None, :], w_down[l])
    out = _ffn_down(a, w_down_bf, x1, norm_final[None, :])
    return out[None]
```

```python
import functools
import math
from typing import Callable, NamedTuple

import jax
import jax.numpy as jnp
from jax import lax
from jax.experimental import pallas as pl
from jax.experimental.pallas import tpu as pltpu

D_MODEL = 2048
SEQ = 8192
CHUNK = 64
M_HEADS = 4
M_QK_DIM = 128
M_V_DIM = 256
M_CONV = 4
A_HEADS = 8
A_HEAD_DIM = 128
A_PAST_CHUNKS = 8
A_MAX_REL = 128
D_FF = 5632
FFN_CONV = 3
EPS = 1e-6
NEG = -1e30
LOG2E = math.log2(math.e)

M_QK = M_HEADS * M_QK_DIM
M_V = M_HEADS * M_V_DIM
A_W = A_HEADS * A_HEAD_DIM
GATE_COL0 = 2 * M_QK + 2 * M_V
N_GATES = 2 * M_HEADS

LANES = 128
SUBLANES = 8
MXU_WIDTH = 256
VMEM_LIMIT = 56 * 1024 * 1024

BF16 = jnp.bfloat16
F32 = jnp.float32


def _params(sem):
    return pltpu.CompilerParams(dimension_semantics=sem, vmem_limit_bytes=VMEM_LIMIT)


def _dot(a, b):
    return jnp.dot(a, b, preferred_element_type=F32)


def _dot_nt(a, b):
    return lax.dot_general(a, b, (((1,), (1,)), ((), ())), preferred_element_type=F32)


def _sigmoid(z):
    return 0.5 * jnp.tanh(0.5 * z) + 0.5


def _log_sigmoid(z):
    return jnp.minimum(z, 0.0) - jnp.log(1.0 + jnp.exp(-jnp.abs(z)))


def _rms_scale(y):
    return lax.rsqrt(jnp.mean(y * y, axis=-1, keepdims=True) + EPS)


def _prenorm_kernel(x_ref, g_ref, wg_ref, h_ref, gt_ref):
    xf = x_ref[...]
    hb = (xf * _rms_scale(xf) * g_ref[...]).astype(BF16)
    h_ref[...] = hb
    gcol = _dot(hb, wg_ref[...])
    gt_ref[...] = gcol.T[0:SUBLANES, :]


def _prenorm(x, g, wg, tm=1024):
    s, d = x.shape
    return pl.pallas_call(
        _prenorm_kernel,
        grid=(s // tm,),
        in_specs=[
            pl.BlockSpec((tm, d), lambda i: (i, 0)),
            pl.BlockSpec((1, d), lambda i: (0, 0)),
            pl.BlockSpec((d, LANES), lambda i: (0, 0)),
        ],
        out_specs=[
            pl.BlockSpec((tm, d), lambda i: (i, 0)),
            pl.BlockSpec((SUBLANES, tm), lambda i: (0, i)),
        ],
        out_shape=[
            jax.ShapeDtypeStruct((s, d), BF16),
            jax.ShapeDtypeStruct((SUBLANES, s), F32),
        ],
        compiler_params=_params(("arbitrary",)),
        name="prenorm",
    )(x, g, wg)


class _Spec(NamedTuple):
    block: tuple
    index: Callable
    mode: object = None


class _Side(NamedTuple):
    body: Callable
    init: Callable
    args: tuple
    in_specs: tuple
    out_specs: tuple
    out_shapes: tuple
    scratch: tuple
    steps: int
    clamped: bool = False
    places: tuple = ()


def _proj_kernel(*refs, shift, tn, sides):
    n_w = 3 if shift else 2
    h_ref, wm_ref = refs[0], refs[1]
    pos = n_w
    side_in = []
    for sd in sides:
        side_in.append(refs[pos:pos + len(sd.args)])
        pos += len(sd.args)
    o_ref = refs[pos]
    pos += 1
    side_out = []
    for sd in sides:
        side_out.append(refs[pos:pos + len(sd.out_specs)])
        pos += len(sd.out_specs)
    w_bf = refs[pos]
    pos += 1
    side_scratch = []
    for sd in sides:
        side_scratch.append(refs[pos:pos + len(sd.scratch)])
        pos += len(sd.scratch)

    @pl.when(pl.program_id(1) == 0)
    def _():
        if shift:
            w_bf[0:tn - shift, :] = wm_ref[shift:tn, :].astype(BF16)
            w_bf[tn - shift:tn, :] = refs[2][...].astype(BF16)
        else:
            w_bf[...] = wm_ref[...].astype(BF16)

    step = pl.program_id(0) * pl.num_programs(1) + pl.program_id(1)
    for sd, scr in zip(sides, side_scratch):
        if sd.init:
            sd.init(step, scr)

    placers = [idx for idx, sd in enumerate(sides) if sd.places]
    assert len(placers) <= 1
    widths = sides[placers[0]].places if placers else (tn // 2, tn // 2)
    n_parts = len(widths)
    assert sum(widths) == tn and all(w % MXU_WIDTH == 0 for w in widths)

    def emit(k):
        lo = sum(widths[:k])
        cols = slice(lo, lo + widths[k])
        o_ref[:, cols] = _dot_nt(h_ref[...], w_bf[cols, :]).astype(o_ref.dtype)

    for idx, sd in enumerate(sides):
        if not sd.places:
            sd.body(step, side_in[idx], side_out[idx], side_scratch[idx], None, 0)
    if placers:
        idx = placers[0]
        sides[idx].body(step, side_in[idx], side_out[idx], side_scratch[idx], emit, n_parts)
    else:
        for k in range(n_parts):
            emit(k)


def _proj(h, wt, col0, ncols, out_dtype, name, tm=1024, tn=1024, sides=()):
    s, d = h.shape
    shift = col0 % tn
    base = col0 - shift
    n_m = s // tm
    n_steps = (ncols // tn) * n_m
    assert shift % SUBLANES == 0 and ncols % tn == 0
    assert all(sd.steps == n_steps or (sd.clamped and sd.steps < n_steps) for sd in sides)

    def flat(spec):
        return pl.BlockSpec(spec.block, lambda j, m: spec.index(j * n_m + m),
                            pipeline_mode=spec.mode)

    in_specs = [
        pl.BlockSpec((tm, d), lambda j, m: (m, 0)),
        pl.BlockSpec((tn, d), lambda j, m: (base // tn + j, 0)),
    ]
    args = [h, wt]
    if shift:
        in_specs.append(pl.BlockSpec((shift, d), lambda j, m: ((base + (j + 1) * tn) // shift, 0)))
        args.append(wt)
    out_specs = [pl.BlockSpec((tm, tn), lambda j, m: (m, j))]
    out_shapes = [jax.ShapeDtypeStruct((s, ncols), out_dtype)]
    scratch = [pltpu.VMEM((tn, d), BF16)]
    for sd in sides:
        in_specs += [flat(sp) for sp in sd.in_specs]
        args += list(sd.args)
    for sd in sides:
        out_specs += [flat(sp) for sp in sd.out_specs]
        out_shapes += list(sd.out_shapes)
    for sd in sides:
        scratch += list(sd.scratch)
    return pl.pallas_call(
        functools.partial(_proj_kernel, shift=shift, tn=tn, sides=tuple(sides)),
        grid=(ncols // tn, n_m),
        in_specs=in_specs,
        out_specs=out_specs,
        out_shape=out_shapes,
        scratch_shapes=scratch,
        compiler_params=_params(("arbitrary", "arbitrary")),
        name=name,
    )(*args)


def _cast_body(step, in_refs, out_refs, scratch_refs, emit, n_parts):
    del step, scratch_refs, emit, n_parts
    out_refs[0][...] = in_refs[0][...].astype(BF16)


CAST_ROWS = 128


def _cast_side(w, rows=CAST_ROWS):
    n, d = w.shape
    nb = n // rows
    spec = _Spec((rows, d), lambda i: (jnp.minimum(i, nb - 1), 0))
    return _Side(body=_cast_body, init=None, args=(w,), in_specs=(spec,), out_specs=(spec,),
                 out_shapes=(jax.ShapeDtypeStruct((n, d), BF16),), scratch=(), steps=nb,
                 clamped=True)


MLSTM_BLOCK = 256


def _cumsum_lanes(v):
    n = v.shape[-1]
    pos = lax.broadcasted_iota(jnp.int32, v.shape, v.ndim - 1)
    shift = 1
    while shift < n:
        v = v + jnp.where(pos >= shift, pltpu.roll(v, shift, v.ndim - 1), 0.0)
        shift *= 2
    return v


def _mlstm_body(step, in_refs, out_refs, scratch_refs, emit, n_parts):
    pa_ref, gt_ref, cw_ref, cb_ref, gbc_ref, mn_ref = in_refs
    v_off = 2 * M_QK
    o_off = 2 * M_QK + M_V
    (out_ref,) = out_refs
    qk_buf, c_s, m_s = scratch_refs
    del step
    halo = SUBLANES
    tc = MLSTM_BLOCK

    qk_buf[halo:halo + tc, :] = pa_ref[:, 0:2 * M_QK].astype(F32)
    cw_half = cw_ref[...] * 0.5
    t = jnp.broadcast_to(cb_ref[...] * 0.5, (tc, 2 * M_QK))
    for tap in range(M_CONV):
        lo = halo - (M_CONV - 1) + tap
        t = t + qk_buf[lo:lo + tc, :] * cw_half[tap:tap + 1, :]
    qk_buf[0:halo, :] = qk_buf[tc:tc + halo, :]
    qk = t * (1.0 + jnp.tanh(t))
    q_all = qk[:, 0:M_QK].astype(BF16)
    k_all = qk[:, M_QK:2 * M_QK] * (M_QK_DIM ** -0.5)

    zt = gt_ref[...] + gbc_ref[:, 0:1]
    lf_all = _log_sigmoid(zt)
    b_all = _cumsum_lanes(lf_all)

    row = lax.broadcasted_iota(jnp.int32, (tc, tc), 0)
    col = lax.broadcasted_iota(jnp.int32, (tc, tc), 1)
    causal = col <= row
    ones_aug = jnp.ones((tc, LANES), BF16)

    def inter(h):
        c_prev = c_s[h]
        qc = _dot(q_all[:, h * M_QK_DIM:(h + 1) * M_QK_DIM], c_prev.astype(BF16))
        return c_prev, qc

    def first(h, c_prev, qc):
        qcols = slice(h * M_QK_DIM, (h + 1) * M_QK_DIM)
        f = M_HEADS + h
        q_h = q_all[:, qcols]
        kt = k_all[:, qcols].T
        m_prev = m_s[h][0:1, 0:1]
        u = zt[h:h + 1, :] - b_all[f:f + 1, :]
        b_tot = b_all[f:f + 1, tc - 1:tc]
        v_aug = jnp.concatenate([pa_ref[:, v_off + h * M_V_DIM:v_off + (h + 1) * M_V_DIM],
                                 ones_aug], axis=1)
        qk = _dot(q_h, kt.astype(BF16))
        w_log = b_tot + u
        m_loc = jnp.max(w_log, axis=1, keepdims=True)
        wkt = (kt * jnp.exp(w_log - m_loc)).astype(BF16)
        m_new = jnp.maximum(b_tot + m_prev, m_loc)
        decay = jnp.exp(b_tot + m_prev - m_new)
        gain = jnp.exp(m_loc - m_new)
        m_s[h] = jnp.broadcast_to(m_new, (SUBLANES, LANES))
        return (u, m_prev, v_aug, qk, qc), (wkt, v_aug, c_prev, decay, gain)

    def third(h, wkt, v_aug, c_prev, decay, gain):
        c_s[h] = decay * c_prev + gain * _dot(wkt, v_aug)

    def second(h, u, m_prev, v_aug, qk, qc):
        vcols = slice(h * M_V_DIM, (h + 1) * M_V_DIM)
        f = M_HEADS + h
        um = jnp.where(causal, u, NEG)
        g = jnp.maximum(jnp.max(um, axis=1, keepdims=True), m_prev)
        b_col = jnp.sum(jnp.where(causal, lf_all[f:f + 1, :], 0.0), axis=1, keepdims=True)
        p = jnp.exp(um - g) * qk
        na = _dot(p.astype(BF16), v_aug) + jnp.exp(m_prev - g) * qc
        num = na[:, 0:M_V_DIM]
        den = na[:, M_V_DIM:M_V_DIM + 1]
        hh = num / jnp.maximum(jnp.abs(den), jnp.exp(-(b_col + g)))
        o_gate = pa_ref[:, o_off + h * M_V_DIM:o_off + (h + 1) * M_V_DIM].astype(F32)
        hh = hh * _rms_scale(hh) * mn_ref[0:1, vcols] * _sigmoid(o_gate)
        out_ref[:, vcols] = hh.astype(BF16)

    emit(0)
    parts = [first(h, *inter(h)) for h in range(M_HEADS)]
    groups = n_parts - 1
    per = M_HEADS // groups
    for grp in range(groups):
        emit(1 + grp)
        for h in range(grp * per, (grp + 1) * per):
            second(h, *parts[h][0])
    for h in range(M_HEADS):
        third(h, *parts[h][1])
    assert groups * per == M_HEADS


def _mlstm_init(step, scratch_refs):
    qk_buf, c_s, m_s = scratch_refs

    @pl.when(step == 0)
    def _():
        qk_buf[0:SUBLANES, :] = jnp.zeros((SUBLANES, 2 * M_QK), F32)
        c_s[...] = jnp.zeros(c_s.shape, F32)
        m_s[...] = jnp.full(m_s.shape, NEG, F32)


def _mlstm_side(pa, gt, cw, cb, gbc, mn):
    s = pa.shape[0]
    tc = MLSTM_BLOCK
    const = lambda i: (0, 0)
    return _Side(
        body=_mlstm_body,
        init=_mlstm_init,
        args=(pa, gt, cw, cb, gbc, mn),
        in_specs=(
            _Spec((tc, pa.shape[1]), lambda i: (i, 0)),
            _Spec((SUBLANES, tc), lambda i: (0, i)),
            _Spec((M_CONV, 2 * M_QK), const),
            _Spec((1, 2 * M_QK), const),
            _Spec((SUBLANES, LANES), const),
            _Spec((1, M_V), const),
        ),
        out_specs=(_Spec((tc, M_V), lambda i: (i, 0)),),
        out_shapes=(jax.ShapeDtypeStruct((s, M_V), BF16),),
        scratch=(
            pltpu.VMEM((SUBLANES + tc, 2 * M_QK), F32),
            pltpu.VMEM((M_HEADS, M_QK_DIM, M_V_DIM + LANES), F32),
            pltpu.VMEM((M_HEADS, SUBLANES, LANES), F32),
        ),
        steps=s // tc,
        places=(MXU_WIDTH,) * 3,
    )


ATT_G = 4
ATT_ROWS = ATT_G * CHUNK
ATT_KBLOCKS = (A_PAST_CHUNKS + ATT_G) // ATT_G


def _attn_body(step, in_refs, out_refs, scratch_refs, emit, n_parts):
    del step, scratch_refs
    q_ref, k_ref, v_ref, bias_ref = in_refs
    (out_ref,) = out_refs
    scale2 = (A_HEAD_DIM ** -0.5) * LOG2E
    ones = jnp.ones((ATT_KBLOCKS * ATT_ROWS, A_HEAD_DIM), BF16)

    def scores(h):
        cols = slice(h * A_HEAD_DIM, (h + 1) * A_HEAD_DIM)
        return _dot_nt(q_ref[:, cols], k_ref[:, cols])

    def finish(h, s):
        cols = slice(h * A_HEAD_DIM, (h + 1) * A_HEAD_DIM)
        s = s * scale2 + bias_ref[h]
        p = jnp.exp2(s - jnp.max(s, axis=1, keepdims=True)).astype(BF16)
        o = _dot(p, jnp.concatenate([v_ref[:, cols], ones], axis=1))
        out_ref[:, cols] = (o[:, 0:A_HEAD_DIM] / o[:, A_HEAD_DIM:A_HEAD_DIM + 1]).astype(BF16)

    per = A_HEADS // n_parts
    s_next = [scores(h) for h in range(per)]
    for grp in range(n_parts):
        s_grp = s_next
        if grp + 1 < n_parts:
            s_next = [scores(h) for h in range((grp + 1) * per, (grp + 2) * per)]
        emit(grp)
        for h, s in zip(range(grp * per, (grp + 1) * per), s_grp):
            finish(h, s)


def _attn_side(pb, bias):
    s = pb.shape[0]
    last = ATT_KBLOCKS - 1

    def kv(group):
        return _Spec((pl.Element(ATT_KBLOCKS * ATT_ROWS), pl.Element(A_W)),
                     lambda g: (jnp.maximum(g - last, 0) * ATT_ROWS, group * A_W))

    return _Side(
        body=_attn_body,
        init=None,
        args=(pb,) * 3 + (bias,),
        in_specs=(
            _Spec((ATT_ROWS, A_W), lambda g: (g, 0)),
            kv(1), kv(2),
            _Spec((pl.Element(A_HEADS), pl.Element(ATT_ROWS), pl.Element(ATT_KBLOCKS * ATT_ROWS)),
                  lambda g: (0, 0, jnp.maximum(last - g, 0) * ATT_ROWS)),
        ),
        out_specs=(_Spec((ATT_ROWS, A_W), lambda g: (g, 0)),),
        out_shapes=(jax.ShapeDtypeStruct((s, A_W), BF16),),
        scratch=(),
        steps=s // ATT_ROWS,
        places=(MXU_WIDTH,) * 4,
    )


def _band_bias(rel_table):
    hds = rel_table.shape[0]
    wcols = ATT_KBLOCKS * ATT_ROWS
    band_w = (A_PAST_CHUNKS + 1) * CHUNK
    far = A_PAST_CHUNKS * CHUNK - A_MAX_REL + CHUNK
    tab = rel_table.astype(F32) * LOG2E
    e = jnp.concatenate([jnp.broadcast_to(tab[:, 2 * A_MAX_REL:], (hds, far)),
                         tab[:, A_MAX_REL - CHUNK + 1:2 * A_MAX_REL][:, ::-1],
                         jnp.zeros((hds, 1), F32)], axis=1)
    period = e.shape[1]
    tiled = jnp.broadcast_to(e[:, None, :], (hds, CHUNK, period)).reshape(hds, -1)
    skew = tiled[:, :CHUNK * (period - 1)].reshape(hds, CHUNK, period - 1)
    chunk_bias = skew[:, :, CHUNK - 1:CHUNK - 1 + band_w]
    total = wcols + (ATT_KBLOCKS - 1) * ATT_ROWS
    rows = [jnp.pad(chunk_bias, ((0, 0), (0, 0), (ci * CHUNK, total - band_w - ci * CHUNK)),
                    constant_values=NEG) for ci in range(ATT_G)]
    return jnp.concatenate(rows, axis=1)


def _merge_kernel(hm_ref, ha_ref, gm_ref, ga_ref, x_ref, wm_hbm, wa_hbm, wo_hbm, gb_ref, nf_ref,
                  x1_ref, h2_ref, wm_s, wa_s, wo_s, w_sem):
    copies = (pltpu.make_async_copy(wm_hbm, wm_s, w_sem.at[0]),
              pltpu.make_async_copy(wa_hbm, wa_s, w_sem.at[1]),
              pltpu.make_async_copy(wo_hbm, wo_s, w_sem.at[2]))

    def body(first_step):
        if first_step:
            for cp in copies:
                cp.start()
            copies[0].wait()
        ym = _dot(hm_ref[...], wm_s[...])
        if first_step:
            copies[1].wait()
        ya = _dot(ha_ref[...], wa_s[...])
        merged = (_sigmoid(gm_ref[...].astype(F32) + gb_ref[0:1, :]) * ym
                  + _sigmoid(ga_ref[...].astype(F32) + gb_ref[1:2, :]) * ya)
        if first_step:
            copies[2].wait()
        x1 = x_ref[...] + _dot(merged.astype(BF16), wo_s[...])
        x1_ref[...] = x1
        h2_ref[...] = (x1 * _rms_scale(x1) * nf_ref[...]).astype(BF16)

    @pl.when(pl.program_id(0) == 0)
    def _():
        body(True)

    @pl.when(pl.program_id(0) > 0)
    def _():
        body(False)


def _merge(hm, ha, pc, x, wm, wa, wo, gb, nf, tm=256):
    s, d = x.shape
    const = lambda i: (0, 0)
    in_hbm = pl.BlockSpec(memory_space=pl.ANY)
    return pl.pallas_call(
        _merge_kernel,
        grid=(s // tm,),
        in_specs=[
            pl.BlockSpec((tm, M_V), lambda i: (i, 0)),
            pl.BlockSpec((tm, A_W), lambda i: (i, 0)),
            pl.BlockSpec((tm, d), lambda i: (i, 0)),
            pl.BlockSpec((tm, d), lambda i: (i, 1)),
            pl.BlockSpec((tm, d), lambda i: (i, 0)),
            in_hbm, in_hbm, in_hbm,
            pl.BlockSpec((2, d), const),
            pl.BlockSpec((1, d), const),
        ],
        out_specs=[
            pl.BlockSpec((tm, d), lambda i: (i, 0)),
            pl.BlockSpec((tm, d), lambda i: (i, 0)),
        ],
        out_shape=[
            jax.ShapeDtypeStruct((s, d), F32),
            jax.ShapeDtypeStruct((s, d), BF16),
        ],
        scratch_shapes=[
            pltpu.VMEM((M_V, d), BF16),
            pltpu.VMEM((A_W, d), BF16),
            pltpu.VMEM((d, d), BF16),
            pltpu.SemaphoreType.DMA((3,)),
        ],
        compiler_params=_params(("arbitrary",)),
        name="merge",
    )(hm, ha, pc, pc, x, wm, wa, wo, gb, nf)


FFN_SUB_ROWS = 512


def _ffn_up_kernel(h_ref, wg_ref, wv_ref, cw_ref, cb_ref, wd_ref, a_ref, wd_bf_ref, w_bf, ug_buf,
                   *, tm, tf):
    halo = SUBLANES
    rs = FFN_SUB_ROWS
    wd_bf_ref[...] = wd_ref[...].astype(BF16)

    @pl.when(pl.program_id(1) == 0)
    def _():
        w_bf[:, 0:tf] = wg_ref[...].astype(BF16)
        w_bf[:, tf:2 * tf] = wv_ref[...].astype(BF16)
        ug_buf[0:halo, :] = jnp.zeros((halo, tf), F32)

    for r in range(tm // rs):
        h = h_ref[r * rs:(r + 1) * rs, :]
        u = _dot(h, w_bf[...])
        ug = u[:, 0:tf]
        uv = u[:, tf:2 * tf]
        base = halo + r * rs
        ug_buf[base:base + rs, :] = ug
        conv = cb_ref[...] + ug * cw_ref[FFN_CONV - 1:FFN_CONV, :]
        for tap in range(FFN_CONV - 1):
            lo = base - (FFN_CONV - 1) + tap
            conv = conv + ug_buf[lo:lo + rs, :] * cw_ref[tap:tap + 1, :]
        a_ref[r * rs:(r + 1) * rs, :] = (conv * _sigmoid(conv) * uv).astype(BF16)
    ug_buf[0:halo, :] = ug_buf[tm:tm + halo, :]


def _ffn_up(h2, w_up, cw, cb, w_down, tm=2048, tf=512):
    s, d = h2.shape
    nf = D_FF // tf
    n_m = s // tm
    wd_rows = D_FF // (nf * n_m)
    assert wd_rows * nf * n_m == D_FF and wd_rows % (2 * SUBLANES) == 0
    wd_spec = pl.BlockSpec((wd_rows, d), lambda j, m: (j * n_m + m, 0))
    return pl.pallas_call(
        functools.partial(_ffn_up_kernel, tm=tm, tf=tf),
        grid=(nf, n_m),
        in_specs=[
            pl.BlockSpec((tm, d), lambda j, m: (m, 0)),
            pl.BlockSpec((d, tf), lambda j, m: (0, j)),
            pl.BlockSpec((d, tf), lambda j, m: (0, nf + j)),
            pl.BlockSpec((FFN_CONV, tf), lambda j, m: (0, j)),
            pl.BlockSpec((1, tf), lambda j, m: (0, j)),
            wd_spec,
        ],
        out_specs=[pl.BlockSpec((tm, tf), lambda j, m: (m, j)), wd_spec],
        out_shape=[jax.ShapeDtypeStruct((s, D_FF), BF16),
                   jax.ShapeDtypeStruct((D_FF, d), BF16)],
        scratch_shapes=[
            pltpu.VMEM((d, 2 * tf), BF16),
            pltpu.VMEM((SUBLANES + tm, tf), F32),
        ],
        compiler_params=_params(("arbitrary", "arbitrary")),
        name="ffn_up",
    )(h2, w_up, w_up, cw, cb, w_down)


FFN_DOWN_COLS = 512


def _ffn_down_kernel(a_ref, w0_ref, w_hbm, x1_ref, nf_ref, o_ref, w_s, w_sem):
    d = o_ref.shape[1]
    n_blk = d // FFN_DOWN_COLS

    def w_copy(n):
        cols = slice(n * FFN_DOWN_COLS, (n + 1) * FFN_DOWN_COLS)
        return pltpu.make_async_copy(w_hbm.at[:, cols], w_s.at[n - 1], w_sem.at[n - 1])

    def body(first_step):
        if first_step:
            for n in range(1, n_blk):
                w_copy(n).start()
        ssq = jnp.zeros((o_ref.shape[0], 1), F32)
        for n in range(n_blk):
            cols = slice(n * FFN_DOWN_COLS, (n + 1) * FFN_DOWN_COLS)
            if first_step and n > 0:
                w_copy(n).wait()
            w_n = w0_ref[...] if n == 0 else w_s[n - 1]
            y = x1_ref[:, cols] + _dot(a_ref[...], w_n)
            o_ref[:, cols] = y
            ssq = ssq + jnp.sum(y * y, axis=-1, keepdims=True)
        o_ref[...] = o_ref[...] * lax.rsqrt(ssq * (1.0 / d) + EPS) * nf_ref[...]

    @pl.when(pl.program_id(0) == 0)
    def _():
        body(True)

    @pl.when(pl.program_id(0) > 0)
    def _():
        body(False)


def _ffn_down(a, w_down, x1, nf, tm=512):
    s, d = x1.shape
    dff = a.shape[1]
    n_blk = d // FFN_DOWN_COLS
    return pl.pallas_call(
        _ffn_down_kernel,
        grid=(s // tm,),
        in_specs=[
            pl.BlockSpec((tm, dff), lambda i: (i, 0)),
            pl.BlockSpec((dff, FFN_DOWN_COLS), lambda i: (0, 0), pipeline_mode=pl.Buffered(1)),
            pl.BlockSpec(memory_space=pl.ANY),
            pl.BlockSpec((tm, d), lambda i: (i, 0)),
            pl.BlockSpec((1, d), lambda i: (0, 0)),
        ],
        out_specs=pl.BlockSpec((tm, d), lambda i: (i, 0)),
        out_shape=jax.ShapeDtypeStruct((s, d), F32),
        scratch_shapes=[
            pltpu.VMEM((n_blk - 1, dff, FFN_DOWN_COLS), BF16),
            pltpu.SemaphoreType.DMA((n_blk - 1,)),
        ],
        compiler_params=_params(("arbitrary",)),
        name="ffn_down",
    )(a, w_down, w_down, x1, nf)


def kernel(x, norm_mix, w_in, conv_qk_w, conv_qk_b, b_igate, b_fgate, m_norm, rel_bias, gate_bias,
           w_branch_m, w_branch_a, w_out, norm_ffn, w_up, conv_ffn_w, conv_ffn_b, w_down, norm_final):
    batch, seq, d = x.shape
    depth = w_in.shape[0]
    assert (batch, seq, d, depth) == (1, SEQ, D_MODEL, 1)
    xs = x[0]
    l = 0

    w_in_t = jnp.swapaxes(w_in[l], 0, 1)
    w_g = jnp.pad(w_in_t[GATE_COL0:GATE_COL0 + N_GATES, :].T, ((0, 0), (0, LANES - N_GATES)))
    gate_b = jnp.concatenate([b_igate[l], b_fgate[l]])
    gb_col = jnp.broadcast_to(gate_b[:, None], (SUBLANES, LANES))
    bias = _band_bias(rel_bias[l])

    h, gt = _prenorm(xs, norm_mix[l][None, :], w_g.astype(BF16))
    na = 2 * M_QK + 2 * M_V
    attn0 = GATE_COL0 + N_GATES
    pa, w_m_bf, w_a_bf = _proj(h, w_in_t, 0, na, BF16, "proj_mlstm", tm=2048,
                               sides=[_cast_side(w_branch_m[l]), _cast_side(w_branch_a[l])])
    mlstm = _mlstm_side(pa, gt, conv_qk_w[l], conv_qk_b[l][None, :], gb_col, m_norm[l][None, :])
    pb, hm, w_out_bf = _proj(h, w_in_t, attn0, 3 * A_W, BF16, "proj_attn_mlstm", tn=768,
                             sides=[mlstm, _cast_side(w_out[l])])
    pc, ha = _proj(h, w_in_t, attn0 + 3 * A_W, 2 * D_MODEL, BF16, "proj_gates_attn",
                   sides=[_attn_side(pb, bias)])

    x1, h2 = _merge(hm, ha, pc, xs, w_m_bf, w_a_bf, w_out_bf, gate_bias[l], norm_ffn[l][None, :])
    a, w_down_bf = _ffn_up(h2, w_up[l], conv_ffn_w[l], conv_ffn_b[l][None, :], w_down[l])
    out = _ffn_down(a, w_down_bf, x1, norm_final[None, :])
    return out[None]
```

```python
import functools
import math
from typing import Callable, NamedTuple

import jax
import jax.numpy as jnp
from jax import lax
from jax.experimental import pallas as pl
from jax.experimental.pallas import tpu as pltpu

D_MODEL = 2048
SEQ = 8192
CHUNK = 64
M_HEADS = 4
M_QK_DIM = 128
M_V_DIM = 256
M_CONV = 4
A_HEADS = 8
A_HEAD_DIM = 128
A_PAST_CHUNKS = 8
A_MAX_REL = 128
D_FF = 5632
FFN_CONV = 3
EPS = 1e-6
NEG = -1e30
LOG2E = math.log2(math.e)

M_QK = M_HEADS * M_QK_DIM
M_V = M_HEADS * M_V_DIM
A_W = A_HEADS * A_HEAD_DIM
GATE_COL0 = 2 * M_QK + 2 * M_V
N_GATES = 2 * M_HEADS

LANES = 128
SUBLANES = 8
MXU_WIDTH = 256
VMEM_LIMIT = 56 * 1024 * 1024

BF16 = jnp.bfloat16
F32 = jnp.float32


def _params(sem):
    return pltpu.CompilerParams(dimension_semantics=sem, vmem_limit_bytes=VMEM_LIMIT)


def _dot(a, b):
    return jnp.dot(a, b, preferred_element_type=F32)


def _dot_nt(a, b):
    return lax.dot_general(a, b, (((1,), (1,)), ((), ())), preferred_element_type=F32)


def _sigmoid(z):
    return 0.5 * jnp.tanh(0.5 * z) + 0.5


def _log_sigmoid(z):
    return jnp.minimum(z, 0.0) - jnp.log(1.0 + jnp.exp(-jnp.abs(z)))


def _rms_scale(y):
    return lax.rsqrt(jnp.mean(y * y, axis=-1, keepdims=True) + EPS)


def _prenorm_kernel(x_ref, g_ref, wg_ref, h_ref, gt_ref):
    xf = x_ref[...]
    hb = (xf * _rms_scale(xf) * g_ref[...]).astype(BF16)
    h_ref[...] = hb
    gcol = _dot(hb, wg_ref[...])
    gt_ref[...] = gcol.T[0:SUBLANES, :]


def _prenorm(x, g, wg, tm=1024):
    s, d = x.shape
    return pl.pallas_call(
        _prenorm_kernel,
        grid=(s // tm,),
        in_specs=[
            pl.BlockSpec((tm, d), lambda i: (i, 0)),
            pl.BlockSpec((1, d), lambda i: (0, 0)),
            pl.BlockSpec((d, LANES), lambda i: (0, 0)),
        ],
        out_specs=[
            pl.BlockSpec((tm, d), lambda i: (i, 0)),
            pl.BlockSpec((SUBLANES, tm), lambda i: (0, i)),
        ],
        out_shape=[
            jax.ShapeDtypeStruct((s, d), BF16),
            jax.ShapeDtypeStruct((SUBLANES, s), F32),
        ],
        compiler_params=_params(("arbitrary",)),
        name="prenorm",
    )(x, g, wg)


class _Spec(NamedTuple):
    block: tuple
    index: Callable
    mode: object = None


class _Side(NamedTuple):
    body: Callable
    init: Callable
    args: tuple
    in_specs: tuple
    out_specs: tuple
    out_shapes: tuple
    scratch: tuple
    steps: int
    clamped: bool = False
    places: tuple = ()


def _proj_kernel(*refs, shift, tn, sides):
    n_w = 3 if shift else 2
    h_ref, wm_ref = refs[0], refs[1]
    pos = n_w
    side_in = []
    for sd in sides:
        side_in.append(refs[pos:pos + len(sd.args)])
        pos += len(sd.args)
    o_ref = refs[pos]
    pos += 1
    side_out = []
    for sd in sides:
        side_out.append(refs[pos:pos + len(sd.out_specs)])
        pos += len(sd.out_specs)
    w_bf = refs[pos]
    pos += 1
    side_scratch = []
    for sd in sides:
        side_scratch.append(refs[pos:pos + len(sd.scratch)])
        pos += len(sd.scratch)

    @pl.when(pl.program_id(1) == 0)
    def _():
        if shift:
            w_bf[0:tn - shift, :] = wm_ref[shift:tn, :].astype(BF16)
            w_bf[tn - shift:tn, :] = refs[2][...].astype(BF16)
        else:
            w_bf[...] = wm_ref[...].astype(BF16)

    step = pl.program_id(0) * pl.num_programs(1) + pl.program_id(1)
    for sd, scr in zip(sides, side_scratch):
        if sd.init:
            sd.init(step, scr)

    placers = [idx for idx, sd in enumerate(sides) if sd.places]
    assert len(placers) <= 1
    widths = sides[placers[0]].places if placers else (tn // 2, tn // 2)
    n_parts = len(widths)
    assert sum(widths) == tn and all(w % MXU_WIDTH == 0 for w in widths)

    def emit(k):
        lo = sum(widths[:k])
        cols = slice(lo, lo + widths[k])
        o_ref[:, cols] = _dot_nt(h_ref[...], w_bf[cols, :]).astype(o_ref.dtype)

    for idx, sd in enumerate(sides):
        if not sd.places:
            sd.body(step, side_in[idx], side_out[idx], side_scratch[idx], None, 0)
    if placers:
        idx = placers[0]
        sides[idx].body(step, side_in[idx], side_out[idx], side_scratch[idx], emit, n_parts)
    else:
        for k in range(n_parts):
            emit(k)


def _proj(h, wt, col0, ncols, out_dtype, name, tm=1024, tn=1024, sides=()):
    s, d = h.shape
    shift = col0 % tn
    base = col0 - shift
    n_m = s // tm
    n_steps = (ncols // tn) * n_m
    assert shift % SUBLANES == 0 and ncols % tn == 0
    assert all(sd.steps == n_steps or (sd.clamped and sd.steps < n_steps) for sd in sides)

    def flat(spec):
        return pl.BlockSpec(spec.block, lambda j, m: spec.index(j * n_m + m),
                            pipeline_mode=spec.mode)

    in_specs = [
        pl.BlockSpec((tm, d), lambda j, m: (m, 0)),
        pl.BlockSpec((tn, d), lambda j, m: (base // tn + j, 0)),
    ]
    args = [h, wt]
    if shift:
        in_specs.append(pl.BlockSpec((shift, d), lambda j, m: ((base + (j + 1) * tn) // shift, 0)))
        args.append(wt)
    out_specs = [pl.BlockSpec((tm, tn), lambda j, m: (m, j))]
    out_shapes = [jax.ShapeDtypeStruct((s, ncols), out_dtype)]
    scratch = [pltpu.VMEM((tn, d), BF16)]
    for sd in sides:
        in_specs += [flat(sp) for sp in sd.in_specs]
        args += list(sd.args)
    for sd in sides:
        out_specs += [flat(sp) for sp in sd.out_specs]
        out_shapes += list(sd.out_shapes)
    for sd in sides:
        scratch += list(sd.scratch)
    return pl.pallas_call(
        functools.partial(_proj_kernel, shift=shift, tn=tn, sides=tuple(sides)),
        grid=(ncols // tn, n_m),
        in_specs=in_specs,
        out_specs=out_specs,
        out_shape=out_shapes,
        scratch_shapes=scratch,
        compiler_params=_params(("arbitrary", "arbitrary")),
        name=name,
    )(*args)


def _cast_body(step, in_refs, out_refs, scratch_refs, emit, n_parts):
    del step, scratch_refs, emit, n_parts
    out_refs[0][...] = in_refs[0][...].astype(BF16)


CAST_ROWS = 128


def _cast_side(w, rows=CAST_ROWS):
    n, d = w.shape
    nb = n // rows
    spec = _Spec((rows, d), lambda i: (jnp.minimum(i, nb - 1), 0))
    return _Side(body=_cast_body, init=None, args=(w,), in_specs=(spec,), out_specs=(spec,),
                 out_shapes=(jax.ShapeDtypeStruct((n, d), BF16),), scratch=(), steps=nb,
                 clamped=True)


MLSTM_BLOCK = 256


def _cumsum_lanes(v):
    n = v.shape[-1]
    pos = lax.broadcasted_iota(jnp.int32, v.shape, v.ndim - 1)
    shift = 1
    while shift < n:
        v = v + jnp.where(pos >= shift, pltpu.roll(v, shift, v.ndim - 1), 0.0)
        shift *= 2
    return v


def _mlstm_body(step, in_refs, out_refs, scratch_refs, emit, n_parts):
    pa_ref, gt_ref, cw_ref, cb_ref, gbc_ref, mn_ref = in_refs
    v_off = 2 * M_QK
    o_off = 2 * M_QK + M_V
    (out_ref,) = out_refs
    qk_buf, c_s, m_s = scratch_refs
    del step
    halo = SUBLANES
    tc = MLSTM_BLOCK

    qk_buf[halo:halo + tc, :] = pa_ref[:, 0:2 * M_QK].astype(F32)
    cw_half = cw_ref[...] * 0.5
    t = jnp.broadcast_to(cb_ref[...] * 0.5, (tc, 2 * M_QK))
    for tap in range(M_CONV):
        lo = halo - (M_CONV - 1) + tap
        t = t + qk_buf[lo:lo + tc, :] * cw_half[tap:tap + 1, :]
    qk_buf[0:halo, :] = qk_buf[tc:tc + halo, :]
    qk = t * (1.0 + jnp.tanh(t))
    q_all = qk[:, 0:M_QK].astype(BF16)
    k_all = qk[:, M_QK:2 * M_QK] * (M_QK_DIM ** -0.5)

    zt = gt_ref[...] + gbc_ref[:, 0:1]
    lf_all = _log_sigmoid(zt)
    b_all = _cumsum_lanes(lf_all)

    row = lax.broadcasted_iota(jnp.int32, (tc, tc), 0)
    col = lax.broadcasted_iota(jnp.int32, (tc, tc), 1)
    causal = col <= row
    ones_aug = jnp.ones((tc, LANES), BF16)

    def inter(h):
        c_prev = c_s[h]
        qc = _dot(q_all[:, h * M_QK_DIM:(h + 1) * M_QK_DIM], c_prev.astype(BF16))
        return c_prev, qc

    def first(h, c_prev, qc):
        qcols = slice(h * M_QK_DIM, (h + 1) * M_QK_DIM)
        f = M_HEADS + h
        q_h = q_all[:, qcols]
        kt = k_all[:, qcols].T
        m_prev = m_s[h][0:1, 0:1]
        u = zt[h:h + 1, :] - b_all[f:f + 1, :]
        b_tot = b_all[f:f + 1, tc - 1:tc]
        v_aug = jnp.concatenate([pa_ref[:, v_off + h * M_V_DIM:v_off + (h + 1) * M_V_DIM],
                                 ones_aug], axis=1)
        qk = _dot(q_h, kt.astype(BF16))
        w_log = b_tot + u
        m_loc = jnp.max(w_log, axis=1, keepdims=True)
        wkt = (kt * jnp.exp(w_log - m_loc)).astype(BF16)
        m_new = jnp.maximum(b_tot + m_prev, m_loc)
        decay = jnp.exp(b_tot + m_prev - m_new)
        gain = jnp.exp(m_loc - m_new)
        m_s[h] = jnp.broadcast_to(m_new, (SUBLANES, LANES))
        return (u, m_prev, v_aug, qk, qc), (wkt, v_aug, c_prev, decay, gain)

    def third(h, wkt, v_aug, c_prev, decay, gain):
        c_s[h] = decay * c_prev + gain * _dot(wkt, v_aug)

    def second(h, u, m_prev, v_aug, qk, qc):
        vcols = slice(h * M_V_DIM, (h + 1) * M_V_DIM)
        f = M_HEADS + h
        um = jnp.where(causal, u, NEG)
        g = jnp.maximum(jnp.max(um, axis=1, keepdims=True), m_prev)
        b_col = jnp.sum(jnp.where(causal, lf_all[f:f + 1, :], 0.0), axis=1, keepdims=True)
        p = jnp.exp(um - g) * qk
        na = _dot(p.astype(BF16), v_aug) + jnp.exp(m_prev - g) * qc
        num = na[:, 0:M_V_DIM]
        den = na[:, M_V_DIM:M_V_DIM + 1]
        hh = num / jnp.maximum(jnp.abs(den), jnp.exp(-(b_col + g)))
        o_gate = pa_ref[:, o_off + h * M_V_DIM:o_off + (h + 1) * M_V_DIM].astype(F32)
        hh = hh * _rms_scale(hh) * mn_ref[0:1, vcols] * _sigmoid(o_gate)
        out_ref[:, vcols] = hh.astype(BF16)

    emit(0)
    parts = [first(h, *inter(h)) for h in range(M_HEADS)]
    groups = n_parts - 1
    per = M_HEADS // groups
    for grp in range(groups):
        emit(1 + grp)
        for h in range(grp * per, (grp + 1) * per):
            second(h, *parts[h][0])
    for h in range(M_HEADS):
        third(h, *parts[h][1])
    assert groups * per == M_HEADS


def _mlstm_init(step, scratch_refs):
    qk_buf, c_s, m_s = scratch_refs

    @pl.when(step == 0)
    def _():
        qk_buf[0:SUBLANES, :] = jnp.zeros((SUBLANES, 2 * M_QK), F32)
        c_s[...] = jnp.zeros(c_s.shape, F32)
        m_s[...] = jnp.full(m_s.shape, NEG, F32)


def _mlstm_side(pa, gt, cw, cb, gbc, mn):
    s = pa.shape[0]
    tc = MLSTM_BLOCK
    const = lambda i: (0, 0)
    return _Side(
        body=_mlstm_body,
        init=_mlstm_init,
        args=(pa, gt, cw, cb, gbc, mn),
        in_specs=(
            _Spec((tc, pa.shape[1]), lambda i: (i, 0)),
            _Spec((SUBLANES, tc), lambda i: (0, i)),
            _Spec((M_CONV, 2 * M_QK), const),
            _Spec((1, 2 * M_QK), const),
            _Spec((SUBLANES, LANES), const),
            _Spec((1, M_V), const),
        ),
        out_specs=(_Spec((tc, M_V), lambda i: (i, 0)),),
        out_shapes=(jax.ShapeDtypeStruct((s, M_V), BF16),),
        scratch=(
            pltpu.VMEM((SUBLANES + tc, 2 * M_QK), F32),
            pltpu.VMEM((M_HEADS, M_QK_DIM, M_V_DIM + LANES), F32),
            pltpu.VMEM((M_HEADS, SUBLANES, LANES), F32),
        ),
        steps=s // tc,
        places=(MXU_WIDTH,) * 3,
    )


ATT_G = 4
ATT_ROWS = ATT_G * CHUNK
ATT_KBLOCKS = (A_PAST_CHUNKS + ATT_G) // ATT_G


def _attn_body(step, in_refs, out_refs, scratch_refs, emit, n_parts):
    del step, scratch_refs
    q_ref, k_ref, v_ref, bias_ref = in_refs
    (out_ref,) = out_refs
    scale2 = (A_HEAD_DIM ** -0.5) * LOG2E
    ones = jnp.ones((ATT_KBLOCKS * ATT_ROWS, A_HEAD_DIM), BF16)

    def scores(h):
        cols = slice(h * A_HEAD_DIM, (h + 1) * A_HEAD_DIM)
        return _dot_nt(q_ref[:, cols], k_ref[:, cols])

    def finish(h, s):
        cols = slice(h * A_HEAD_DIM, (h + 1) * A_HEAD_DIM)
        s = s * scale2 + bias_ref[h]
        p = jnp.exp2(s - jnp.max(s, axis=1, keepdims=True)).astype(BF16)
        o = _dot(p, jnp.concatenate([v_ref[:, cols], ones], axis=1))
        out_ref[:, cols] = (o[:, 0:A_HEAD_DIM] / o[:, A_HEAD_DIM:A_HEAD_DIM + 1]).astype(BF16)

    per = A_HEADS // n_parts
    s_next = [scores(h) for h in range(per)]
    for grp in range(n_parts):
        s_grp = s_next
        if grp + 1 < n_parts:
            s_next = [scores(h) for h in range((grp + 1) * per, (grp + 2) * per)]
        emit(grp)
        for h, s in zip(range(grp * per, (grp + 1) * per), s_grp):
            finish(h, s)


def _attn_side(pb, bias):
    s = pb.shape[0]
    last = ATT_KBLOCKS - 1

    def kv(group):
        return _Spec((pl.Element(ATT_KBLOCKS * ATT_ROWS), pl.Element(A_W)),
                     lambda g: (jnp.maximum(g - last, 0) * ATT_ROWS, group * A_W))

    return _Side(
        body=_attn_body,
        init=None,
        args=(pb,) * 3 + (bias,),
        in_specs=(
            _Spec((ATT_ROWS, A_W), lambda g: (g, 0)),
            kv(1), kv(2),
            _Spec((pl.Element(A_HEADS), pl.Element(ATT_ROWS), pl.Element(ATT_KBLOCKS * ATT_ROWS)),
                  lambda g: (0, 0, jnp.maximum(last - g, 0) * ATT_ROWS)),
        ),
        out_specs=(_Spec((ATT_ROWS, A_W), lambda g: (g, 0)),),
        out_shapes=(jax.ShapeDtypeStruct((s, A_W), BF16),),
        scratch=(),
        steps=s // ATT_ROWS,
        places=(MXU_WIDTH,) * 4,
    )


def _band_bias(rel_table):
    hds = rel_table.shape[0]
    wcols = ATT_KBLOCKS * ATT_ROWS
    band_w = (A_PAST_CHUNKS + 1) * CHUNK
    far = A_PAST_CHUNKS * CHUNK - A_MAX_REL + CHUNK
    tab = rel_table.astype(F32) * LOG2E
    e = jnp.concatenate([jnp.broadcast_to(tab[:, 2 * A_MAX_REL:], (hds, far)),
                         tab[:, A_MAX_REL - CHUNK + 1:2 * A_MAX_REL][:, ::-1],
                         jnp.zeros((hds, 1), F32)], axis=1)
    period = e.shape[1]
    tiled = jnp.broadcast_to(e[:, None, :], (hds, CHUNK, period)).reshape(hds, -1)
    skew = tiled[:, :CHUNK * (period - 1)].reshape(hds, CHUNK, period - 1)
    chunk_bias = skew[:, :, CHUNK - 1:CHUNK - 1 + band_w]
    total = wcols + (ATT_KBLOCKS - 1) * ATT_ROWS
    rows = [jnp.pad(chunk_bias, ((0, 0), (0, 0), (ci * CHUNK, total - band_w - ci * CHUNK)),
                    constant_values=NEG) for ci in range(ATT_G)]
    return jnp.concatenate(rows, axis=1)


def _merge_kernel(hm_ref, ha_ref, gm_ref, ga_ref, x_ref, wm_hbm, wa_hbm, wo_hbm, gb_ref, nf_ref,
                  x1_ref, h2_ref, wm_s, wa_s, wo_s, w_sem):
    copies = (pltpu.make_async_copy(wm_hbm, wm_s, w_sem.at[0]),
              pltpu.make_async_copy(wa_hbm, wa_s, w_sem.at[1]),
              pltpu.make_async_copy(wo_hbm, wo_s, w_sem.at[2]))

    def body(first_step):
        if first_step:
            for cp in copies:
                cp.start()
            copies[0].wait()
        ym = _dot(hm_ref[...], wm_s[...])
        if first_step:
            copies[1].wait()
        ya = _dot(ha_ref[...], wa_s[...])
        merged = (_sigmoid(gm_ref[...].astype(F32) + gb_ref[0:1, :]) * ym
                  + _sigmoid(ga_ref[...].astype(F32) + gb_ref[1:2, :]) * ya)
        if first_step:
            copies[2].wait()
        x1 = x_ref[...] + _dot(merged.astype(BF16), wo_s[...])
        x1_ref[...] = x1
        h2_ref[...] = (x1 * _rms_scale(x1) * nf_ref[...]).astype(BF16)

    @pl.when(pl.program_id(0) == 0)
    def _():
        body(True)

    @pl.when(pl.program_id(0) > 0)
    def _():
        body(False)


def _merge(hm, ha, pc, x, wm, wa, wo, gb, nf, tm=256):
    s, d = x.shape
    const = lambda i: (0, 0)
    in_hbm = pl.BlockSpec(memory_space=pl.ANY)
    return pl.pallas_call(
        _merge_kernel,
        grid=(s // tm,),
        in_specs=[
            pl.BlockSpec((tm, M_V), lambda i: (i, 0)),
            pl.BlockSpec((tm, A_W), lambda i: (i, 0)),
            pl.BlockSpec((tm, d), lambda i: (i, 0)),
            pl.BlockSpec((tm, d), lambda i: (i, 1)),
            pl.BlockSpec((tm, d), lambda i: (i, 0)),
            in_hbm, in_hbm, in_hbm,
            pl.BlockSpec((2, d), const),
            pl.BlockSpec((1, d), const),
        ],
        out_specs=[
            pl.BlockSpec((tm, d), lambda i: (i, 0)),
            pl.BlockSpec((tm, d), lambda i: (i, 0)),
        ],
        out_shape=[
            jax.ShapeDtypeStruct((s, d), F32),
            jax.ShapeDtypeStruct((s, d), BF16),
        ],
        scratch_shapes=[
            pltpu.VMEM((M_V, d), BF16),
            pltpu.VMEM((A_W, d), BF16),
            pltpu.VMEM((d, d), BF16),
            pltpu.SemaphoreType.DMA((3,)),
        ],
        compiler_params=_params(("arbitrary",)),
        name="merge",
    )(hm, ha, pc, pc, x, wm, wa, wo, gb, nf)


FFN_SUB_ROWS = 512


def _ffn_up_kernel(h_ref, wg_ref, wv_ref, cw_ref, cb_ref, wd_ref, a_ref, wd_bf_ref, w_bf, ug_buf,
                   *, tm, tf):
    halo = SUBLANES
    rs = FFN_SUB_ROWS
    wd_bf_ref[...] = wd_ref[...].astype(BF16)

    @pl.when(pl.program_id(1) == 0)
    def _():
        w_bf[:, 0:tf] = wg_ref[...].astype(BF16)
        w_bf[:, tf:2 * tf] = wv_ref[...].astype(BF16)
        ug_buf[0:halo, :] = jnp.zeros((halo, tf), F32)

    for r in range(tm // rs):
        h = h_ref[r * rs:(r + 1) * rs, :]
        u = _dot(h, w_bf[...])
        ug = u[:, 0:tf]
        uv = u[:, tf:2 * tf]
        base = halo + r * rs
        ug_buf[base:base + rs, :] = ug
        conv = cb_ref[...] + ug * cw_ref[FFN_CONV - 1:FFN_CONV, :]
        for tap in range(FFN_CONV - 1):
            lo = base - (FFN_CONV - 1) + tap
            conv = conv + ug_buf[lo:lo + rs, :] * cw_ref[tap:tap + 1, :]
        a_ref[r * rs:(r + 1) * rs, :] = (conv * _sigmoid(conv) * uv).astype(BF16)
    ug_buf[0:halo, :] = ug_buf[tm:tm + halo, :]


def _ffn_up(h2, w_up, cw, cb, w_down, tm=2048, tf=512):
    s, d = h2.shape
    nf = D_FF // tf
    n_m = s // tm
    wd_rows = D_FF // (nf * n_m)
    assert wd_rows * nf * n_m == D_FF and wd_rows % (2 * SUBLANES) == 0
    wd_spec = pl.BlockSpec((wd_rows, d), lambda j, m: (j * n_m + m, 0))
    return pl.pallas_call(
        functools.partial(_ffn_up_kernel, tm=tm, tf=tf),
        grid=(nf, n_m),
        in_specs=[
            pl.BlockSpec((tm, d), lambda j, m: (m, 0)),
            pl.BlockSpec((d, tf), lambda j, m: (0, j)),
            pl.BlockSpec((d, tf), lambda j, m: (0, nf + j)),
            pl.BlockSpec((FFN_CONV, tf), lambda j, m: (0, j)),
            pl.BlockSpec((1, tf), lambda j, m: (0, j)),
            wd_spec,
        ],
        out_specs=[pl.BlockSpec((tm, tf), lambda j, m: (m, j)), wd_spec],
        out_shape=[jax.ShapeDtypeStruct((s, D_FF), BF16),
                   jax.ShapeDtypeStruct((D_FF, d), BF16)],
        scratch_shapes=[
            pltpu.VMEM((d, 2 * tf), BF16),
            pltpu.VMEM((SUBLANES + tm, tf), F32),
        ],
        compiler_params=_params(("arbitrary", "arbitrary")),
        name="ffn_up",
    )(h2, w_up, w_up, cw, cb, w_down)


FFN_DOWN_COLS = 1024


def _ffn_down_kernel(a_ref, w_hbm, x1_ref, nf_ref, o_ref, w_s, w_sem):
    d = o_ref.shape[1]
    n_blk = d // FFN_DOWN_COLS

    def w_copy(n):
        cols = slice(n * FFN_DOWN_COLS, (n + 1) * FFN_DOWN_COLS)
        return pltpu.make_async_copy(w_hbm.at[:, cols], w_s.at[n], w_sem.at[n])

    def body(first_step):
        if first_step:
            for n in range(n_blk):
                w_copy(n).start()
        ssq = jnp.zeros((o_ref.shape[0], 1), F32)
        for n in range(n_blk):
            cols = slice(n * FFN_DOWN_COLS, (n + 1) * FFN_DOWN_COLS)
            if first_step:
                w_copy(n).wait()
            y = x1_ref[:, cols] + _dot(a_ref[...], w_s[n])
            o_ref[:, cols] = y
            ssq = ssq + jnp.sum(y * y, axis=-1, keepdims=True)
        o_ref[...] = o_ref[...] * lax.rsqrt(ssq * (1.0 / d) + EPS) * nf_ref[...]

    @pl.when(pl.program_id(0) == 0)
    def _():
        body(True)

    @pl.when(pl.program_id(0) > 0)
    def _():
        body(False)


def _ffn_down(a, w_down, x1, nf, tm=512):
    s, d = x1.shape
    dff = a.shape[1]
    n_blk = d // FFN_DOWN_COLS
    return pl.pallas_call(
        _ffn_down_kernel,
        grid=(s // tm,),
        in_specs=[
            pl.BlockSpec((tm, dff), lambda i: (i, 0)),
            pl.BlockSpec(memory_space=pl.ANY),
            pl.BlockSpec((tm, d), lambda i: (i, 0)),
            pl.BlockSpec((1, d), lambda i: (0, 0)),
        ],
        out_specs=pl.BlockSpec((tm, d), lambda i: (i, 0)),
        out_shape=jax.ShapeDtypeStruct((s, d), F32),
        scratch_shapes=[
            pltpu.VMEM((n_blk, dff, FFN_DOWN_COLS), BF16),
            pltpu.SemaphoreType.DMA((n_blk,)),
        ],
        compiler_params=_params(("arbitrary",)),
        name="ffn_down",
    )(a, w_down, x1, nf)


def kernel(x, norm_mix, w_in, conv_qk_w, conv_qk_b, b_igate, b_fgate, m_norm, rel_bias, gate_bias,
           w_branch_m, w_branch_a, w_out, norm_ffn, w_up, conv_ffn_w, conv_ffn_b, w_down, norm_final):
    batch, seq, d = x.shape
    depth = w_in.shape[0]
    assert (batch, seq, d, depth) == (1, SEQ, D_MODEL, 1)
    xs = x[0]
    l = 0

    w_in_t = jnp.swapaxes(w_in[l], 0, 1)
    w_g = jnp.pad(w_in_t[GATE_COL0:GATE_COL0 + N_GATES, :].T, ((0, 0), (0, LANES - N_GATES)))
    gate_b = jnp.concatenate([b_igate[l], b_fgate[l]])
    gb_col = jnp.broadcast_to(gate_b[:, None], (SUBLANES, LANES))
    bias = _band_bias(rel_bias[l])

    h, gt = _prenorm(xs, norm_mix[l][None, :], w_g.astype(BF16))
    na = 2 * M_QK + 2 * M_V
    attn0 = GATE_COL0 + N_GATES
    pa, w_m_bf, w_a_bf = _proj(h, w_in_t, 0, na, BF16, "proj_mlstm", tm=2048,
                               sides=[_cast_side(w_branch_m[l]), _cast_side(w_branch_a[l])])
    mlstm = _mlstm_side(pa, gt, conv_qk_w[l], conv_qk_b[l][None, :], gb_col, m_norm[l][None, :])
    pb, hm, w_out_bf = _proj(h, w_in_t, attn0, 3 * A_W, BF16, "proj_attn_mlstm", tn=768,
                             sides=[mlstm, _cast_side(w_out[l])])
    pc, ha = _proj(h, w_in_t, attn0 + 3 * A_W, 2 * D_MODEL, BF16, "proj_gates_attn",
                   sides=[_attn_side(pb, bias)])

    x1, h2 = _merge(hm, ha, pc, xs, w_m_bf, w_a_bf, w_out_bf, gate_bias[l], norm_ffn[l][None, :])
    a, w_down_bf = _ffn_up(h2, w_up[l], conv_ffn_w[l], conv_ffn_b[l][None, :], w_down[l])
    out = _ffn_down(a, w_down_bf, x1, norm_final[None, :])
    return out[None]
```
